```python
import math
import jax, jax.numpy as jnp
from jax import lax
import numpy as np

D_MODEL = 4096
BATCH = 1
SEQ = 8192
DEPTH = 1
DEC_BATCH = 32
DEC_SEQ = 4
PAST_LEN = 8192
PAGE_SIZE = 128

M_HEADS = 8
M_DQK = D_MODEL // 16
M_DV = D_MODEL // 8
M_CHUNK = 128
A_GROUPS = ((128, 1), (512, 4), (2048, 16))
A_HEADS = 8
A_DH = 128
A_STEPS = 128
A_BLOCK = A_STEPS
P_HEADS = 8
P_NKEYS = 128
P_NEXP = P_NKEYS * P_NKEYS
P_DKEY = 256
P_TOPK = 16
P_TOKEN_BLOCK = 64
ALPHA = (2 * DEPTH) ** 0.25
BETA = (8 * DEPTH) ** -0.25
LN_EPS = 1e-5
A_WIDTH = len(A_GROUPS) * A_HEADS * A_DH
SPLITS = [M_HEADS * M_DQK, M_HEADS * M_DQK, M_HEADS * M_DV, M_HEADS * M_DV, M_HEADS, M_HEADS,
          A_WIDTH, A_WIDTH, A_WIDTH, D_MODEL, D_MODEL]
SPLIT_SCALES = [1.0, 1.0, BETA, 1.0, 1.0, 1.0, 1.0, 1.0, BETA, 1.0, 1.0]
D_IN = sum(SPLITS)

kernel_name = 'hybrid_mlstm_dilated_peer_decoder_step'


def layer_norm(x, g, b):
    xf = x.astype(jnp.float32)
    mu = xf.mean(-1, keepdims=True)
    var = jnp.square(xf - mu).mean(-1, keepdims=True)
    return ((xf - mu) * lax.rsqrt(var + LN_EPS) * g.astype(jnp.float32) + b.astype(jnp.float32)).astype(x.dtype)


def alibi_slopes():
    n = len(A_GROUPS) * A_HEADS
    return (2.0 ** (-8.0 * (jnp.arange(n, dtype=jnp.float32) + 1.0) / n)).reshape(len(A_GROUPS), A_HEADS)


def mlstm_chunk_step(carry, inp):
    C, n, m = carry
    q, k, v, ig, lf = inp
    L = q.shape[2]
    F = jnp.cumsum(lf, axis=-1)
    causal = jnp.tril(jnp.ones((L, L), dtype=bool))
    D = jnp.where(causal, F[..., :, None] - F[..., None, :] + ig[..., None, :], -jnp.inf)
    inter = F + m[..., None]
    m_t = jnp.maximum(inter, D.max(-1))
    w_intra = jnp.exp(D - m_t[..., None])
    w_inter = jnp.exp(inter - m_t)
    s = jnp.einsum('bhtd,bhsd->bhts', q, k) * w_intra
    num = w_inter[..., None] * jnp.einsum('bhtd,bhde->bhte', q, C) + jnp.einsum('bhts,bhse->bhte', s, v)
    den = w_inter * jnp.einsum('bhtd,bhd->bht', q, n) + s.sum(-1)
    h = num / jnp.maximum(jnp.abs(den), jnp.exp(-m_t))[..., None]
    m_end = m_t[..., -1]
    w_state = jnp.exp(F[..., -1] + m - m_end)
    w_tok = jnp.exp(F[..., -1:] - F + ig - m_end[..., None])
    kw = k * w_tok[..., None]
    C_new = w_state[..., None, None] * C + jnp.einsum('bhsd,bhse->bhde', kw, v)
    n_new = w_state[..., None] * n + kw.sum(2)
    return (C_new, n_new, m_end), h


def mlstm_scan(q, k, v, ig, lf, C0, n0, m0, chunk):
    B, H, S = ig.shape
    nc = S // chunk

    def chunks(a):
        return jnp.moveaxis(a.reshape((B, H, nc, chunk) + a.shape[3:]), 2, 0)

    (C, n, m), h = lax.scan(mlstm_chunk_step, (C0, n0, m0),
                            (chunks(q), chunks(k), chunks(v), chunks(ig), chunks(lf)))
    h = jnp.moveaxis(h, 0, 2).reshape(B, H, S, -1)
    return h, C, n, m


def mlstm_branch(mq, mk, mv, mo, mi, mf, C0, n0, m0, norm_w):
    B, S = mi.shape[:2]
    f32 = jnp.float32
    chunk = M_CHUNK if S % M_CHUNK == 0 else S
    q = mq.astype(f32).reshape(B, S, M_HEADS, M_DQK).transpose(0, 2, 1, 3)
    k = (mk.astype(f32) / math.sqrt(M_DQK)).reshape(B, S, M_HEADS, M_DQK).transpose(0, 2, 1, 3)
    v = mv.astype(f32).reshape(B, S, M_HEADS, M_DV).transpose(0, 2, 1, 3)
    ig = mi.astype(f32).transpose(0, 2, 1)
    lf = jax.nn.log_sigmoid(mf.astype(f32)).transpose(0, 2, 1)
    h, C, n, m = mlstm_scan(q, k, v, ig, lf, C0.astype(f32), n0.astype(f32), m0.astype(f32), chunk)
    h = h.transpose(0, 2, 1, 3)
    mu = h.mean(-1, keepdims=True)
    var = jnp.square(h - mu).mean(-1, keepdims=True)
    hn = (h - mu) * lax.rsqrt(var + LN_EPS) * norm_w.astype(f32).reshape(M_HEADS, M_DV)
    out = hn.reshape(B, S, M_HEADS * M_DV) * jax.nn.sigmoid(mo.astype(f32))
    return out.astype(mq.dtype), C, n, m


def dilated_prompt(q, k, v, slopes, dil):
    B, S, H, E = q.shape
    Ls = S // dil
    nb = -(-Ls // A_BLOCK)
    Lp = nb * A_BLOCK

    def residues(a):
        return a.reshape(B, Ls, dil, H, E).transpose(0, 2, 1, 3, 4)

    def key_blocks(a):
        ap = jnp.pad(residues(a), ((0, 0), (0, 0), (A_BLOCK, Lp - Ls), (0, 0), (0, 0)))
        ap = ap.reshape(B, dil, nb + 1, A_BLOCK, H, E)
        return jnp.concatenate([ap[:, :, :-1], ap[:, :, 1:]], axis=3)

    qb = jnp.pad(residues(q), ((0, 0), (0, 0), (0, Lp - Ls), (0, 0), (0, 0))).reshape(B, dil, nb, A_BLOCK, H, E)
    kb, vb = key_blocks(k), key_blocks(v)
    s = jnp.einsum('bdnqhe,bdnkhe->bdnhqk', qb, kb, preferred_element_type=jnp.float32) / math.sqrt(E)
    qi = jnp.arange(A_BLOCK)[:, None]
    ki = jnp.arange(2 * A_BLOCK)[None, :]
    step = qi + A_BLOCK - ki
    key_sub = jnp.arange(nb)[:, None, None] * A_BLOCK + ki[None] - A_BLOCK
    valid = (step >= 0) & (step <= A_STEPS) & (key_sub >= 0)
    bias = -slopes[:, None, None] * (dil * step).astype(jnp.float32)[None]
    s = jnp.where(valid[:, None], s + bias, -jnp.inf)
    lse = jax.nn.logsumexp(s, axis=-1)
    p = jnp.exp(s - lse[..., None])
    o = jnp.einsum('bdnhqk,bdnkhe->bdnqhe', p, vb.astype(jnp.float32))
    o = o.reshape(B, dil, Lp, H, E)[:, :, :Ls].transpose(0, 2, 1, 3, 4).reshape(B, S, H, E)
    lse = lse.transpose(0, 1, 2, 4, 3).reshape(B, dil, Lp, H)[:, :, :Ls].transpose(0, 2, 1, 3).reshape(B, S, H)
    return o, lse


def dilated_sample(q, k, v, buf, slopes, dil):
    Lb = buf.shape[1]
    T, E = q.shape[1], q.shape[3]
    k_all = jnp.concatenate([buf[:, :, 0].astype(k.dtype), k], axis=1)
    v_all = jnp.concatenate([buf[:, :, 1].astype(v.dtype), v], axis=1)
    m = jnp.arange(A_STEPS + 1)
    idx = Lb + jnp.arange(T)[:, None] - dil * m[None, :]
    valid = idx >= 0
    idxc = jnp.maximum(idx, 0)
    kg = k_all[:, idxc]
    vg = v_all[:, idxc]
    s = jnp.einsum('bthe,btmhe->bhtm', q, kg, preferred_element_type=jnp.float32) / math.sqrt(E)
    s = s - slopes[:, None, None] * (dil * m).astype(jnp.float32)[None, None, :]
    s = jnp.where(valid[None, None], s, -jnp.inf)
    lse = jax.nn.logsumexp(s, axis=-1)
    p = jnp.exp(s - lse[..., None])
    o = jnp.einsum('bhtm,btmhe->bthe', p, vg.astype(jnp.float32))
    return o, lse.transpose(0, 2, 1)


def peer(x, w_query, sub_keys, expert_u, expert_v):
    B, S, D = x.shape
    T = B * S
    xt = x.reshape(T, D)
    q = (xt @ w_query).reshape(T, P_HEADS, 2, P_DKEY // 2)
    sc = jnp.einsum('thpc,hpkc->thpk', q, sub_keys, preferred_element_type=jnp.float32)
    top_s, top_i = lax.top_k(sc, P_TOPK)
    cand_s = (top_s[:, :, 0, :, None] + top_s[:, :, 1, None, :]).reshape(T, P_HEADS, P_TOPK * P_TOPK)
    cand_i = (top_i[:, :, 0, :, None] * P_NKEYS + top_i[:, :, 1, None, :]).reshape(T, P_HEADS, P_TOPK * P_TOPK)
    best_s, best_j = lax.top_k(cand_s, P_TOPK)
    experts = jnp.take_along_axis(cand_i, best_j, axis=-1).reshape(T, P_HEADS * P_TOPK)
    gates = jax.nn.softmax(best_s, axis=-1).reshape(T, P_HEADS * P_TOPK)
    nb = -(-T // P_TOKEN_BLOCK)
    pad = nb * P_TOKEN_BLOCK - T
    xp = jnp.pad(xt, ((0, pad), (0, 0))).reshape(nb, P_TOKEN_BLOCK, D)
    ep = jnp.pad(experts, ((0, pad), (0, 0))).reshape(nb, P_TOKEN_BLOCK, -1)
    gp = jnp.pad(gates, ((0, pad), (0, 0))).reshape(nb, P_TOKEN_BLOCK, -1)

    def block(args):
        xb, eb, gb = args
        u = expert_u[eb]
        a = jnp.einsum('tkd,td->tk', u, xb, preferred_element_type=jnp.float32)
        act = jax.nn.gelu(a, approximate=False) * gb
        vv = expert_v[eb]
        return jnp.einsum('tk,tkd->td', act.astype(vv.dtype), vv)

    y = lax.map(block, (xp, ep, gp))
    return y.reshape(nb * P_TOKEN_BLOCK, D)[:T].reshape(B, S, D).astype(x.dtype)


def trunk_layer(x, C0, n0, m0, kv_bufs, w_in, b_in, mlstm_norm_w, w_branch_m, w_branch_a, w_out,
                ln1_g, ln1_b, w_query, sub_keys, expert_u, expert_v, ln2_g, ln2_b):
    B, S = x.shape[:2]
    G = len(A_GROUPS)
    offsets = [int(o) for o in np.cumsum(SPLITS)[:-1]]
    z = jnp.einsum('bsd,de->bse', x, w_in) + b_in
    mq, mk, mv, mo, mi, mf, aq, ak, av, gm, ga = jnp.split(z, offsets, axis=-1)
    m_out, C, n, m = mlstm_branch(mq, mk, mv, mo, mi, mf, C0, n0, m0, mlstm_norm_w)
    aq = aq.reshape(B, S, G, A_HEADS, A_DH)
    ak = ak.reshape(B, S, G, A_HEADS, A_DH)
    av = av.reshape(B, S, G, A_HEADS, A_DH)
    slopes = alibi_slopes()
    outs, lses, new_kv = [], [], []
    for g, (win, dil) in enumerate(A_GROUPS):
        qg, kg, vg = aq[:, :, g], ak[:, :, g], av[:, :, g]
        kv_new = jnp.stack([kg, vg], axis=2)
        if kv_bufs is None:
            o, l = dilated_prompt(qg, kg, vg, slopes[g], dil)
            new_kv.append(kv_new[:, -min(win, S):])
        else:
            buf = kv_bufs[g]
            o, l = dilated_sample(qg, kg, vg, buf, slopes[g], dil)
            new_kv.append(jnp.concatenate([buf.astype(kv_new.dtype), kv_new], axis=1)[:, -buf.shape[1]:])
        outs.append(o)
        lses.append(l)
    w_grp = jax.nn.softmax(jnp.stack(lses), axis=0)
    a_out = jnp.einsum('gbsh,gbshe->bshe', w_grp, jnp.stack(outs)).reshape(B, S, A_HEADS * A_DH).astype(x.dtype)
    merged = jax.nn.sigmoid(gm) * (m_out @ w_branch_m) + jax.nn.sigmoid(ga) * (a_out @ w_branch_a)
    x1 = layer_norm(ALPHA * x + merged @ w_out, ln1_g, ln1_b)
    x2 = layer_norm(ALPHA * x1 + peer(x1, w_query, sub_keys, expert_u, expert_v), ln2_g, ln2_b)
    return x2, C.astype(x.dtype), n.astype(x.dtype), m.astype(x.dtype), new_kv


def setup_inputs(seed: int = 0) -> dict:
    key = jax.random.key(seed)
    ks = jax.random.split(key, 24)
    f32 = jnp.float32

    def nrm(k, shape, scale):
        return jax.random.normal(k, shape, f32) * scale

    col_scale = jnp.concatenate([jnp.full((w,), s, f32) for w, s in zip(SPLITS, SPLIT_SCALES)])
    f_off = int(np.cumsum([0] + SPLITS)[5])
    b_in = nrm(ks[9], (D_IN,), 0.02).at[f_off:f_off + M_HEADS].add(jnp.linspace(3.0, 6.0, M_HEADS))
    return {
        'x_prompt': nrm(ks[0], (BATCH, SEQ, D_MODEL), 1.0),
        'x_sample': nrm(ks[1], (DEC_BATCH, DEC_SEQ, D_MODEL), 1.0),
        'state_C': nrm(ks[2], (DEC_BATCH, M_HEADS, M_DQK, M_DV), 0.5),
        'state_n': nrm(ks[3], (DEC_BATCH, M_HEADS, M_DQK), 0.5),
        'state_m': nrm(ks[4], (DEC_BATCH, M_HEADS), 1.0),
        'cache_kv_w128': nrm(ks[5], (DEC_BATCH, min(A_GROUPS[0][0], PAST_LEN), 2, A_HEADS, A_DH), 1.0),
        'cache_kv_w512': nrm(ks[6], (DEC_BATCH, min(A_GROUPS[1][0], PAST_LEN), 2, A_HEADS, A_DH), 1.0),
        'cache_kv_w2048': nrm(ks[7], (DEC_BATCH, min(A_GROUPS[2][0], PAST_LEN), 2, A_HEADS, A_DH), 1.0),
        'w_in': nrm(ks[8], (D_MODEL, D_IN), D_MODEL ** -0.5) * col_scale,
        'b_in': b_in,
        'mlstm_norm_w': 1.0 + nrm(ks[10], (M_HEADS * M_DV,), 0.02),
        'w_branch_m': nrm(ks[11], (M_HEADS * M_DV, D_MODEL), (M_HEADS * M_DV) ** -0.5 * BETA),
        'w_branch_a': nrm(ks[12], (A_HEADS * A_DH, D_MODEL), (A_HEADS * A_DH) ** -0.5 * BETA),
        'w_out': nrm(ks[13], (D_MODEL, D_MODEL), D_MODEL ** -0.5 * BETA),
        'ln1_g': 1.0 + nrm(ks[14], (D_MODEL,), 0.02),
        'ln1_b': nrm(ks[15], (D_MODEL,), 0.02),
        'w_query': nrm(ks[16], (D_MODEL, P_HEADS * P_DKEY), D_MODEL ** -0.5),
        'sub_keys': nrm(ks[17], (P_HEADS, 2, P_NKEYS, P_DKEY // 2), (P_DKEY // 2) ** -0.5),
        'expert_u': nrm(ks[18], (P_NEXP, D_MODEL), D_MODEL ** -0.5),
        'expert_v': nrm(ks[19], (P_NEXP, D_MODEL), BETA),
        'ln2_g': 1.0 + nrm(ks[20], (D_MODEL,), 0.02),
        'ln2_b': nrm(ks[21], (D_MODEL,), 0.02),
    }


def reference(x_prompt, x_sample, state_C, state_n, state_m, cache_kv_w128, cache_kv_w512, cache_kv_w2048,
              w_in, b_in, mlstm_norm_w, w_branch_m, w_branch_a, w_out, ln1_g, ln1_b,
              w_query, sub_keys, expert_u, expert_v, ln2_g, ln2_b):
    B = x_prompt.shape[0]
    C0 = jnp.zeros((B, M_HEADS, M_DQK, M_DV), jnp.float32)
    n0 = jnp.zeros((B, M_HEADS, M_DQK), jnp.float32)
    m0 = jnp.zeros((B, M_HEADS), jnp.float32)
    y_prompt, y_sample = x_prompt, x_sample
    for _ in range(DEPTH):
        y_prompt, C_p, n_p, m_p, (kv128_p, kv512_p, kv2048_p) = trunk_layer(
            y_prompt, C0, n0, m0, None, w_in, b_in, mlstm_norm_w, w_branch_m, w_branch_a, w_out,
            ln1_g, ln1_b, w_query, sub_keys, expert_u, expert_v, ln2_g, ln2_b)
        y_sample, C_s, n_s, m_s, (kv128_s, kv512_s, kv2048_s) = trunk_layer(
            y_sample, state_C, state_n, state_m, (cache_kv_w128, cache_kv_w512, cache_kv_w2048),
            w_in, b_in, mlstm_norm_w, w_branch_m, w_branch_a, w_out,
            ln1_g, ln1_b, w_query, sub_keys, expert_u, expert_v, ln2_g, ln2_b)
    return (y_prompt, y_sample, C_p, n_p, m_p, kv128_p, kv512_p, kv2048_p, C_s, n_s, m_s, kv128_s, kv512_s, kv2048_s)
```

```python
import functools
import math

import jax
import jax.numpy as jnp
from jax import lax
from jax.experimental import pallas as pl
from jax.experimental.pallas import tpu as pltpu

F32 = jnp.float32
BF16 = jnp.bfloat16

M_HEADS = 8
M_CHUNK = 128
A_DILATIONS = (1, 4, 16)
A_GROUPS = len(A_DILATIONS)
A_HEADS = 8
A_DH = 128
A_STEPS = 128
P_HEADS = 8
P_NKEYS = 128
P_TOPK = 16
DEPTH = 1
ALPHA = (2 * DEPTH) ** 0.25
LN_EPS = 1e-5

LANES = 128
SUBLANES = 8
VMEM_LIMIT = 56 * 1024 * 1024
NEG_INF = float("-inf")


def _cparams(*sem):
    return pltpu.CompilerParams(dimension_semantics=sem, vmem_limit_bytes=VMEM_LIMIT)


def _dot(a, b):
    return jnp.dot(a, b, preferred_element_type=F32)


def _dot_nt(a, b):
    return lax.dot_general(a, b, (((1,), (1,)), ((), ())), preferred_element_type=F32)


def _dot_tn(a, b):
    return lax.dot_general(a, b, (((0,), (0,)), ((), ())), preferred_element_type=F32)


def _proj_kernel(x_ref, w_ref, b_ref, o_ref):
    o_ref[...] = _dot(x_ref[...], w_ref[...]) + b_ref[...]


def _proj(xb, wb, b, tm, tn):
    T, K = xb.shape
    N = wb.shape[1]
    return pl.pallas_call(
        _proj_kernel,
        grid=(N // tn, T // tm),
        in_specs=[
            pl.BlockSpec((tm, K), lambda j, i: (i, 0)),
            pl.BlockSpec((K, tn), lambda j, i: (0, j)),
            pl.BlockSpec((1, tn), lambda j, i: (0, j)),
        ],
        out_specs=pl.BlockSpec((tm, tn), lambda j, i: (i, j)),
        out_shape=jax.ShapeDtypeStruct((T, N), F32),
        compiler_params=_cparams("parallel", "parallel"),
        name="in_proj",
    )(xb, wb, b)


def _split_bf16(a):
    hi = a.astype(BF16)
    lo = (a - hi.astype(F32)).astype(BF16)
    return hi, lo


def _gates_kernel(x_ref, w_ref, b_ref, o_ref):
    xh, xl = _split_bf16(x_ref[...])
    wh, wl = _split_bf16(w_ref[...])
    o_ref[...] = _dot(xh, wh) + _dot(xh, wl) + _dot(xl, wh) + b_ref[...]


def _gates(x, wg, bg, tm):
    T, K = x.shape
    return pl.pallas_call(
        _gates_kernel,
        grid=(T // tm,),
        in_specs=[
            pl.BlockSpec((tm, K), lambda i: (i, 0)),
            pl.BlockSpec((K, LANES), lambda i: (0, 0)),
            pl.BlockSpec((1, LANES), lambda i: (0, 0)),
        ],
        out_specs=pl.BlockSpec((tm, LANES), lambda i: (i, 0)),
        out_shape=jax.ShapeDtypeStruct((T, LANES), F32),
        compiler_params=_cparams("parallel"),
        name="gate_proj",
    )(x, wg, bg)


def _merge_kernel(m_ref, a_ref, wm_ref, wa_ref, gm_ref, ga_ref, o_ref):
    bm = _dot(m_ref[...], wm_ref[...])
    ba = _dot(a_ref[...].astype(BF16), wa_ref[...])
    o_ref[...] = (jax.nn.sigmoid(gm_ref[...]) * bm + jax.nn.sigmoid(ga_ref[...]) * ba).astype(o_ref.dtype)


def _merge(m_out, a_out, wm, wa, z, gm_blk, ga_blk, tm, tn):
    T, Km = m_out.shape
    Ka = a_out.shape[1]
    N = wm.shape[1]
    return pl.pallas_call(
        _merge_kernel,
        grid=(N // tn, T // tm),
        in_specs=[
            pl.BlockSpec((tm, Km), lambda j, i: (i, 0)),
            pl.BlockSpec((tm, Ka), lambda j, i: (i, 0)),
            pl.BlockSpec((Km, tn), lambda j, i: (0, j)),
            pl.BlockSpec((Ka, tn), lambda j, i: (0, j)),
            pl.BlockSpec((tm, tn), lambda j, i: (i, gm_blk + j)),
            pl.BlockSpec((tm, tn), lambda j, i: (i, ga_blk + j)),
        ],
        out_specs=pl.BlockSpec((tm, tn), lambda j, i: (i, j)),
        out_shape=jax.ShapeDtypeStruct((T, N), BF16),
        compiler_params=_cparams("parallel", "parallel"),
        name="branch_merge",
    )(m_out, a_out, wm, wa, z, z)


def _layer_norm_rows(v, g, b):
    mu = jnp.mean(v, axis=-1, keepdims=True)
    c = v - mu
    var = jnp.mean(c * c, axis=-1, keepdims=True)
    return c * lax.rsqrt(var + LN_EPS) * g + b


def _outproj_ln_kernel(a_ref, w_ref, x_ref, g_ref, b_ref, o_ref, ob_ref):
    k = pl.program_id(1)

    @pl.when(k == 0)
    def _():
        o_ref[...] = ALPHA * x_ref[...]

    o_ref[...] += _dot(a_ref[...], w_ref[...])

    @pl.when(k == pl.num_programs(1) - 1)
    def _():
        y = _layer_norm_rows(o_ref[...], g_ref[...], b_ref[...])
        o_ref[...] = y
        ob_ref[...] = y.astype(BF16)


def _outproj_ln(a, w, x, g, b, tm, tk):
    T, K = a.shape
    D = w.shape[1]
    return pl.pallas_call(
        _outproj_ln_kernel,
        grid=(T // tm, K // tk),
        in_specs=[
            pl.BlockSpec((tm, tk), lambda i, k: (i, k)),
            pl.BlockSpec((tk, D), lambda i, k: (k, 0)),
            pl.BlockSpec((tm, D), lambda i, k: (i, 0)),
            pl.BlockSpec((1, D), lambda i, k: (0, 0)),
            pl.BlockSpec((1, D), lambda i, k: (0, 0)),
        ],
        out_specs=[
            pl.BlockSpec((tm, D), lambda i, k: (i, 0)),
            pl.BlockSpec((tm, D), lambda i, k: (i, 0)),
        ],
        out_shape=[jax.ShapeDtypeStruct((T, D), F32), jax.ShapeDtypeStruct((T, D), BF16)],
        compiler_params=_cparams("parallel", "arbitrary"),
        name="out_proj_ln",
    )(a, w, x, g, b)


def _residual_ln_kernel(x_ref, y_ref, g_ref, b_ref, o_ref):
    o_ref[...] = _layer_norm_rows(ALPHA * x_ref[...] + y_ref[...], g_ref[...], b_ref[...])


def _residual_ln(x, y, g, b, tm):
    T, D = x.shape
    return pl.pallas_call(
        _residual_ln_kernel,
        grid=(T // tm,),
        in_specs=[
            pl.BlockSpec((tm, D), lambda i: (i, 0)),
            pl.BlockSpec((tm, D), lambda i: (i, 0)),
            pl.BlockSpec((1, D), lambda i: (0, 0)),
            pl.BlockSpec((1, D), lambda i: (0, 0)),
        ],
        out_specs=pl.BlockSpec((tm, D), lambda i: (i, 0)),
        out_shape=jax.ShapeDtypeStruct((T, D), F32),
        compiler_params=_cparams("parallel"),
        name="residual_ln",
    )(x, y, g, b)


def _log_sigmoid(x):
    return jnp.minimum(x, 0.0) - jnp.log1p(jnp.exp(-jnp.abs(x)))


def _lane_column(a, col):
    lane = lax.broadcasted_iota(jnp.int32, a.shape, 1)
    return jnp.sum(jnp.where(lane == col, a, 0.0), axis=1, keepdims=True)


def _mlstm_gate_math(ig_col, fg_col, ig_row, fg_row, m0_col, seg_shift):
    L = M_CHUNK
    t = lax.broadcasted_iota(jnp.int32, (L, L), 0)
    s = lax.broadcasted_iota(jnp.int32, (L, L), 1)
    lower, upper = s <= t, t <= s
    if seg_shift is not None:
        same = (t >> seg_shift) == (s >> seg_shift)
        lower, upper = lower & same, upper & same
    lf_col, lf_row = _log_sigmoid(fg_col), _log_sigmoid(fg_row)
    f_col = jnp.sum(jnp.where(lower, lf_row, 0.0), axis=1, keepdims=True)
    f_row = jnp.sum(jnp.where(upper, lf_col, 0.0), axis=0, keepdims=True)
    d = jnp.where(lower, f_col - f_row + ig_row, NEG_INF)
    inter = f_col + m0_col
    m_t = jnp.maximum(inter, jnp.max(d, axis=1, keepdims=True))
    w_intra = jnp.exp(d - m_t)
    w_inter = jnp.exp(inter - m_t)
    return f_col, f_row, m_t, w_intra, w_inter


def _mlstm_head_out(num, den, m_t, nw_row, mo):
    hid = num * (1.0 / jnp.maximum(jnp.abs(den), jnp.exp(-m_t)))
    mu = jnp.mean(hid, axis=-1, keepdims=True)
    c = hid - mu
    var = jnp.mean(c * c, axis=-1, keepdims=True)
    return c * lax.rsqrt(var + LN_EPS) * nw_row * jax.nn.sigmoid(mo)


def _mlstm_prompt_kernel(q_ref, k_ref, v_ref, mo_ref, g_ref, gt_ref, nw_ref,
                         mout_ref, c_out_ref, n_out_ref, m_out_ref, c_scr, n_scr, m_scr):
    h = pl.program_id(0)
    c = pl.program_id(1)
    L = M_CHUNK

    @pl.when(c == 0)
    def _():
        c_scr[...] = jnp.zeros_like(c_scr)
        n_scr[...] = jnp.zeros_like(n_scr)
        m_scr[...] = jnp.zeros_like(m_scr)

    g = g_ref[...]
    ig_col = _lane_column(g, h)
    fg_col = _lane_column(g, M_HEADS + h)
    ig_row = gt_ref[pl.ds(h, 1), :]
    fg_row = gt_ref[pl.ds(M_HEADS + h, 1), :]
    f_col, _, m_t, w_intra, w_inter = _mlstm_gate_math(ig_col, fg_col, ig_row, fg_row, m_scr[...], None)

    q = q_ref[...]
    kf = k_ref[...] * (1.0 / math.sqrt(q.shape[1]))
    qb, kb, vb = q.astype(BF16), kf.astype(BF16), v_ref[...].astype(BF16)
    c_prev = c_scr[...]
    n_prev = n_scr[...]
    s = _dot_nt(qb, kb) * w_intra
    num = w_inter * _dot(qb, c_prev.astype(BF16)) + _dot(s.astype(BF16), vb)
    den = w_inter * jnp.sum(q * n_prev, axis=1, keepdims=True) + jnp.sum(s, axis=1, keepdims=True)
    mout_ref[...] = _mlstm_head_out(num, den, m_t, nw_ref[...], mo_ref[...]).astype(mout_ref.dtype)

    m_end = m_t[L - 1:L, :]
    w_state = w_inter[L - 1:L, :]
    w_tok = jnp.exp(f_col[L - 1:L, :] - f_col + ig_col - m_end)
    kw = kf * w_tok
    c_scr[...] = w_state * c_prev + _dot_tn(kw.astype(BF16), vb)
    n_scr[...] = w_state * n_prev + jnp.sum(kw, axis=0, keepdims=True)
    m_scr[...] = m_end

    @pl.when(c == pl.num_programs(1) - 1)
    def _():
        c_out_ref[0, 0] = c_scr[...]
        n_out_ref[0] = n_scr[...]
        m_out_ref[0] = jnp.broadcast_to(m_scr[...], (SUBLANES, LANES))


def _mlstm_prompt(z, g, gt, nw, s_len, t_all):
    dv = nw.shape[1] // M_HEADS
    dk = dv // 2
    L = M_CHUNK
    H = M_HEADS
    return pl.pallas_call(
        _mlstm_prompt_kernel,
        grid=(H, s_len // L),
        in_specs=[
            pl.BlockSpec((L, dk), lambda h, c: (c, h)),
            pl.BlockSpec((L, dk), lambda h, c: (c, H + h)),
            pl.BlockSpec((L, dv), lambda h, c: (c, H + h)),
            pl.BlockSpec((L, dv), lambda h, c: (c, 2 * H + h)),
            pl.BlockSpec((L, LANES), lambda h, c: (c, 0)),
            pl.BlockSpec((2 * H, L), lambda h, c: (0, c)),
            pl.BlockSpec((1, dv), lambda h, c: (0, h)),
        ],
        out_specs=[
            pl.BlockSpec((L, dv), lambda h, c: (c, h)),
            pl.BlockSpec((1, 1, dk, dv), lambda h, c: (0, h, 0, 0)),
            pl.BlockSpec((1, 1, dk), lambda h, c: (h, 0, 0)),
            pl.BlockSpec((1, SUBLANES, LANES), lambda h, c: (h, 0, 0)),
        ],
        out_shape=[
            jax.ShapeDtypeStruct((t_all, H * dv), BF16),
            jax.ShapeDtypeStruct((1, H, dk, dv), F32),
            jax.ShapeDtypeStruct((H, 1, dk), F32),
            jax.ShapeDtypeStruct((H, SUBLANES, LANES), F32),
        ],
        scratch_shapes=[pltpu.VMEM((dk, dv), F32), pltpu.VMEM((1, dk), F32), pltpu.VMEM((1, 1), F32)],
        compiler_params=_cparams("parallel", "arbitrary"),
        name="mlstm_prompt",
    )(z, z, z, z, g, gt, nw)


def _mlstm_sample_kernel(q_ref, k_ref, v_ref, mo_ref, g_ref, gt_ref, nw_ref, m0_ref, c0_ref, n0_ref, alias_ref,
                         mout_ref, c_out_ref, n_out_ref, mt_out_ref,
                         num_scr, den_scr, kw_scr, vb_scr, qb_scr, wi_scr, mt_scr, *, seg_shift):
    del alias_ref
    h = pl.program_id(0)
    b = pl.program_id(1)
    L = M_CHUNK
    seg = 1 << seg_shift

    @pl.when(b == 0)
    def _():
        g = g_ref[...]
        ig_col = _lane_column(g, h)
        fg_col = _lane_column(g, M_HEADS + h)
        ig_row = gt_ref[pl.ds(h, 1), :]
        fg_row = gt_ref[pl.ds(M_HEADS + h, 1), :]
        m0_col = _lane_column(m0_ref[...], h)
        f_col, f_row, m_t, w_intra, w_inter = _mlstm_gate_math(ig_col, fg_col, ig_row, fg_row, m0_col, seg_shift)
        q = q_ref[...]
        kf = k_ref[...] * (1.0 / math.sqrt(q.shape[1]))
        qb, kb, vb = q.astype(BF16), kf.astype(BF16), v_ref[...].astype(BF16)
        s = _dot_nt(qb, kb) * w_intra
        num_scr[...] = _dot(s.astype(BF16), vb)
        den_scr[...] = jnp.sum(s, axis=1, keepdims=True)
        t = lax.broadcasted_iota(jnp.int32, (L, L), 0)
        sidx = lax.broadcasted_iota(jnp.int32, (L, L), 1)
        is_last = sidx == (t | (seg - 1))
        m_row = jnp.sum(jnp.where(t == sidx, m_t, 0.0), axis=0, keepdims=True)
        f_last = jnp.sum(jnp.where(is_last, f_row, 0.0), axis=1, keepdims=True)
        m_end = jnp.sum(jnp.where(is_last, m_row, 0.0), axis=1, keepdims=True)
        w_tok = jnp.exp(f_last - f_col + ig_col - m_end)
        kw_scr[...] = kf * w_tok
        vb_scr[...] = vb
        qb_scr[...] = qb
        wi_scr[...] = w_inter
        mt_scr[...] = m_t

    c_b = c0_ref[0, 0]
    n_b = n0_ref[0]
    row = lax.broadcasted_iota(jnp.int32, (L, 1), 0)
    in_b = (row >> seg_shift) == b
    wi = wi_scr[...]
    qc = _dot(qb_scr[...], c_b.astype(BF16))
    num_scr[...] += jnp.where(in_b, wi * qc, 0.0)
    qn = jnp.sum(q_ref[...] * n_b, axis=1, keepdims=True)
    den_scr[...] += jnp.where(in_b, wi * qn, 0.0)
    w_state = jnp.sum(jnp.where(row == seg * b + (seg - 1), wi, 0.0), axis=0, keepdims=True)
    kw_b = jnp.where(in_b, kw_scr[...], 0.0)
    c_out_ref[0, 0] = w_state * c_b + _dot_tn(kw_b.astype(BF16), vb_scr[...])
    n_out_ref[0] = w_state * n_b + jnp.sum(kw_b, axis=0, keepdims=True)

    @pl.when(b == pl.num_programs(1) - 1)
    def _():
        m_t = mt_scr[...]
        out = _mlstm_head_out(num_scr[...], den_scr[...], m_t, nw_ref[...], mo_ref[...])
        mout_ref[...] = out.astype(mout_ref.dtype)
        mt_out_ref[0] = jnp.broadcast_to(m_t, (L, LANES))


def _mlstm_sample(z, g, gt, nw, m0_rows, c0, n0, m_out, row0):
    dv = nw.shape[1] // M_HEADS
    dk = dv // 2
    L = M_CHUNK
    H = M_HEADS
    B = c0.shape[0]
    seg = L // B
    rb = row0 // L
    kern = functools.partial(_mlstm_sample_kernel, seg_shift=seg.bit_length() - 1)
    return pl.pallas_call(
        kern,
        grid=(H, B),
        in_specs=[
            pl.BlockSpec((L, dk), lambda h, b: (rb, h)),
            pl.BlockSpec((L, dk), lambda h, b: (rb, H + h)),
            pl.BlockSpec((L, dv), lambda h, b: (rb, H + h)),
            pl.BlockSpec((L, dv), lambda h, b: (rb, 2 * H + h)),
            pl.BlockSpec((L, LANES), lambda h, b: (rb, 0)),
            pl.BlockSpec((2 * H, L), lambda h, b: (0, rb)),
            pl.BlockSpec((1, dv), lambda h, b: (0, h)),
            pl.BlockSpec((L, LANES), lambda h, b: (0, 0)),
            pl.BlockSpec((1, 1, dk, dv), lambda h, b: (b, h, 0, 0)),
            pl.BlockSpec((1, 1, dk), lambda h, b: (b * H + h, 0, 0)),
            pl.BlockSpec(memory_space=pl.ANY),
        ],
        out_specs=[
            pl.BlockSpec((L, dv), lambda h, b: (rb, h)),
            pl.BlockSpec((1, 1, dk, dv), lambda h, b: (b, h, 0, 0)),
            pl.BlockSpec((1, 1, dk), lambda h, b: (b * H + h, 0, 0)),
            pl.BlockSpec((1, L, LANES), lambda h, b: (h, 0, 0)),
        ],
        out_shape=[
            jax.ShapeDtypeStruct(m_out.shape, m_out.dtype),
            jax.ShapeDtypeStruct(c0.shape, F32),
            jax.ShapeDtypeStruct(n0.shape, F32),
            jax.ShapeDtypeStruct((H, L, LANES), F32),
        ],
        scratch_shapes=[
            pltpu.VMEM((L, dv), F32), pltpu.VMEM((L, 1), F32), pltpu.VMEM((L, dk), F32), pltpu.VMEM((L, dv), BF16),
            pltpu.VMEM((L, dk), BF16), pltpu.VMEM((L, 1), F32), pltpu.VMEM((L, 1), F32),
        ],
        input_output_aliases={10: 0},
        compiler_params=_cparams("parallel", "arbitrary"),
        name="mlstm_sample",
    )(z, z, z, z, g, gt, nw, m0_rows, c0, n0, m_out)


A_SUPER = A_STEPS * max(A_DILATIONS)
ATT_SCALE = 1.0 / math.sqrt(A_DH)


def _rows(start, size, stride):
    return pl.ds(start, size) if stride == 1 else pl.ds(start, size, stride=stride)


def _group_softmax_merge(lses, outs):
    mx = functools.reduce(jnp.maximum, lses)
    ws = [jnp.exp(l - mx) for l in lses]
    tot = functools.reduce(lambda a, b: a + b, ws)
    acc = functools.reduce(lambda a, b: a + b, [w * o for w, o in zip(ws, outs)])
    return acc / tot


def _attn_prompt_kernel(sl_ref, *refs):
    in_refs = refs[:5 * A_GROUPS]
    a_ref, o_scr, l_scr = refs[5 * A_GROUPS:]
    h = pl.program_id(0)
    n = pl.program_id(1)
    B = A_STEPS
    qi = lax.broadcasted_iota(jnp.int32, (B, B), 0)
    ki = lax.broadcasted_iota(jnp.int32, (B, B), 1)
    valid_prev, valid_cur = ki >= qi, ki <= qi
    step_prev = (qi - ki + B).astype(F32)
    step_cur = (qi - ki).astype(F32)

    for g, dil in enumerate(A_DILATIONS):
        q_ref, kc_ref, kp_ref, vc_ref, vp_ref = in_refs[5 * g:5 * g + 5]
        slope = sl_ref[g, h] * float(dil)
        bias_cur = jnp.where(valid_cur, -slope * step_cur, NEG_INF)
        bias_prev = jnp.where(valid_prev, -slope * step_prev, NEG_INF)
        bias_prev_first = jnp.where(n > 0, bias_prev, NEG_INF)
        nblk = A_SUPER // (B * dil)
        for r in range(dil):
            for blk in range(nblk):
                start = r + dil * B * blk
                rows = _rows(start, B, dil)
                qb = q_ref[rows, :].astype(BF16)
                kc = kc_ref[rows, :].astype(BF16)
                vc = vc_ref[rows, :].astype(BF16)
                if blk > 0:
                    prow = _rows(start - dil * B, B, dil)
                    kp, vp, bp = kc_ref[prow, :], vc_ref[prow, :], bias_prev
                else:
                    prow = _rows(r + dil * B * (nblk - 1), B, dil)
                    kp, vp, bp = kp_ref[prow, :], vp_ref[prow, :], bias_prev_first
                s_prev = _dot_nt(qb, kp.astype(BF16)) * ATT_SCALE + bp
                s_cur = _dot_nt(qb, kc) * ATT_SCALE + bias_cur
                mx = jnp.maximum(jnp.max(s_prev, axis=1, keepdims=True), jnp.max(s_cur, axis=1, keepdims=True))
                e_prev = jnp.exp(s_prev - mx)
                e_cur = jnp.exp(s_cur - mx)
                den = jnp.sum(e_prev, axis=1, keepdims=True) + jnp.sum(e_cur, axis=1, keepdims=True)
                o = (_dot(e_prev.astype(BF16), vp.astype(BF16)) + _dot(e_cur.astype(BF16), vc)) * (1.0 / den)
                o_scr[g, rows, :] = o
                l_scr[g, rows, :] = jnp.broadcast_to(mx + jnp.log(den), (B, LANES))

    def merge(i, carry):
        rows = pl.ds(pl.multiple_of(i * B, B), B)
        lses = [l_scr[g, rows, :] for g in range(A_GROUPS)]
        outs = [o_scr[g, rows, :] for g in range(A_GROUPS)]
        a_ref[rows, :] = _group_softmax_merge(lses, outs)
        return carry

    lax.fori_loop(0, A_SUPER // B, merge, 0)


def _attn_prompt(z, slopes, s_len, t_all, aq_blk, ak_blk, av_blk):
    R = A_SUPER
    in_specs = [pl.BlockSpec(memory_space=pltpu.SMEM)]
    args = [slopes]
    for g in range(A_GROUPS):
        off = g * A_HEADS
        cur = lambda h, n, base: (n, base + h)
        prev = lambda h, n, base: (jnp.maximum(n - 1, 0), base + h)
        for base, fn in ((aq_blk + off, cur), (ak_blk + off, cur), (ak_blk + off, prev), (av_blk + off, cur), (av_blk + off, prev)):
            in_specs.append(pl.BlockSpec((R, A_DH), functools.partial(fn, base=base)))
            args.append(z)
    return pl.pallas_call(
        _attn_prompt_kernel,
        grid=(A_HEADS, s_len // R),
        in_specs=in_specs,
        out_specs=pl.BlockSpec((R, A_DH), lambda h, n: (n, h)),
        out_shape=jax.ShapeDtypeStruct((t_all, A_HEADS * A_DH), F32),
        scratch_shapes=[pltpu.VMEM((A_GROUPS, R, A_DH), F32), pltpu.VMEM((A_GROUPS, R, LANES), F32)],
        compiler_params=_cparams("parallel", "arbitrary"),
        name="attn_prompt",
    )(*args)


def _attn_sample_kernel(sl_ref, *refs, seq):
    z_refs = refs[:3 * A_GROUPS]
    buf_refs = refs[3 * A_GROUPS:4 * A_GROUPS]
    a_ref = refs[4 * A_GROUPS + 1]
    b = pl.program_id(0)
    half = b % (SUBLANES // seq)
    PADK = LANES

    for h in range(A_HEADS):
        cols = slice(h * A_DH, (h + 1) * A_DH)
        lses, outs = [], []
        for g, dil in enumerate(A_DILATIONS):
            q_ref, kn_ref, vn_ref = z_refs[3 * g:3 * g + 3]
            buf_ref = buf_refs[g]
            lb = A_STEPS * dil
            slope = sl_ref[g, h]
            kbuf = buf_ref[0, pl.ds(h, lb, stride=2 * A_HEADS), :]
            vbuf = buf_ref[0, pl.ds(A_HEADS + h, lb, stride=2 * A_HEADS), :]
            pad = jnp.zeros((PADK - SUBLANES, A_DH), F32)
            kx = jnp.concatenate([kbuf, kn_ref[:, cols], pad], axis=0).astype(BF16)
            vx = jnp.concatenate([vbuf, vn_ref[:, cols], pad], axis=0).astype(BF16)
            s = _dot_nt(q_ref[:, cols].astype(BF16), kx) * ATT_SCALE
            col = lax.broadcasted_iota(jnp.int32, s.shape, 1)
            t = lax.broadcasted_iota(jnp.int32, s.shape, 0) - seq * half
            t_new = col - lb - seq * half
            in_buf = col < lb
            pos = jnp.where(in_buf, col, lb + t_new)
            delta = lb + t - pos
            valid = (in_buf | ((t_new >= 0) & (t_new < seq))) & (delta >= 0) & (delta <= A_STEPS * dil) \
                & ((delta & (dil - 1)) == 0)
            s = jnp.where(valid, s - slope * delta.astype(F32), NEG_INF)
            mx = jnp.max(s, axis=1, keepdims=True)
            e = jnp.exp(s - mx)
            den = jnp.sum(e, axis=1, keepdims=True)
            outs.append(_dot(e.astype(BF16), vx) * (1.0 / den))
            lses.append(mx + jnp.log(den))
        merged = _group_softmax_merge(lses, outs)
        for hv in range(SUBLANES // seq):
            @pl.when(half == hv)
            def _(merged=merged, hv=hv):
                a_ref[hv * seq:(hv + 1) * seq, cols] = merged[hv * seq:(hv + 1) * seq, :]


def _attn_sample(z, slopes, bufs, a_out, row0, aq_blk, ak_blk, av_blk, seq):
    B = bufs[0].shape[0]
    W = A_HEADS * A_DH
    per_blk = SUBLANES // seq
    rb0 = row0 // SUBLANES
    in_specs = [pl.BlockSpec(memory_space=pltpu.SMEM)]
    args = [slopes]
    for g in range(A_GROUPS):
        for base in (aq_blk + g, ak_blk + g, av_blk + g):
            in_specs.append(pl.BlockSpec((SUBLANES, W), functools.partial(lambda b, base: (rb0 + b // per_blk, base), base=base)))
            args.append(z)
    for g in range(A_GROUPS):
        in_specs.append(pl.BlockSpec((1,) + bufs[g].shape[1:], lambda b: (b, 0, 0)))
        args.append(bufs[g])
    in_specs.append(pl.BlockSpec(memory_space=pl.ANY))
    args.append(a_out)
    return pl.pallas_call(
        functools.partial(_attn_sample_kernel, seq=seq),
        grid=(B,),
        in_specs=in_specs,
        out_specs=pl.BlockSpec((SUBLANES, W), lambda b: (rb0 + b // per_blk, 0)),
        out_shape=jax.ShapeDtypeStruct(a_out.shape, a_out.dtype),
        input_output_aliases={len(args) - 1: 0},
        compiler_params=_cparams("arbitrary"),
        name="attn_sample",
    )(*args)


ST_TAU, ST_MAX0, ST_MAX1, ST_INVZ = 0, 1, 2, 3
ST_ROWS = SUBLANES


def _top_values(sc, k_top):
    R, n = sc.shape
    rows = lax.broadcasted_iota(jnp.int32, (R, n), 0)
    out_rows = lax.broadcasted_iota(jnp.int32, (k_top, n), 0)

    def body(k, carry):
        sc, out = carry
        m = jnp.max(sc, axis=0, keepdims=True)
        first = jnp.min(jnp.where(sc == m, rows, R), axis=0, keepdims=True)
        out = jnp.where(out_rows == k, m, out)
        sc = jnp.where(rows == first, NEG_INF, sc)
        return sc, out

    _, out = lax.fori_loop(0, k_top, body, (sc, jnp.zeros((k_top, n), F32)))
    return out


def _peer_scores_kernel(x_ref, wq_ref, sk_ref, sc_ref, st_ref, q_scr):
    tm = x_ref.shape[0]
    dkh = sk_ref.shape[2]
    q_scr[...] = _dot(x_ref[...], wq_ref[...]).astype(BF16)
    for hp in range(2 * P_HEADS):
        sc_ref[hp] = _dot_nt(sk_ref[hp], q_scr[:, hp * dkh:(hp + 1) * dkh])

    def per_chunk(ci, carry):
        lanes = pl.ds(pl.multiple_of(ci * LANES, LANES), LANES)

        def per_head(h, carry):
            top0 = _top_values(sc_ref[2 * h, :, lanes], P_TOPK)
            top1 = _top_values(sc_ref[2 * h + 1, :, lanes], P_TOPK)
            cand = jnp.concatenate([top0[i:i + 1, :] + top1 for i in range(P_TOPK)], axis=0)
            best = _top_values(cand, P_TOPK)
            z = jnp.sum(jnp.exp(best - best[0:1, :]), axis=0, keepdims=True)
            st_ref[h, :, lanes] = jnp.concatenate(
                [best[P_TOPK - 1:P_TOPK, :], top0[0:1, :], top1[0:1, :], 1.0 / z,
                 jnp.zeros((ST_ROWS - 4, LANES), F32)], axis=0)
            return carry

        return lax.fori_loop(0, P_HEADS, per_head, carry)

    lax.fori_loop(0, tm // LANES, per_chunk, 0)


def _peer_scores(xb, wq, sk, tm):
    T, D = xb.shape
    nq = wq.shape[1]
    return pl.pallas_call(
        _peer_scores_kernel,
        grid=(T // tm,),
        in_specs=[
            pl.BlockSpec((tm, D), lambda i: (i, 0)),
            pl.BlockSpec((D, nq), lambda i: (0, 0)),
            pl.BlockSpec(sk.shape, lambda i: (0, 0, 0)),
        ],
        out_specs=[
            pl.BlockSpec((2 * P_HEADS, P_NKEYS, tm), lambda i: (0, 0, i)),
            pl.BlockSpec((P_HEADS, ST_ROWS, tm), lambda i: (0, 0, i)),
        ],
        out_shape=[
            jax.ShapeDtypeStruct((2 * P_HEADS, P_NKEYS, T), F32),
            jax.ShapeDtypeStruct((P_HEADS, ST_ROWS, T), F32),
        ],
        scratch_shapes=[pltpu.VMEM((tm, nq), BF16)],
        compiler_params=_cparams("parallel"),
        name="peer_scores",
    )(xb, wq, sk)


def _gelu(a):
    return 0.5 * a * (1.0 + lax.erf(a * (1.0 / math.sqrt(2.0))))


def _peer_dense_kernel(x_ref, sc_ref, st_ref, u_ref, v_ref, y_ref, e1_scr, act_scr, at_scr):
    e = pl.program_id(1)
    tm = x_ref.shape[0]
    te = u_ref.shape[0]
    K = P_NKEYS

    @pl.when(e == 0)
    def _():
        y_ref[...] = jnp.zeros_like(y_ref)
        for h in range(P_HEADS):
            e1_scr[h] = jnp.exp(sc_ref[2 * h + 1] - st_ref[h, ST_MAX1:ST_MAX1 + 1, :])

    at_scr[...] = _dot_nt(u_ref[...], x_ref[...])
    i0 = e * (te // K)

    def per_chunk(ci, carry):
        lanes = pl.ds(pl.multiple_of(ci * LANES, LANES), LANES)
        for ig in range(te // K):
            gate = jnp.zeros((K, LANES), F32)
            i = i0 + ig
            i_base = pl.multiple_of((i // SUBLANES) * SUBLANES, SUBLANES)
            sub = lax.broadcasted_iota(jnp.int32, (SUBLANES, LANES), 0)
            for h in range(P_HEADS):
                s0_rows = sc_ref[2 * h, pl.ds(i_base, SUBLANES), lanes]
                s0 = jnp.sum(jnp.where(sub == i - i_base, s0_rows, 0.0), axis=0, keepdims=True)
                st = st_ref[h, :, lanes]
                w0 = jnp.exp(s0 - st[ST_MAX0:ST_MAX0 + 1, :]) * st[ST_INVZ:ST_INVZ + 1, :]
                total = sc_ref[2 * h + 1, :, lanes] + s0
                gate = gate + jnp.where(total >= st[ST_TAU:ST_TAU + 1, :], e1_scr[h, :, lanes] * w0, 0.0)
            rows = slice(ig * K, (ig + 1) * K)
            act_scr[rows, lanes] = (_gelu(at_scr[rows, lanes]) * gate).astype(BF16)
        return carry

    lax.fori_loop(0, tm // LANES, per_chunk, 0)
    y_ref[...] += _dot_tn(act_scr[...], v_ref[...])


def _peer_dense(xb, sc, st, ub, vb, tm, te):
    T, D = xb.shape
    E = ub.shape[0]
    once = pl.Buffered(1)
    return pl.pallas_call(
        _peer_dense_kernel,
        grid=(T // tm, E // te),
        in_specs=[
            pl.BlockSpec((tm, D), lambda i, e: (i, 0), pipeline_mode=once),
            pl.BlockSpec((2 * P_HEADS, P_NKEYS, tm), lambda i, e: (0, 0, i), pipeline_mode=once),
            pl.BlockSpec((P_HEADS, ST_ROWS, tm), lambda i, e: (0, 0, i)),
            pl.BlockSpec((te, D), lambda i, e: (e, 0)),
            pl.BlockSpec((te, D), lambda i, e: (e, 0)),
        ],
        out_specs=pl.BlockSpec((tm, D), lambda i, e: (i, 0)),
        out_shape=jax.ShapeDtypeStruct((T, D), F32),
        scratch_shapes=[
            pltpu.VMEM((P_HEADS, P_NKEYS, tm), F32),
            pltpu.VMEM((te, tm), BF16),
            pltpu.VMEM((te, tm), F32),
        ],
        compiler_params=_cparams("parallel", "arbitrary"),
        name="peer_dense",
    )(xb, sc, st, ub, vb)


def _tile(n, *candidates):
    for c in candidates:
        if n % c == 0:
            return c
    raise ValueError(f"no tile of {candidates} divides {n}")


def kernel(x_prompt, x_sample, state_C, state_n, state_m, cache_kv_w128, cache_kv_w512, cache_kv_w2048,
           w_in, b_in, mlstm_norm_w, w_branch_m, w_branch_a, w_out, ln1_g, ln1_b,
           w_query, sub_keys, expert_u, expert_v, ln2_g, ln2_b):
    bp, S, D = x_prompt.shape
    DB, DS, _ = x_sample.shape
    H = M_HEADS
    dv, dk = D // 8, D // 16
    AW = A_HEADS * A_DH
    caches = (cache_kv_w128, cache_kv_w512, cache_kv_w2048)
    assert bp == 1 and DB * DS == M_CHUNK and S % A_SUPER == 0
    assert all(c.shape[1] == A_STEPS * d for c, d in zip(caches, A_DILATIONS))
    T = S + DB * DS

    x = jnp.concatenate([x_prompt.reshape(S, D), x_sample.reshape(DB * DS, D)], axis=0)
    xb = x.astype(BF16)

    g_lo = 2 * H * dk + 2 * H * dv
    g_hi = g_lo + 2 * H
    w_main = jnp.concatenate([w_in[:, :g_lo], w_in[:, g_hi:]], axis=1).astype(BF16)
    b_main = jnp.concatenate([b_in[:g_lo], b_in[g_hi:]])[None, :]
    w_gate = jnp.pad(w_in[:, g_lo:g_hi], ((0, 0), (0, LANES - 2 * H)))
    b_gate = jnp.pad(b_in[g_lo:g_hi], (0, LANES - 2 * H))[None, :]
    tm = _tile(T, 640, 128)
    z = _proj(xb, w_main, b_main, tm, 1024)
    gates = _gates(x, w_gate, b_gate, tm)
    gates_t = gates[:, :2 * H].T
    aq_col, ak_col, av_col = g_lo, g_lo + A_GROUPS * AW, g_lo + 2 * A_GROUPS * AW
    gm_col = g_lo + 3 * A_GROUPS * AW
    ga_col = gm_col + D

    nw = mlstm_norm_w[None, :]
    m_out, c_p, n_p, m_p = _mlstm_prompt(z, gates, gates_t, nw, S, T)
    m0_rows = jnp.pad(jnp.repeat(state_m, DS, axis=0), ((0, 0), (0, LANES - H)))
    m_out, c_s, n_s, mt_s = _mlstm_sample(z, gates, gates_t, nw, m0_rows, state_C,
                                          state_n.reshape(DB * H, 1, dk), m_out, S)

    n_slopes = A_GROUPS * A_HEADS
    slopes = (2.0 ** (-8.0 * (jnp.arange(n_slopes, dtype=F32) + 1.0) / n_slopes)).reshape(A_GROUPS, A_HEADS)
    a_out = _attn_prompt(z, slopes, S, T, aq_col // A_DH, ak_col // A_DH, av_col // A_DH)
    bufs = [c.reshape(DB, c.shape[1] * 2 * A_HEADS, A_DH) for c in caches]
    a_out = _attn_sample(z, slopes, bufs, a_out, S, aq_col // AW, ak_col // AW, av_col // AW, DS)

    merged = _merge(m_out, a_out, w_branch_m.astype(BF16), w_branch_a.astype(BF16), z,
                    gm_col // 1024, ga_col // 1024, tm, 1024)
    x1, x1b = _outproj_ln(merged, w_out.astype(BF16), x, ln1_g[None, :], ln1_b[None, :], _tile(T, 320, 128), 512)
    sk = sub_keys.reshape(2 * P_HEADS, P_NKEYS, sub_keys.shape[-1]).astype(BF16)
    sc, st = _peer_scores(x1b, w_query.astype(BF16), sk, tm)
    y = _peer_dense(x1b, sc, st, expert_u.astype(BF16), expert_v.astype(BF16), tm, 512)
    x2 = _residual_ln(x1, y, ln2_g[None, :], ln2_b[None, :], _tile(T, 320, 128))

    y_prompt = x2[:S].reshape(1, S, D)
    y_sample = x2[S:].reshape(DB, DS, D)
    kv_p, kv_s = [], []
    for g, cache in enumerate(caches):
        win = cache.shape[1]
        kg = z[:, ak_col + g * AW:ak_col + (g + 1) * AW]
        vg = z[:, av_col + g * AW:av_col + (g + 1) * AW]
        kv_p.append(jnp.stack([kg[S - win:S], vg[S - win:S]], axis=1).reshape(1, win, 2, A_HEADS, A_DH))
        kv_new = jnp.stack([kg[S:], vg[S:]], axis=1).reshape(DB, DS, 2, A_HEADS, A_DH)
        kv_s.append(jnp.concatenate([cache[:, DS:], kv_new], axis=1))
    m_s = mt_s[:, DS - 1::DS, 0].T
    return (y_prompt, y_sample, c_p, n_p.reshape(1, H, dk), m_p[:, 0, 0][None, :],
            kv_p[0], kv_p[1], kv_p[2],
            c_s, n_s.reshape(DB, H, dk), m_s, kv_s[0], kv_s[1], kv_s[2])
```

```python
import functools
import math

import jax
import jax.numpy as jnp
from jax import lax
from jax.experimental import pallas as pl
from jax.experimental.pallas import tpu as pltpu

F32 = jnp.float32
BF16 = jnp.bfloat16

M_HEADS = 8
M_CHUNK = 128
A_DILATIONS = (1, 4, 16)
A_GROUPS = len(A_DILATIONS)
A_HEADS = 8
A_DH = 128
A_STEPS = 128
P_HEADS = 8
P_NKEYS = 128
P_TOPK = 16
DEPTH = 1
ALPHA = (2 * DEPTH) ** 0.25
LN_EPS = 1e-5

LANES = 128
SUBLANES = 8
VMEM_LIMIT = 56 * 1024 * 1024
NEG_INF = float("-inf")


def _cparams(*sem):
    return pltpu.CompilerParams(dimension_semantics=sem, vmem_limit_bytes=VMEM_LIMIT)


def _dot(a, b):
    return jnp.dot(a, b, preferred_element_type=F32)


def _dot_nt(a, b):
    return lax.dot_general(a, b, (((1,), (1,)), ((), ())), preferred_element_type=F32)


def _dot_tn(a, b):
    return lax.dot_general(a, b, (((0,), (0,)), ((), ())), preferred_element_type=F32)


def _proj_kernel(x_ref, wt_ref, b_ref, o_ref, w_scr):
    @pl.when(pl.program_id(1) == 0)
    def _():
        w_scr[...] = wt_ref[...].astype(BF16)

    o_ref[...] = _dot_nt(x_ref[...], w_scr[...]) + b_ref[...]


def _proj(xb, wt, b, tm, tn, skip_lo, skip_hi):
    T, K = xb.shape
    skip = skip_hi - skip_lo
    N = wt.shape[0] - skip
    assert skip_lo % tn == 0 and skip % SUBLANES == 0 and N % tn == 0 and b.shape == (1, N)
    first_shifted = skip_lo // tn

    def wt_rows(j, i):
        return (pl.multiple_of(j * tn + jnp.where(j >= first_shifted, skip, 0), SUBLANES), 0)

    return pl.pallas_call(
        _proj_kernel,
        grid=(N // tn, T // tm),
        in_specs=[
            pl.BlockSpec((tm, K), lambda j, i: (i, 0)),
            pl.BlockSpec((pl.Element(tn), pl.Element(K)), wt_rows),
            pl.BlockSpec((1, tn), lambda j, i: (0, j)),
        ],
        out_specs=pl.BlockSpec((tm, tn), lambda j, i: (i, j)),
        out_shape=jax.ShapeDtypeStruct((T, N), F32),
        scratch_shapes=[pltpu.VMEM((tn, K), BF16)],
        compiler_params=_cparams("arbitrary", "arbitrary"),
        name="in_proj",
    )(xb, wt, b)


def _split_bf16(a):
    hi = a.astype(BF16)
    lo = (a - hi.astype(F32)).astype(BF16)
    return hi, lo


def _gates_kernel(x_ref, w_ref, b_ref, o_ref):
    xh, xl = _split_bf16(x_ref[...])
    wh, wl = _split_bf16(w_ref[...])
    o_ref[...] = _dot_nt(xh, wh) + _dot_nt(xh, wl) + _dot_nt(xl, wh) + b_ref[...]


def _gates(x, wgt, bg, tm):
    T, K = x.shape
    return pl.pallas_call(
        _gates_kernel,
        grid=(T // tm,),
        in_specs=[
            pl.BlockSpec((tm, K), lambda i: (i, 0)),
            pl.BlockSpec((LANES, K), lambda i: (0, 0)),
            pl.BlockSpec((1, LANES), lambda i: (0, 0)),
        ],
        out_specs=pl.BlockSpec((tm, LANES), lambda i: (i, 0)),
        out_shape=jax.ShapeDtypeStruct((T, LANES), F32),
        compiler_params=_cparams("parallel"),
        name="gate_proj",
    )(x, wgt, bg)


def _merge_kernel(m_ref, a_ref, wm_ref, wa_ref, gm_ref, ga_ref, o_ref):
    bm = _dot(m_ref[...], wm_ref[...])
    ba = _dot(a_ref[...].astype(BF16), wa_ref[...])
    o_ref[...] = (jax.nn.sigmoid(gm_ref[...]) * bm + jax.nn.sigmoid(ga_ref[...]) * ba).astype(o_ref.dtype)


def _merge(m_out, a_out, wm, wa, z, gm_blk, ga_blk, tm, tn):
    T, Km = m_out.shape
    Ka = a_out.shape[1]
    N = wm.shape[1]
    return pl.pallas_call(
        _merge_kernel,
        grid=(N // tn, T // tm),
        in_specs=[
            pl.BlockSpec((tm, Km), lambda j, i: (i, 0)),
            pl.BlockSpec((tm, Ka), lambda j, i: (i, 0)),
            pl.BlockSpec((Km, tn), lambda j, i: (0, j)),
            pl.BlockSpec((Ka, tn), lambda j, i: (0, j)),
            pl.BlockSpec((tm, tn), lambda j, i: (i, gm_blk + j)),
            pl.BlockSpec((tm, tn), lambda j, i: (i, ga_blk + j)),
        ],
        out_specs=pl.BlockSpec((tm, tn), lambda j, i: (i, j)),
        out_shape=jax.ShapeDtypeStruct((T, N), BF16),
        compiler_params=_cparams("parallel", "parallel"),
        name="branch_merge",
    )(m_out, a_out, wm, wa, z, z)


def _layer_norm_rows(v, g, b):
    mu = jnp.mean(v, axis=-1, keepdims=True)
    c = v - mu
    var = jnp.mean(c * c, axis=-1, keepdims=True)
    return c * lax.rsqrt(var + LN_EPS) * g + b


def _outproj_ln_kernel(a_ref, w_ref, x_ref, g_ref, b_ref, o_ref, ob_ref):
    k = pl.program_id(1)

    @pl.when(k == 0)
    def _():
        o_ref[...] = ALPHA * x_ref[...]

    o_ref[...] += _dot(a_ref[...], w_ref[...])

    @pl.when(k == pl.num_programs(1) - 1)
    def _():
        y = _layer_norm_rows(o_ref[...], g_ref[...], b_ref[...])
        o_ref[...] = y
        ob_ref[...] = y.astype(BF16)


def _outproj_ln(a, w, x, g, b, tm, tk):
    T, K = a.shape
    D = w.shape[1]
    return pl.pallas_call(
        _outproj_ln_kernel,
        grid=(T // tm, K // tk),
        in_specs=[
            pl.BlockSpec((tm, tk), lambda i, k: (i, k)),
            pl.BlockSpec((tk, D), lambda i, k: (k, 0)),
            pl.BlockSpec((tm, D), lambda i, k: (i, 0)),
            pl.BlockSpec((1, D), lambda i, k: (0, 0)),
            pl.BlockSpec((1, D), lambda i, k: (0, 0)),
        ],
        out_specs=[
            pl.BlockSpec((tm, D), lambda i, k: (i, 0)),
            pl.BlockSpec((tm, D), lambda i, k: (i, 0)),
        ],
        out_shape=[jax.ShapeDtypeStruct((T, D), F32), jax.ShapeDtypeStruct((T, D), BF16)],
        compiler_params=_cparams("parallel", "arbitrary"),
        name="out_proj_ln",
    )(a, w, x, g, b)


def _residual_ln_kernel(x_ref, y_ref, g_ref, b_ref, o_ref):
    o_ref[...] = _layer_norm_rows(ALPHA * x_ref[...] + y_ref[...], g_ref[...], b_ref[...])


def _residual_ln(x, y, g, b, tm):
    T, D = x.shape
    return pl.pallas_call(
        _residual_ln_kernel,
        grid=(T // tm,),
        in_specs=[
            pl.BlockSpec((tm, D), lambda i: (i, 0)),
            pl.BlockSpec((tm, D), lambda i: (i, 0)),
            pl.BlockSpec((1, D), lambda i: (0, 0)),
            pl.BlockSpec((1, D), lambda i: (0, 0)),
        ],
        out_specs=pl.BlockSpec((tm, D), lambda i: (i, 0)),
        out_shape=jax.ShapeDtypeStruct((T, D), F32),
        compiler_params=_cparams("parallel"),
        name="residual_ln",
    )(x, y, g, b)


def _log_sigmoid(x):
    return jnp.minimum(x, 0.0) - jnp.log1p(jnp.exp(-jnp.abs(x)))


def _lane_column(a, col):
    lane = lax.broadcasted_iota(jnp.int32, a.shape, 1)
    return jnp.sum(jnp.where(lane == col, a, 0.0), axis=1, keepdims=True)


def _mlstm_gate_math(ig_col, fg_col, ig_row, fg_row, m0_col, seg_shift):
    L = M_CHUNK
    t = lax.broadcasted_iota(jnp.int32, (L, L), 0)
    s = lax.broadcasted_iota(jnp.int32, (L, L), 1)
    lower, upper = s <= t, t <= s
    if seg_shift is not None:
        same = (t >> seg_shift) == (s >> seg_shift)
        lower, upper = lower & same, upper & same
    lf_col, lf_row = _log_sigmoid(fg_col), _log_sigmoid(fg_row)
    f_col = jnp.sum(jnp.where(lower, lf_row, 0.0), axis=1, keepdims=True)
    f_row = jnp.sum(jnp.where(upper, lf_col, 0.0), axis=0, keepdims=True)
    d = jnp.where(lower, f_col - f_row + ig_row, NEG_INF)
    inter = f_col + m0_col
    m_t = jnp.maximum(inter, jnp.max(d, axis=1, keepdims=True))
    w_intra = jnp.exp(d - m_t)
    w_inter = jnp.exp(inter - m_t)
    return f_col, f_row, m_t, w_intra, w_inter


def _mlstm_head_out(num, den, m_t, nw_row, mo):
    hid = num * (1.0 / jnp.maximum(jnp.abs(den), jnp.exp(-m_t)))
    mu = jnp.mean(hid, axis=-1, keepdims=True)
    c = hid - mu
    var = jnp.mean(c * c, axis=-1, keepdims=True)
    return c * lax.rsqrt(var + LN_EPS) * nw_row * jax.nn.sigmoid(mo)


def _mlstm_prompt_kernel(q_ref, k_ref, v_ref, mo_ref, g_ref, gt_ref, nw_ref, init_ref,
                         mout_ref, c_out_ref, n_out_ref, m_out_ref, c_scr, n_scr, m_scr):
    del init_ref
    h = pl.program_id(0)
    c = pl.program_id(1)
    L = M_CHUNK

    @pl.when(c == 0)
    def _():
        c_scr[...] = jnp.zeros_like(c_scr)
        n_scr[...] = jnp.zeros_like(n_scr)
        m_scr[...] = jnp.zeros_like(m_scr)

    g = g_ref[...]
    ig_col = _lane_column(g, h)
    fg_col = _lane_column(g, M_HEADS + h)
    ig_row = gt_ref[pl.ds(h, 1), :]
    fg_row = gt_ref[pl.ds(M_HEADS + h, 1), :]
    f_col, _, m_t, w_intra, w_inter = _mlstm_gate_math(ig_col, fg_col, ig_row, fg_row, m_scr[...], None)

    q = q_ref[...]
    kf = k_ref[...] * (1.0 / math.sqrt(q.shape[1]))
    qb, kb, vb = q.astype(BF16), kf.astype(BF16), v_ref[...].astype(BF16)
    c_prev = c_scr[...]
    n_prev = n_scr[...]
    s = _dot_nt(qb, kb) * w_intra
    num = w_inter * _dot(qb, c_prev.astype(BF16)) + _dot(s.astype(BF16), vb)
    den = w_inter * jnp.sum(q * n_prev, axis=1, keepdims=True) + jnp.sum(s, axis=1, keepdims=True)
    mout_ref[...] = _mlstm_head_out(num, den, m_t, nw_ref[...], mo_ref[...]).astype(mout_ref.dtype)

    m_end = m_t[L - 1:L, :]
    w_state = w_inter[L - 1:L, :]
    w_tok = jnp.exp(f_col[L - 1:L, :] - f_col + ig_col - m_end)
    kw = kf * w_tok
    c_scr[...] = w_state * c_prev + _dot_tn(kw.astype(BF16), vb)
    n_scr[...] = w_state * n_prev + jnp.sum(kw, axis=0, keepdims=True)
    m_scr[...] = m_end

    @pl.when(c == pl.num_programs(1) - 1)
    def _():
        c_out_ref[0, 0] = c_scr[...]
        n_out_ref[0] = n_scr[...]
        m_out_ref[0] = jnp.broadcast_to(m_scr[...], (SUBLANES, LANES))


def _mlstm_prompt(z, g, gt, nw, s_len, t_all):
    dv = nw.shape[1] // M_HEADS
    dk = dv // 2
    L = M_CHUNK
    H = M_HEADS
    return pl.pallas_call(
        _mlstm_prompt_kernel,
        grid=(H, s_len // L),
        in_specs=[
            pl.BlockSpec((L, dk), lambda h, c: (c, h)),
            pl.BlockSpec((L, dk), lambda h, c: (c, H + h)),
            pl.BlockSpec((L, dv), lambda h, c: (c, H + h)),
            pl.BlockSpec((L, dv), lambda h, c: (c, 2 * H + h)),
            pl.BlockSpec((L, LANES), lambda h, c: (c, 0)),
            pl.BlockSpec((2 * H, L), lambda h, c: (0, c)),
            pl.BlockSpec((1, dv), lambda h, c: (0, h)),
            pl.BlockSpec(memory_space=pl.ANY),
        ],
        out_specs=[
            pl.BlockSpec((L, dv), lambda h, c: (c, h)),
            pl.BlockSpec((1, 1, dk, dv), lambda h, c: (0, h, 0, 0)),
            pl.BlockSpec((1, 1, dk), lambda h, c: (h, 0, 0)),
            pl.BlockSpec((1, SUBLANES, LANES), lambda h, c: (h, 0, 0)),
        ],
        out_shape=[
            jax.ShapeDtypeStruct((t_all, H * dv), BF16),
            jax.ShapeDtypeStruct((1, H, dk, dv), F32),
            jax.ShapeDtypeStruct((H, 1, dk), F32),
            jax.ShapeDtypeStruct((H, SUBLANES, LANES), F32),
        ],
        scratch_shapes=[pltpu.VMEM((dk, dv), F32), pltpu.VMEM((1, dk), F32), pltpu.VMEM((1, 1), F32)],
        input_output_aliases={7: 0},
        compiler_params=_cparams("parallel", "arbitrary"),
        name="mlstm_prompt",
    )(z, z, z, z, g, gt, nw, jnp.zeros((t_all, H * dv), BF16))


def _mlstm_sample_kernel(q_ref, k_ref, v_ref, mo_ref, g_ref, gt_ref, nw_ref, m0_ref, c0_ref, n0_ref, alias_ref,
                         mout_ref, c_out_ref, n_out_ref, mt_out_ref,
                         num_scr, den_scr, kw_scr, vb_scr, qb_scr, wi_scr, mt_scr, *, seg_shift):
    del alias_ref
    h = pl.program_id(0)
    b = pl.program_id(1)
    L = M_CHUNK
    seg = 1 << seg_shift

    @pl.when(b == 0)
    def _():
        g = g_ref[...]
        ig_col = _lane_column(g, h)
        fg_col = _lane_column(g, M_HEADS + h)
        ig_row = gt_ref[pl.ds(h, 1), :]
        fg_row = gt_ref[pl.ds(M_HEADS + h, 1), :]
        m0_col = _lane_column(m0_ref[...], h)
        f_col, f_row, m_t, w_intra, w_inter = _mlstm_gate_math(ig_col, fg_col, ig_row, fg_row, m0_col, seg_shift)
        q = q_ref[...]
        kf = k_ref[...] * (1.0 / math.sqrt(q.shape[1]))
        qb, kb, vb = q.astype(BF16), kf.astype(BF16), v_ref[...].astype(BF16)
        s = _dot_nt(qb, kb) * w_intra
        num_scr[...] = _dot(s.astype(BF16), vb)
        den_scr[...] = jnp.sum(s, axis=1, keepdims=True)
        t = lax.broadcasted_iota(jnp.int32, (L, L), 0)
        sidx = lax.broadcasted_iota(jnp.int32, (L, L), 1)
        is_last = sidx == (t | (seg - 1))
        m_row = jnp.sum(jnp.where(t == sidx, m_t, 0.0), axis=0, keepdims=True)
        f_last = jnp.sum(jnp.where(is_last, f_row, 0.0), axis=1, keepdims=True)
        m_end = jnp.sum(jnp.where(is_last, m_row, 0.0), axis=1, keepdims=True)
        w_tok = jnp.exp(f_last - f_col + ig_col - m_end)
        kw_scr[...] = kf * w_tok
        vb_scr[...] = vb
        qb_scr[...] = qb
        wi_scr[...] = w_inter
        mt_scr[...] = m_t

    c_b = c0_ref[0, 0]
    n_b = n0_ref[0]
    row = lax.broadcasted_iota(jnp.int32, (L, 1), 0)
    in_b = (row >> seg_shift) == b
    wi = wi_scr[...]
    qc = _dot(qb_scr[...], c_b.astype(BF16))
    num_scr[...] += jnp.where(in_b, wi * qc, 0.0)
    qn = jnp.sum(q_ref[...] * n_b, axis=1, keepdims=True)
    den_scr[...] += jnp.where(in_b, wi * qn, 0.0)
    w_state = jnp.sum(jnp.where(row == seg * b + (seg - 1), wi, 0.0), axis=0, keepdims=True)
    kw_b = jnp.where(in_b, kw_scr[...], 0.0)
    c_out_ref[0, 0] = w_state * c_b + _dot_tn(kw_b.astype(BF16), vb_scr[...])
    n_out_ref[0] = w_state * n_b + jnp.sum(kw_b, axis=0, keepdims=True)

    @pl.when(b == pl.num_programs(1) - 1)
    def _():
        m_t = mt_scr[...]
        out = _mlstm_head_out(num_scr[...], den_scr[...], m_t, nw_ref[...], mo_ref[...])
        mout_ref[...] = out.astype(mout_ref.dtype)
        mt_out_ref[0] = jnp.broadcast_to(m_t, (L, LANES))


def _mlstm_sample(z, g, gt, nw, m0_rows, c0, n0, m_out, row0):
    dv = nw.shape[1] // M_HEADS
    dk = dv // 2
    L = M_CHUNK
    H = M_HEADS
    B = c0.shape[0]
    seg = L // B
    rb = row0 // L
    kern = functools.partial(_mlstm_sample_kernel, seg_shift=seg.bit_length() - 1)
    return pl.pallas_call(
        kern,
        grid=(H, B),
        in_specs=[
            pl.BlockSpec((L, dk), lambda h, b: (rb, h)),
            pl.BlockSpec((L, dk), lambda h, b: (rb, H + h)),
            pl.BlockSpec((L, dv), lambda h, b: (rb, H + h)),
            pl.BlockSpec((L, dv), lambda h, b: (rb, 2 * H + h)),
            pl.BlockSpec((L, LANES), lambda h, b: (rb, 0)),
            pl.BlockSpec((2 * H, L), lambda h, b: (0, rb)),
            pl.BlockSpec((1, dv), lambda h, b: (0, h)),
            pl.BlockSpec((L, LANES), lambda h, b: (0, 0)),
            pl.BlockSpec((1, 1, dk, dv), lambda h, b: (b, h, 0, 0)),
            pl.BlockSpec((1, 1, dk), lambda h, b: (b * H + h, 0, 0)),
            pl.BlockSpec(memory_space=pl.ANY),
        ],
        out_specs=[
            pl.BlockSpec((L, dv), lambda h, b: (rb, h)),
            pl.BlockSpec((1, 1, dk, dv), lambda h, b: (b, h, 0, 0)),
            pl.BlockSpec((1, 1, dk), lambda h, b: (b * H + h, 0, 0)),
            pl.BlockSpec((1, L, LANES), lambda h, b: (h, 0, 0)),
        ],
        out_shape=[
            jax.ShapeDtypeStruct(m_out.shape, m_out.dtype),
            jax.ShapeDtypeStruct(c0.shape, F32),
            jax.ShapeDtypeStruct(n0.shape, F32),
            jax.ShapeDtypeStruct((H, L, LANES), F32),
        ],
        scratch_shapes=[
            pltpu.VMEM((L, dv), F32), pltpu.VMEM((L, 1), F32), pltpu.VMEM((L, dk), F32), pltpu.VMEM((L, dv), BF16),
            pltpu.VMEM((L, dk), BF16), pltpu.VMEM((L, 1), F32), pltpu.VMEM((L, 1), F32),
        ],
        input_output_aliases={10: 0},
        compiler_params=_cparams("parallel", "arbitrary"),
        name="mlstm_sample",
    )(z, z, z, z, g, gt, nw, m0_rows, c0, n0, m_out)


A_SUPER = A_STEPS * max(A_DILATIONS)
ATT_SCALE = 1.0 / math.sqrt(A_DH)


def _rows(start, size, stride):
    return pl.ds(start, size) if stride == 1 else pl.ds(start, size, stride=stride)


def _group_softmax_merge(lses, outs):
    mx = functools.reduce(jnp.maximum, lses)
    ws = [jnp.exp(l - mx) for l in lses]
    tot = functools.reduce(lambda a, b: a + b, ws)
    acc = functools.reduce(lambda a, b: a + b, [w * o for w, o in zip(ws, outs)])
    return acc / tot


def _attn_prompt_kernel(sl_ref, *refs):
    in_refs = refs[:5 * A_GROUPS]
    a_ref, o_scr, l_scr = refs[5 * A_GROUPS + 1:]
    h = pl.program_id(0)
    n = pl.program_id(1)
    B = A_STEPS
    qi = lax.broadcasted_iota(jnp.int32, (B, B), 0)
    ki = lax.broadcasted_iota(jnp.int32, (B, B), 1)
    valid_prev, valid_cur = ki >= qi, ki <= qi
    step_prev = (qi - ki + B).astype(F32)
    step_cur = (qi - ki).astype(F32)

    for g, dil in enumerate(A_DILATIONS):
        q_ref, kc_ref, kp_ref, vc_ref, vp_ref = in_refs[5 * g:5 * g + 5]
        slope = sl_ref[g, h] * float(dil)
        bias_cur = jnp.where(valid_cur, -slope * step_cur, NEG_INF)
        bias_prev = jnp.where(valid_prev, -slope * step_prev, NEG_INF)
        bias_prev_first = jnp.where(n > 0, bias_prev, NEG_INF)
        nblk = A_SUPER // (B * dil)
        for r in range(dil):
            for blk in range(nblk):
                start = r + dil * B * blk
                rows = _rows(start, B, dil)
                qb = q_ref[rows, :].astype(BF16)
                kc = kc_ref[rows, :].astype(BF16)
                vc = vc_ref[rows, :].astype(BF16)
                if blk > 0:
                    prow = _rows(start - dil * B, B, dil)
                    kp, vp, bp = kc_ref[prow, :], vc_ref[prow, :], bias_prev
                else:
                    prow = _rows(r + dil * B * (nblk - 1), B, dil)
                    kp, vp, bp = kp_ref[prow, :], vp_ref[prow, :], bias_prev_first
                s_prev = _dot_nt(qb, kp.astype(BF16)) * ATT_SCALE + bp
                s_cur = _dot_nt(qb, kc) * ATT_SCALE + bias_cur
                mx = jnp.maximum(jnp.max(s_prev, axis=1, keepdims=True), jnp.max(s_cur, axis=1, keepdims=True))
                e_prev = jnp.exp(s_prev - mx)
                e_cur = jnp.exp(s_cur - mx)
                den = jnp.sum(e_prev, axis=1, keepdims=True) + jnp.sum(e_cur, axis=1, keepdims=True)
                o = (_dot(e_prev.astype(BF16), vp.astype(BF16)) + _dot(e_cur.astype(BF16), vc)) * (1.0 / den)
                o_scr[g, rows, :] = o
                l_scr[g, rows, :] = jnp.broadcast_to(mx + jnp.log(den), (B, LANES))

    def merge(i, carry):
        rows = pl.ds(pl.multiple_of(i * B, B), B)
        lses = [l_scr[g, rows, :] for g in range(A_GROUPS)]
        outs = [o_scr[g, rows, :] for g in range(A_GROUPS)]
        a_ref[rows, :] = _group_softmax_merge(lses, outs)
        return carry

    lax.fori_loop(0, A_SUPER // B, merge, 0)


def _attn_prompt(z, slopes, s_len, t_all, aq_blk, ak_blk, av_blk):
    R = A_SUPER
    in_specs = [pl.BlockSpec(memory_space=pltpu.SMEM)]
    args = [slopes]
    for g in range(A_GROUPS):
        off = g * A_HEADS
        cur = lambda h, n, base: (n, base + h)
        prev = lambda h, n, base: (jnp.maximum(n - 1, 0), base + h)
        for base, fn in ((aq_blk + off, cur), (ak_blk + off, cur), (ak_blk + off, prev), (av_blk + off, cur), (av_blk + off, prev)):
            in_specs.append(pl.BlockSpec((R, A_DH), functools.partial(fn, base=base)))
            args.append(z)
    in_specs.append(pl.BlockSpec(memory_space=pl.ANY))
    args.append(jnp.zeros((t_all, A_HEADS * A_DH), F32))
    return pl.pallas_call(
        _attn_prompt_kernel,
        grid=(A_HEADS, s_len // R),
        in_specs=in_specs,
        out_specs=pl.BlockSpec((R, A_DH), lambda h, n: (n, h)),
        out_shape=jax.ShapeDtypeStruct((t_all, A_HEADS * A_DH), F32),
        scratch_shapes=[pltpu.VMEM((A_GROUPS, R, A_DH), F32), pltpu.VMEM((A_GROUPS, R, LANES), F32)],
        input_output_aliases={len(args) - 1: 0},
        compiler_params=_cparams("parallel", "arbitrary"),
        name="attn_prompt",
    )(*args)


def _attn_sample_kernel(sl_ref, *refs, seq):
    G = A_GROUPS
    z_refs = refs[:3 * G]
    buf_refs = refs[3 * G:4 * G]
    new_refs = refs[4 * G:5 * G]
    a_ref = refs[5 * G + 1]
    cache_out_refs = refs[5 * G + 2:6 * G + 2]
    sems = refs[6 * G + 2]
    b = pl.program_id(0)
    half = b % (SUBLANES // seq)
    PADK = LANES

    rows_per_pos = 2 * A_HEADS
    copies = []
    for g in range(G):
        kept = buf_refs[g].shape[1] - seq * rows_per_pos
        copies.append(pltpu.make_async_copy(buf_refs[g].at[0, pl.ds(seq * rows_per_pos, kept), :],
                                            cache_out_refs[g].at[b, pl.ds(0, kept), :], sems.at[2 * g]))
        copies.append(pltpu.make_async_copy(new_refs[g].at[0],
                                            cache_out_refs[g].at[b, pl.ds(kept, seq * rows_per_pos), :],
                                            sems.at[2 * g + 1]))
    for cp in copies:
        cp.start()

    for h in range(A_HEADS):
        cols = slice(h * A_DH, (h + 1) * A_DH)
        lses, outs = [], []
        for g, dil in enumerate(A_DILATIONS):
            q_ref, kn_ref, vn_ref = z_refs[3 * g:3 * g + 3]
            buf_ref = buf_refs[g]
            lb = A_STEPS * dil
            slope = sl_ref[g, h]
            kbuf = buf_ref[0, pl.ds(h, lb, stride=2 * A_HEADS), :]
            vbuf = buf_ref[0, pl.ds(A_HEADS + h, lb, stride=2 * A_HEADS), :]
            pad = jnp.zeros((PADK - SUBLANES, A_DH), F32)
            kx = jnp.concatenate([kbuf, kn_ref[:, cols], pad], axis=0).astype(BF16)
            vx = jnp.concatenate([vbuf, vn_ref[:, cols], pad], axis=0).astype(BF16)
            s = _dot_nt(q_ref[:, cols].astype(BF16), kx) * ATT_SCALE
            col = lax.broadcasted_iota(jnp.int32, s.shape, 1)
            t = lax.broadcasted_iota(jnp.int32, s.shape, 0) - seq * half
            t_new = col - lb - seq * half
            in_buf = col < lb
            pos = jnp.where(in_buf, col, lb + t_new)
            delta = lb + t - pos
            valid = (in_buf | ((t_new >= 0) & (t_new < seq))) & (delta >= 0) & (delta <= A_STEPS * dil) \
                & ((delta & (dil - 1)) == 0)
            s = jnp.where(valid, s - slope * delta.astype(F32), NEG_INF)
            mx = jnp.max(s, axis=1, keepdims=True)
            e = jnp.exp(s - mx)
            den = jnp.sum(e, axis=1, keepdims=True)
            outs.append(_dot(e.astype(BF16), vx) * (1.0 / den))
            lses.append(mx + jnp.log(den))
        merged = _group_softmax_merge(lses, outs)
        for hv in range(SUBLANES // seq):
            @pl.when(half == hv)
            def _(merged=merged, hv=hv):
                a_ref[hv * seq:(hv + 1) * seq, cols] = merged[hv * seq:(hv + 1) * seq, :]

    for cp in copies:
        cp.wait()


def _attn_sample(z, slopes, bufs, kv_new, a_out, row0, aq_blk, ak_blk, av_blk, seq):
    B = bufs[0].shape[0]
    W = A_HEADS * A_DH
    per_blk = SUBLANES // seq
    rb0 = row0 // SUBLANES
    in_specs = [pl.BlockSpec(memory_space=pltpu.SMEM)]
    args = [slopes]
    for g in range(A_GROUPS):
        for base in (aq_blk + g, ak_blk + g, av_blk + g):
            in_specs.append(pl.BlockSpec((SUBLANES, W), functools.partial(lambda b, base: (rb0 + b // per_blk, base), base=base)))
            args.append(z)
    for arrs in (bufs, kv_new):
        for g in range(A_GROUPS):
            in_specs.append(pl.BlockSpec((1,) + arrs[g].shape[1:], lambda b: (b, 0, 0)))
            args.append(arrs[g])
    in_specs.append(pl.BlockSpec(memory_space=pl.ANY))
    args.append(a_out)
    outs = pl.pallas_call(
        functools.partial(_attn_sample_kernel, seq=seq),
        grid=(B,),
        in_specs=in_specs,
        out_specs=[pl.BlockSpec((SUBLANES, W), lambda b: (rb0 + b // per_blk, 0))]
        + [pl.BlockSpec(memory_space=pl.ANY)] * A_GROUPS,
        out_shape=[jax.ShapeDtypeStruct(a_out.shape, a_out.dtype)]
        + [jax.ShapeDtypeStruct(c.shape, c.dtype) for c in bufs],
        scratch_shapes=[pltpu.SemaphoreType.DMA((2 * A_GROUPS,))],
        input_output_aliases={len(args) - 1: 0},
        compiler_params=_cparams("arbitrary"),
        name="attn_sample",
    )(*args)
    return outs[0], outs[1:]


ST_TAU, ST_MAX0, ST_MAX1, ST_INVZ = 0, 1, 2, 3
ST_ROWS = SUBLANES
PEER_TOKEN_CHUNK = 256
PEER_KEY_ROWS = 32
PEER_KEY_GROUPS = PEER_KEY_ROWS // SUBLANES


def _take_max(sc, out, k):
    rows = lax.broadcasted_iota(jnp.int32, sc.shape, 0)
    out_rows = lax.broadcasted_iota(jnp.int32, out.shape, 0)
    m = jnp.max(sc, axis=0, keepdims=True)
    first = jnp.min(jnp.where(sc == m, rows, sc.shape[0]), axis=0, keepdims=True)
    return jnp.where(rows == first, NEG_INF, sc), jnp.where(out_rows == k, m, out)


def _pair_candidates(top0, top1):
    assert P_TOPK == 16 and SUBLANES == 8
    lo0, hi0, lo1, hi1 = top0[0:8], top0[8:16], top1[0:8], top1[8:16]
    skip = lax.broadcasted_iota(jnp.int32, lo0.shape, 0) < 4
    blocks = [top0[0:1] + lo1, top0[0:1] + hi1]
    blocks += [top0[i:i + 1] + lo1 for i in (1, 2, 3)]
    blocks += [jnp.where(skip, NEG_INF, lo0 + top1[0:1]), hi0 + top1[0:1]]
    blocks += [jnp.where(skip, NEG_INF, lo0 + top1[j:j + 1]) for j in (1, 2, 3)]
    return jnp.concatenate(blocks, axis=0)


N_PAIR_ROWS = 10 * SUBLANES


def _peer_scores_kernel(x_ref, wq_ref, sk_ref, sc_ref, st_ref, q_scr):
    tm = x_ref.shape[0]
    dkh = sk_ref.shape[2]
    q_scr[...] = _dot(x_ref[...], wq_ref[...]).astype(BF16)
    for hp in range(2 * P_HEADS):
        sc_ref[hp] = _dot_nt(sk_ref[hp], q_scr[:, hp * dkh:(hp + 1) * dkh])

    def per_chunk(ci, carry):
        lanes = pl.ds(pl.multiple_of(ci * LANES, LANES), LANES)
        empty = jnp.zeros((P_TOPK, LANES), F32)

        def per_head(h, carry):
            cand, max0, max1 = carry
            hh = jnp.minimum(h, P_HEADS - 1)
            sc0 = sc_ref[2 * hh, :, lanes]
            sc1 = sc_ref[2 * hh + 1, :, lanes]
            top0, top1, best = empty, empty, empty
            for k in range(P_TOPK):
                sc0, top0 = _take_max(sc0, top0, k)
                sc1, top1 = _take_max(sc1, top1, k)
                cand, best = _take_max(cand, best, k)
            z = jnp.sum(jnp.exp(best - best[0:1, :]), axis=0, keepdims=True)
            st_ref[jnp.maximum(h - 1, 0), :, lanes] = jnp.concatenate(
                [best[P_TOPK - 1:P_TOPK, :], max0, max1, 1.0 / z, jnp.zeros((ST_ROWS - 4, LANES), F32)], axis=0)
            return _pair_candidates(top0, top1), top0[0:1, :], top1[0:1, :]

        row = jnp.zeros((1, LANES), F32)
        lax.fori_loop(0, P_HEADS + 1, per_head, (jnp.zeros((N_PAIR_ROWS, LANES), F32), row, row))
        return carry

    lax.fori_loop(0, tm // LANES, per_chunk, 0)


def _peer_scores(xb, wq, sk, tm):
    T, D = xb.shape
    nq = wq.shape[1]
    return pl.pallas_call(
        _peer_scores_kernel,
        grid=(T // tm,),
        in_specs=[
            pl.BlockSpec((tm, D), lambda i: (i, 0)),
            pl.BlockSpec((D, nq), lambda i: (0, 0)),
            pl.BlockSpec(sk.shape, lambda i: (0, 0, 0)),
        ],
        out_specs=[
            pl.BlockSpec((2 * P_HEADS, P_NKEYS, tm), lambda i: (0, 0, i)),
            pl.BlockSpec((P_HEADS, ST_ROWS, tm), lambda i: (0, 0, i)),
        ],
        out_shape=[
            jax.ShapeDtypeStruct((2 * P_HEADS, P_NKEYS, T), F32),
            jax.ShapeDtypeStruct((P_HEADS, ST_ROWS, T), F32),
        ],
        scratch_shapes=[pltpu.VMEM((tm, nq), BF16)],
        compiler_params=_cparams("parallel"),
        name="peer_scores",
    )(xb, wq, sk)


def _gelu(a):
    return 0.5 * a * (1.0 + lax.erf(a * (1.0 / math.sqrt(2.0))))


def _peer_dense_kernel(x_ref, sc_ref, st_ref, u_ref, v_ref, y_ref,
                       e1_scr, tau_scr, row_scr, act_a, act_b, *at_scrs):
    e = pl.program_id(1)
    n_e = pl.num_programs(1) - 1
    tm = x_ref.shape[0]
    te = u_ref.shape[0]
    K = P_NKEYS
    n_ig = te // K

    @pl.when(e == 0)
    def _():
        y_ref[...] = jnp.zeros_like(y_ref)
        act_b[...] = jnp.zeros_like(act_b)
        for h in range(P_HEADS):
            e1_scr[h] = jnp.exp(sc_ref[2 * h + 1] - st_ref[h, ST_MAX1:ST_MAX1 + 1, :])
            tau_scr[h] = jnp.broadcast_to(st_ref[h, ST_TAU:ST_TAU + 1, :], (SUBLANES, tm))

    chunks = [(c0, min(PEER_TOKEN_CHUNK, tm - c0)) for c0 in range(0, tm, PEER_TOKEN_CHUNK)]
    assert len(chunks) == len(at_scrs)
    d_model = v_ref.shape[1]
    n_parts = max(len(chunks) - 1, 1)
    part = d_model // n_parts

    def step(act_cur, act_prev):
        def pre_activations(k):
            c0, cw = chunks[k]
            at_scrs[k][...] = _dot_nt(u_ref[...], x_ref[c0:c0 + cw, :])

        def second_matmul(p):
            cols = slice(p * part, (p + 1) * part)
            y_ref[:, cols] += _dot_tn(act_prev[...], v_ref[:, cols])

        pre_activations(0)
        for p in range(n_parts):
            second_matmul(p)
            if p + 1 < len(chunks):
                pre_activations(p + 1)

        i0 = jnp.minimum(e, n_e - 1) * n_ig
        sub = lax.broadcasted_iota(jnp.int32, (SUBLANES, tm), 0)
        for ig in range(n_ig):
            i = i0 + ig
            i_grp = i // SUBLANES
            for h in range(P_HEADS):
                s0 = jnp.sum(jnp.where(sub == i - i_grp * SUBLANES, sc_ref[2 * h, i_grp], 0.0), axis=0, keepdims=True)
                w0 = jnp.exp(s0 - st_ref[h, ST_MAX0:ST_MAX0 + 1, :]) * st_ref[h, ST_INVZ:ST_INVZ + 1, :]
                row_scr[h, ig, 0] = jnp.broadcast_to(s0, (SUBLANES, tm))
                row_scr[h, ig, 1] = jnp.broadcast_to(w0, (SUBLANES, tm))

        for k, (c0, cw) in enumerate(chunks):
            at_scr = at_scrs[k]
            for l0 in range(c0, c0 + cw, LANES):
                lanes = slice(l0, l0 + LANES)
                at_lanes = slice(l0 - c0, l0 - c0 + LANES)
                for jq in range(K // PEER_KEY_ROWS):
                    jg = slice(jq * PEER_KEY_GROUPS, (jq + 1) * PEER_KEY_GROUPS)
                    gates = [jnp.zeros((PEER_KEY_GROUPS, SUBLANES, LANES), F32) for _ in range(n_ig)]
                    for h in range(P_HEADS):
                        s1 = sc_ref[2 * h + 1, jg, :, lanes]
                        e1 = e1_scr[h, jg, :, lanes]
                        tau = tau_scr[h, :, lanes]
                        for ig in range(n_ig):
                            total = s1 + row_scr[h, ig, 0, :, lanes]
                            gates[ig] = gates[ig] + jnp.where(total >= tau, e1 * row_scr[h, ig, 1, :, lanes], 0.0)
                    for ig in range(n_ig):
                        rows = slice(ig * K + jq * PEER_KEY_ROWS, ig * K + (jq + 1) * PEER_KEY_ROWS)
                        gate = gates[ig].reshape(PEER_KEY_ROWS, LANES)
                        act_cur[rows, lanes] = (_gelu(at_scr[rows, at_lanes]) * gate).astype(BF16)

    pl.when(e % 2 == 0)(lambda: step(act_a, act_b))
    pl.when(e % 2 == 1)(lambda: step(act_b, act_a))


def _peer_dense(xb, sc, st, ub, vb, tm, te):
    T, D = xb.shape
    E = ub.shape[0]
    n_e = E // te
    once = pl.Buffered(1)
    return pl.pallas_call(
        _peer_dense_kernel,
        grid=(T // tm, n_e + 1),
        in_specs=[
            pl.BlockSpec((tm, D), lambda i, e: (i, 0), pipeline_mode=once),
            pl.BlockSpec((2 * P_HEADS, P_NKEYS // SUBLANES, SUBLANES, tm), lambda i, e: (0, 0, 0, i), pipeline_mode=once),
            pl.BlockSpec((P_HEADS, ST_ROWS, tm), lambda i, e: (0, 0, i)),
            pl.BlockSpec((te, D), lambda i, e: (jnp.minimum(e, n_e - 1), 0)),
            pl.BlockSpec((te, D), lambda i, e: (jnp.maximum(e - 1, 0), 0)),
        ],
        out_specs=pl.BlockSpec((tm, D), lambda i, e: (i, 0)),
        out_shape=jax.ShapeDtypeStruct((T, D), F32),
        scratch_shapes=[
            pltpu.VMEM((P_HEADS, P_NKEYS // SUBLANES, SUBLANES, tm), F32),
            pltpu.VMEM((P_HEADS, SUBLANES, tm), F32),
            pltpu.VMEM((P_HEADS, te // P_NKEYS, 2, SUBLANES, tm), F32),
            pltpu.VMEM((te, tm), BF16),
            pltpu.VMEM((te, tm), BF16),
        ] + [pltpu.VMEM((te, min(PEER_TOKEN_CHUNK, tm - c0)), F32) for c0 in range(0, tm, PEER_TOKEN_CHUNK)],
        compiler_params=_cparams("parallel", "arbitrary"),
        name="peer_dense",
    )(xb, sc.reshape(2 * P_HEADS, P_NKEYS // SUBLANES, SUBLANES, T), st, ub, vb)


def _tile(n, *candidates):
    for c in candidates:
        if n % c == 0:
            return c
    raise ValueError(f"no tile of {candidates} divides {n}")


def kernel(x_prompt, x_sample, state_C, state_n, state_m, cache_kv_w128, cache_kv_w512, cache_kv_w2048,
           w_in, b_in, mlstm_norm_w, w_branch_m, w_branch_a, w_out, ln1_g, ln1_b,
           w_query, sub_keys, expert_u, expert_v, ln2_g, ln2_b):
    bp, S, D = x_prompt.shape
    DB, DS, _ = x_sample.shape
    H = M_HEADS
    dv, dk = D // 8, D // 16
    AW = A_HEADS * A_DH
    caches = (cache_kv_w128, cache_kv_w512, cache_kv_w2048)
    assert bp == 1 and DB * DS == M_CHUNK and S % A_SUPER == 0
    assert all(c.shape[1] == A_STEPS * d for c, d in zip(caches, A_DILATIONS))
    T = S + DB * DS

    x = jnp.concatenate([x_prompt.reshape(S, D), x_sample.reshape(DB * DS, D)], axis=0)
    xb = x.astype(BF16)

    g_lo = 2 * H * dk + 2 * H * dv
    g_hi = g_lo + 2 * H
    w_in_t = w_in.T
    b_main = jnp.concatenate([b_in[:g_lo], b_in[g_hi:]])[None, :]
    w_gate_t = jnp.pad(w_in_t[g_lo:g_hi], ((0, LANES - 2 * H), (0, 0)))
    b_gate = jnp.pad(b_in[g_lo:g_hi], (0, LANES - 2 * H))[None, :]
    tm = _tile(T, 640, 128)
    z = _proj(xb, w_in_t, b_main, tm, 512, g_lo, g_hi)
    gates = _gates(x, w_gate_t, b_gate, tm)
    gates_t = gates[:, :2 * H].T
    aq_col, ak_col, av_col = g_lo, g_lo + A_GROUPS * AW, g_lo + 2 * A_GROUPS * AW
    gm_col = g_lo + 3 * A_GROUPS * AW
    ga_col = gm_col + D

    nw = mlstm_norm_w[None, :]
    m_out, c_p, n_p, m_p = _mlstm_prompt(z, gates, gates_t, nw, S, T)
    m0_rows = jnp.pad(jnp.repeat(state_m, DS, axis=0), ((0, 0), (0, LANES - H)))
    m_out, c_s, n_s, mt_s = _mlstm_sample(z, gates, gates_t, nw, m0_rows, state_C,
                                          state_n.reshape(DB * H, 1, dk), m_out, S)

    n_slopes = A_GROUPS * A_HEADS
    slopes = (2.0 ** (-8.0 * (jnp.arange(n_slopes, dtype=F32) + 1.0) / n_slopes)).reshape(A_GROUPS, A_HEADS)
    a_out = _attn_prompt(z, slopes, S, T, aq_col // A_DH, ak_col // A_DH, av_col // A_DH)
    bufs = [c.reshape(DB, c.shape[1] * 2 * A_HEADS, A_DH) for c in caches]
    kv_new = []
    for g in range(A_GROUPS):
        kg = z[S:, ak_col + g * AW:ak_col + (g + 1) * AW]
        vg = z[S:, av_col + g * AW:av_col + (g + 1) * AW]
        kv_new.append(jnp.stack([kg, vg], axis=1).reshape(DB, DS * 2 * A_HEADS, A_DH))
    a_out, kv_s = _attn_sample(z, slopes, bufs, kv_new, a_out, S, aq_col // AW, ak_col // AW, av_col // AW, DS)
    kv_s = [k.reshape(c.shape) for k, c in zip(kv_s, caches)]

    merged = _merge(m_out, a_out, w_branch_m.astype(BF16), w_branch_a.astype(BF16), z,
                    gm_col // 1024, ga_col // 1024, tm, 1024)
    x1, x1b = _outproj_ln(merged, w_out.astype(BF16), x, ln1_g[None, :], ln1_b[None, :], _tile(T, 320, 128), 512)
    sk = sub_keys.reshape(2 * P_HEADS, P_NKEYS, sub_keys.shape[-1]).astype(BF16)
    sc, st = _peer_scores(x1b, w_query.astype(BF16), sk, tm)
    y = _peer_dense(x1b, sc, st, expert_u.astype(BF16), expert_v.astype(BF16), tm, 512)
    x2 = _residual_ln(x1, y, ln2_g[None, :], ln2_b[None, :], _tile(T, 320, 128))

    y_prompt = x2[:S].reshape(1, S, D)
    y_sample = x2[S:].reshape(DB, DS, D)
    kv_p = []
    for g, cache in enumerate(caches):
        win = cache.shape[1]
        kg = z[S - win:S, ak_col + g * AW:ak_col + (g + 1) * AW]
        vg = z[S - win:S, av_col + g * AW:av_col + (g + 1) * AW]
        kv_p.append(jnp.stack([kg, vg], axis=1).reshape(1, win, 2, A_HEADS, A_DH))
    m_s = mt_s[:, DS - 1::DS, 0].T
    return (y_prompt, y_sample, c_p, n_p.reshape(1, H, dk), m_p[:, 0, 0][None, :],
            kv_p[0], kv_p[1], kv_p[2],
            c_s, n_s.reshape(DB, H, dk), m_s, kv_s[0], kv_s[1], kv_s[2])
```

```python
import functools
import math

import jax
import jax.numpy as jnp
from jax import lax
from jax.experimental import pallas as pl
from jax.experimental.pallas import tpu as pltpu

F32 = jnp.float32
BF16 = jnp.bfloat16

M_HEADS = 8
M_CHUNK = 128
M_HEADS_PER_STEP = 2
A_DILATIONS = (1, 4, 16)
A_GROUPS = len(A_DILATIONS)
A_HEADS = 8
A_DH = 128
A_STEPS = 128
P_HEADS = 8
P_NKEYS = 128
P_TOPK = 16
DEPTH = 1
ALPHA = (2 * DEPTH) ** 0.25
LN_EPS = 1e-5

LANES = 128
SUBLANES = 8
VMEM_LIMIT = 56 * 1024 * 1024
NEG_INF = float("-inf")


def _cparams(*sem):
    return pltpu.CompilerParams(dimension_semantics=sem, vmem_limit_bytes=VMEM_LIMIT)


def _dot(a, b):
    return jnp.dot(a, b, preferred_element_type=F32)


def _dot_nt(a, b):
    return lax.dot_general(a, b, (((1,), (1,)), ((), ())), preferred_element_type=F32)


def _dot_tn(a, b):
    return lax.dot_general(a, b, (((0,), (0,)), ((), ())), preferred_element_type=F32)


def _proj_kernel(x_ref, wt_hbm, b_ref, o_ref, stage, w_scr, sem, *, first_shifted, skip):
    j = pl.program_id(0)
    i = pl.program_id(1)
    tn = stage.shape[0]

    def fetch(jj):
        row0 = pl.multiple_of(jj * tn + jnp.where(jj >= first_shifted, skip, 0), SUBLANES)
        return pltpu.make_async_copy(wt_hbm.at[pl.ds(row0, tn), :], stage, sem)

    @pl.when((i == 0) & (j == 0))
    def _():
        fetch(j).start()

    @pl.when(i == 0)
    def _():
        fetch(j).wait()
        w_scr[...] = stage[...].astype(BF16)

        @pl.when(j + 1 < pl.num_programs(0))
        def _():
            fetch(j + 1).start()

    o_ref[...] = _dot_nt(x_ref[...], w_scr[...]) + b_ref[...]


def _proj(xb, wt, b, tm, tn, skip_lo, skip_hi):
    T, K = xb.shape
    skip = skip_hi - skip_lo
    N = wt.shape[0] - skip
    assert skip_lo % tn == 0 and skip % SUBLANES == 0 and N % tn == 0 and b.shape == (1, N)
    return pl.pallas_call(
        functools.partial(_proj_kernel, first_shifted=skip_lo // tn, skip=skip),
        grid=(N // tn, T // tm),
        in_specs=[
            pl.BlockSpec((tm, K), lambda j, i: (i, 0)),
            pl.BlockSpec(memory_space=pl.ANY),
            pl.BlockSpec((1, tn), lambda j, i: (0, j)),
        ],
        out_specs=pl.BlockSpec((tm, tn), lambda j, i: (i, j)),
        out_shape=jax.ShapeDtypeStruct((T, N), F32),
        scratch_shapes=[pltpu.VMEM((tn, K), F32), pltpu.VMEM((tn, K), BF16), pltpu.SemaphoreType.DMA(())],
        compiler_params=_cparams("arbitrary", "arbitrary"),
        name="in_proj",
    )(xb, wt, b)


def _split_bf16(a):
    hi = a.astype(BF16)
    lo = (a - hi.astype(F32)).astype(BF16)
    return hi, lo


def _gates_kernel(x_ref, w_ref, b_ref, o_ref):
    xh, xl = _split_bf16(x_ref[...])
    wh, wl = _split_bf16(w_ref[...])
    o_ref[...] = _dot_nt(xh, wh) + _dot_nt(xh, wl) + _dot_nt(xl, wh) + b_ref[...]


def _gates(x, wgt, bg, tm):
    T, K = x.shape
    return pl.pallas_call(
        _gates_kernel,
        grid=(T // tm,),
        in_specs=[
            pl.BlockSpec((tm, K), lambda i: (i, 0)),
            pl.BlockSpec((LANES, K), lambda i: (0, 0)),
            pl.BlockSpec((1, LANES), lambda i: (0, 0)),
        ],
        out_specs=pl.BlockSpec((tm, LANES), lambda i: (i, 0)),
        out_shape=jax.ShapeDtypeStruct((T, LANES), F32),
        compiler_params=_cparams("parallel"),
        name="gate_proj",
    )(x, wgt, bg)


def _merge_kernel(m_ref, a_ref, wm_ref, wa_ref, gm_ref, ga_ref, o_ref):
    bm = _dot(m_ref[...], wm_ref[...])
    ba = _dot(a_ref[...].astype(BF16), wa_ref[...])
    o_ref[...] = (jax.nn.sigmoid(gm_ref[...]) * bm + jax.nn.sigmoid(ga_ref[...]) * ba).astype(o_ref.dtype)


def _merge(m_out, a_out, wm, wa, z, gm_blk, ga_blk, tm, tn):
    T, Km = m_out.shape
    Ka = a_out.shape[1]
    N = wm.shape[1]
    return pl.pallas_call(
        _merge_kernel,
        grid=(N // tn, T // tm),
        in_specs=[
            pl.BlockSpec((tm, Km), lambda j, i: (i, 0)),
            pl.BlockSpec((tm, Ka), lambda j, i: (i, 0)),
            pl.BlockSpec((Km, tn), lambda j, i: (0, j)),
            pl.BlockSpec((Ka, tn), lambda j, i: (0, j)),
            pl.BlockSpec((tm, tn), lambda j, i: (i, gm_blk + j)),
            pl.BlockSpec((tm, tn), lambda j, i: (i, ga_blk + j)),
        ],
        out_specs=pl.BlockSpec((tm, tn), lambda j, i: (i, j)),
        out_shape=jax.ShapeDtypeStruct((T, N), BF16),
        compiler_params=_cparams("parallel", "parallel"),
        name="branch_merge",
    )(m_out, a_out, wm, wa, z, z)


LN_ROWS = 32


def _layer_norm_rows(v, g, b):
    mu = jnp.mean(v, axis=-1, keepdims=True)
    c = v - mu
    var = jnp.mean(c * c, axis=-1, keepdims=True)
    return c * lax.rsqrt(var + LN_EPS) * g + b


def _outproj_ln_kernel(a_ref, w_ref, x_ref, g_ref, b_ref, o_ref, ob_ref):
    k = pl.program_id(1)

    @pl.when(k == 0)
    def _():
        o_ref[...] = ALPHA * x_ref[...] + _dot(a_ref[...], w_ref[...])

    @pl.when(k > 0)
    def _():
        o_ref[...] += _dot(a_ref[...], w_ref[...])

    @pl.when(k == pl.num_programs(1) - 1)
    def _():
        def norm_rows(r, carry):
            rows = pl.ds(pl.multiple_of(r * LN_ROWS, LN_ROWS), LN_ROWS)
            y = _layer_norm_rows(o_ref[rows, :], g_ref[...], b_ref[...])
            o_ref[rows, :] = y
            ob_ref[rows, :] = y.astype(BF16)
            return carry

        lax.fori_loop(0, o_ref.shape[0] // LN_ROWS, norm_rows, 0)


def _outproj_ln(a, w, x, g, b, tm, tk):
    T, K = a.shape
    D = w.shape[1]
    return pl.pallas_call(
        _outproj_ln_kernel,
        grid=(T // tm, K // tk),
        in_specs=[
            pl.BlockSpec((tm, tk), lambda i, k: (i, k)),
            pl.BlockSpec((tk, D), lambda i, k: (k, 0)),
            pl.BlockSpec((tm, D), lambda i, k: (i, 0), pipeline_mode=pl.Buffered(1)),
            pl.BlockSpec((1, D), lambda i, k: (0, 0)),
            pl.BlockSpec((1, D), lambda i, k: (0, 0)),
        ],
        out_specs=[
            pl.BlockSpec((tm, D), lambda i, k: (i, 0)),
            pl.BlockSpec((tm, D), lambda i, k: (i, 0)),
        ],
        out_shape=[jax.ShapeDtypeStruct((T, D), F32), jax.ShapeDtypeStruct((T, D), BF16)],
        compiler_params=_cparams("parallel", "arbitrary"),
        name="out_proj_ln",
    )(a, w, x, g, b)


def _residual_ln_kernel(x_ref, y_ref, g_ref, b_ref, head_ref, tail_ref, *, n_head):
    i = pl.program_id(0)

    def emit(dst_ref):
        def norm_rows(r, carry):
            rows = pl.ds(pl.multiple_of(r * LN_ROWS, LN_ROWS), LN_ROWS)
            dst_ref[rows, :] = _layer_norm_rows(ALPHA * x_ref[rows, :] + y_ref[rows, :], g_ref[...], b_ref[...])
            return carry

        lax.fori_loop(0, x_ref.shape[0] // LN_ROWS, norm_rows, 0)

    pl.when(i < n_head)(lambda: emit(head_ref))
    pl.when(i >= n_head)(lambda: emit(tail_ref))


def _residual_ln(x, y, g, b, tm, n_first):
    T, D = x.shape
    assert n_first % tm == 0 and (T - n_first) % tm == 0
    n_head = n_first // tm
    return pl.pallas_call(
        functools.partial(_residual_ln_kernel, n_head=n_head),
        grid=(T // tm,),
        in_specs=[
            pl.BlockSpec((tm, D), lambda i: (i, 0)),
            pl.BlockSpec((tm, D), lambda i: (i, 0)),
            pl.BlockSpec((1, D), lambda i: (0, 0)),
            pl.BlockSpec((1, D), lambda i: (0, 0)),
        ],
        out_specs=[
            pl.BlockSpec((tm, D), lambda i: (jnp.minimum(i, n_head - 1), 0)),
            pl.BlockSpec((tm, D), lambda i: (jnp.maximum(i - n_head, 0), 0)),
        ],
        out_shape=[jax.ShapeDtypeStruct((n_first, D), F32), jax.ShapeDtypeStruct((T - n_first, D), F32)],
        compiler_params=_cparams("arbitrary"),
        name="residual_ln",
    )(x, y, g, b)


def _log_sigmoid(x):
    return jnp.minimum(x, 0.0) - jnp.log1p(jnp.exp(-jnp.abs(x)))


def _lane_column(a, col):
    lane = lax.broadcasted_iota(jnp.int32, a.shape, 1)
    return jnp.sum(jnp.where(lane == col, a, 0.0), axis=1, keepdims=True)


def _mlstm_gate_math(ig_col, fg_col, ig_row, fg_row, m0_col, seg_shift):
    L = M_CHUNK
    t = lax.broadcasted_iota(jnp.int32, (L, L), 0)
    s = lax.broadcasted_iota(jnp.int32, (L, L), 1)
    lower, upper = s <= t, t <= s
    if seg_shift is not None:
        same = (t >> seg_shift) == (s >> seg_shift)
        lower, upper = lower & same, upper & same
    lf_col, lf_row = _log_sigmoid(fg_col), _log_sigmoid(fg_row)
    f_col = jnp.sum(jnp.where(lower, lf_row, 0.0), axis=1, keepdims=True)
    f_row = jnp.sum(jnp.where(upper, lf_col, 0.0), axis=0, keepdims=True)
    d = jnp.where(lower, f_col - f_row + ig_row, NEG_INF)
    inter = f_col + m0_col
    m_t = jnp.maximum(inter, jnp.max(d, axis=1, keepdims=True))
    w_intra = jnp.exp(d - m_t)
    w_inter = jnp.exp(inter - m_t)
    return f_col, f_row, m_t, w_intra, w_inter


def _mlstm_head_out(num, den, m_t, nw_row, mo):
    hid = num * (1.0 / jnp.maximum(jnp.abs(den), jnp.exp(-m_t)))
    mu = jnp.mean(hid, axis=-1, keepdims=True)
    c = hid - mu
    var = jnp.mean(c * c, axis=-1, keepdims=True)
    return c * lax.rsqrt(var + LN_EPS) * nw_row * jax.nn.sigmoid(mo)


def _mlstm_prompt_kernel(q_ref, k_ref, v_ref, mo_ref, g_ref, gt_ref, nw_ref, init_ref,
                         mout_ref, c_out_ref, n_out_ref, m_out_ref, c_scr, n_scr, m_scr):
    del init_ref
    hp = pl.program_id(0)
    c = pl.program_id(1)
    L = M_CHUNK
    HP = M_HEADS_PER_STEP
    dk = q_ref.shape[1] // HP
    dv = v_ref.shape[1] // HP

    @pl.when(c == 0)
    def _():
        c_scr[...] = jnp.zeros_like(c_scr)
        n_scr[...] = jnp.zeros_like(n_scr)
        m_scr[...] = jnp.zeros_like(m_scr)

    g = g_ref[...]
    for hh in range(HP):
        h = hp * HP + hh
        kcols = slice(hh * dk, (hh + 1) * dk)
        vcols = slice(hh * dv, (hh + 1) * dv)
        ig_col = _lane_column(g, h)
        fg_col = _lane_column(g, M_HEADS + h)
        ig_row = gt_ref[pl.ds(h, 1), :]
        fg_row = gt_ref[pl.ds(M_HEADS + h, 1), :]
        f_col, _, m_t, w_intra, w_inter = _mlstm_gate_math(ig_col, fg_col, ig_row, fg_row, m_scr[hh], None)

        q = q_ref[:, kcols]
        kf = k_ref[:, kcols] * (1.0 / math.sqrt(dk))
        qb, kb, vb = q.astype(BF16), kf.astype(BF16), v_ref[:, vcols].astype(BF16)
        c_prev = c_scr[hh]
        n_prev = n_scr[hh]
        s = _dot_nt(qb, kb) * w_intra
        num = w_inter * _dot(qb, c_prev.astype(BF16)) + _dot(s.astype(BF16), vb)
        den = w_inter * jnp.sum(q * n_prev, axis=1, keepdims=True) + jnp.sum(s, axis=1, keepdims=True)
        mout_ref[:, vcols] = _mlstm_head_out(num, den, m_t, nw_ref[:, vcols], mo_ref[:, vcols]).astype(mout_ref.dtype)

        m_end = m_t[L - 1:L, :]
        w_state = w_inter[L - 1:L, :]
        w_tok = jnp.exp(f_col[L - 1:L, :] - f_col + ig_col - m_end)
        kw = kf * w_tok
        c_scr[hh] = w_state * c_prev + _dot_tn(kw.astype(BF16), vb)
        n_scr[hh] = w_state * n_prev + jnp.sum(kw, axis=0, keepdims=True)
        m_scr[hh] = m_end

    @pl.when(c == pl.num_programs(1) - 1)
    def _():
        for hh in range(HP):
            c_out_ref[0, hh] = c_scr[hh]
            n_out_ref[hh] = n_scr[hh]
            m_out_ref[hh] = jnp.broadcast_to(m_scr[hh], (SUBLANES, LANES))


def _mlstm_prompt(z, g, gt, nw, s_len, t_all):
    dv = nw.shape[1] // M_HEADS
    dk = dv // 2
    L = M_CHUNK
    H = M_HEADS
    HP = M_HEADS_PER_STEP
    G = H // HP
    return pl.pallas_call(
        _mlstm_prompt_kernel,
        grid=(G, s_len // L),
        in_specs=[
            pl.BlockSpec((L, HP * dk), lambda h, c: (c, h)),
            pl.BlockSpec((L, HP * dk), lambda h, c: (c, G + h)),
            pl.BlockSpec((L, HP * dv), lambda h, c: (c, G + h)),
            pl.BlockSpec((L, HP * dv), lambda h, c: (c, 2 * G + h)),
            pl.BlockSpec((L, LANES), lambda h, c: (c, 0)),
            pl.BlockSpec((2 * H, L), lambda h, c: (0, c)),
            pl.BlockSpec((1, HP * dv), lambda h, c: (0, h)),
            pl.BlockSpec(memory_space=pl.ANY),
        ],
        out_specs=[
            pl.BlockSpec((L, HP * dv), lambda h, c: (c, h)),
            pl.BlockSpec((1, HP, dk, dv), lambda h, c: (0, h, 0, 0)),
            pl.BlockSpec((HP, 1, dk), lambda h, c: (h, 0, 0)),
            pl.BlockSpec((HP, SUBLANES, LANES), lambda h, c: (h, 0, 0)),
        ],
        out_shape=[
            jax.ShapeDtypeStruct((t_all, H * dv), BF16),
            jax.ShapeDtypeStruct((1, H, dk, dv), F32),
            jax.ShapeDtypeStruct((H, 1, dk), F32),
            jax.ShapeDtypeStruct((H, SUBLANES, LANES), F32),
        ],
        scratch_shapes=[pltpu.VMEM((HP, dk, dv), F32), pltpu.VMEM((HP, 1, dk), F32), pltpu.VMEM((HP, 1, 1), F32)],
        input_output_aliases={7: 0},
        compiler_params=_cparams("parallel", "arbitrary"),
        name="mlstm_prompt",
    )(z, z, z, z, g, gt, nw, jnp.zeros((t_all, H * dv), BF16))


def _mlstm_sample_kernel(q_ref, k_ref, v_ref, mo_ref, g_ref, gt_ref, nw_ref, m0_ref, c0_ref, n0_ref, alias_ref,
                         mout_ref, c_out_ref, n_out_ref, mt_out_ref,
                         num_scr, den_scr, kw_scr, vb_scr, qb_scr, wi_scr, mt_scr, *, seg_shift):
    del alias_ref
    h = pl.program_id(0)
    b = pl.program_id(1)
    L = M_CHUNK
    seg = 1 << seg_shift

    @pl.when(b == 0)
    def _():
        g = g_ref[...]
        ig_col = _lane_column(g, h)
        fg_col = _lane_column(g, M_HEADS + h)
        ig_row = gt_ref[pl.ds(h, 1), :]
        fg_row = gt_ref[pl.ds(M_HEADS + h, 1), :]
        m0_col = _lane_column(m0_ref[...], h)
        f_col, f_row, m_t, w_intra, w_inter = _mlstm_gate_math(ig_col, fg_col, ig_row, fg_row, m0_col, seg_shift)
        q = q_ref[...]
        kf = k_ref[...] * (1.0 / math.sqrt(q.shape[1]))
        qb, kb, vb = q.astype(BF16), kf.astype(BF16), v_ref[...].astype(BF16)
        s = _dot_nt(qb, kb) * w_intra
        num_scr[...] = _dot(s.astype(BF16), vb)
        den_scr[...] = jnp.sum(s, axis=1, keepdims=True)
        t = lax.broadcasted_iota(jnp.int32, (L, L), 0)
        sidx = lax.broadcasted_iota(jnp.int32, (L, L), 1)
        is_last = sidx == (t | (seg - 1))
        m_row = jnp.sum(jnp.where(t == sidx, m_t, 0.0), axis=0, keepdims=True)
        f_last = jnp.sum(jnp.where(is_last, f_row, 0.0), axis=1, keepdims=True)
        m_end = jnp.sum(jnp.where(is_last, m_row, 0.0), axis=1, keepdims=True)
        w_tok = jnp.exp(f_last - f_col + ig_col - m_end)
        kw_scr[...] = kf * w_tok
        vb_scr[...] = vb
        qb_scr[...] = qb
        wi_scr[...] = w_inter
        mt_scr[...] = m_t

    c_b = c0_ref[0, 0]
    n_b = n0_ref[0]
    row = lax.broadcasted_iota(jnp.int32, (L, 1), 0)
    in_b = (row >> seg_shift) == b
    wi = wi_scr[...]
    qc = _dot(qb_scr[...], c_b.astype(BF16))
    num_scr[...] += jnp.where(in_b, wi * qc, 0.0)
    qn = jnp.sum(q_ref[...] * n_b, axis=1, keepdims=True)
    den_scr[...] += jnp.where(in_b, wi * qn, 0.0)
    w_state = jnp.sum(jnp.where(row == seg * b + (seg - 1), wi, 0.0), axis=0, keepdims=True)
    kw_b = jnp.where(in_b, kw_scr[...], 0.0)
    c_out_ref[0, 0] = w_state * c_b + _dot_tn(kw_b.astype(BF16), vb_scr[...])
    n_out_ref[0] = w_state * n_b + jnp.sum(kw_b, axis=0, keepdims=True)

    @pl.when(b == pl.num_programs(1) - 1)
    def _():
        m_t = mt_scr[...]
        out = _mlstm_head_out(num_scr[...], den_scr[...], m_t, nw_ref[...], mo_ref[...])
        mout_ref[...] = out.astype(mout_ref.dtype)
        mt_out_ref[0] = jnp.broadcast_to(m_t, (L, LANES))


def _mlstm_sample(z, g, gt, nw, m0_rows, c0, n0, m_out, row0):
    dv = nw.shape[1] // M_HEADS
    dk = dv // 2
    L = M_CHUNK
    H = M_HEADS
    B = c0.shape[0]
    seg = L // B
    rb = row0 // L
    kern = functools.partial(_mlstm_sample_kernel, seg_shift=seg.bit_length() - 1)
    return pl.pallas_call(
        kern,
        grid=(H, B),
        in_specs=[
            pl.BlockSpec((L, dk), lambda h, b: (rb, h)),
            pl.BlockSpec((L, dk), lambda h, b: (rb, H + h)),
            pl.BlockSpec((L, dv), lambda h, b: (rb, H + h)),
            pl.BlockSpec((L, dv), lambda h, b: (rb, 2 * H + h)),
            pl.BlockSpec((L, LANES), lambda h, b: (rb, 0)),
            pl.BlockSpec((2 * H, L), lambda h, b: (0, rb)),
            pl.BlockSpec((1, dv), lambda h, b: (0, h)),
            pl.BlockSpec((L, LANES), lambda h, b: (0, 0)),
            pl.BlockSpec((1, 1, dk, dv), lambda h, b: (b, h, 0, 0)),
            pl.BlockSpec((1, 1, dk), lambda h, b: (b * H + h, 0, 0)),
            pl.BlockSpec(memory_space=pl.ANY),
        ],
        out_specs=[
            pl.BlockSpec((L, dv), lambda h, b: (rb, h)),
            pl.BlockSpec((1, 1, dk, dv), lambda h, b: (b, h, 0, 0)),
            pl.BlockSpec((1, 1, dk), lambda h, b: (b * H + h, 0, 0)),
            pl.BlockSpec((1, L, LANES), lambda h, b: (h, 0, 0)),
        ],
        out_shape=[
            jax.ShapeDtypeStruct(m_out.shape, m_out.dtype),
            jax.ShapeDtypeStruct(c0.shape, F32),
            jax.ShapeDtypeStruct(n0.shape, F32),
            jax.ShapeDtypeStruct((H, L, LANES), F32),
        ],
        scratch_shapes=[
            pltpu.VMEM((L, dv), F32), pltpu.VMEM((L, 1), F32), pltpu.VMEM((L, dk), F32), pltpu.VMEM((L, dv), BF16),
            pltpu.VMEM((L, dk), BF16), pltpu.VMEM((L, 1), F32), pltpu.VMEM((L, 1), F32),
        ],
        input_output_aliases={10: 0},
        compiler_params=_cparams("parallel", "arbitrary"),
        name="mlstm_sample",
    )(z, z, z, z, g, gt, nw, m0_rows, c0, n0, m_out)


A_SUPER = A_STEPS * max(A_DILATIONS)
ATT_SCALE = 1.0 / math.sqrt(A_DH)


def _rows(start, size, stride):
    return pl.ds(start, size) if stride == 1 else pl.ds(start, size, stride=stride)


def _group_softmax_merge(lses, outs):
    mx = functools.reduce(jnp.maximum, lses)
    ws = [jnp.exp(l - mx) for l in lses]
    tot = functools.reduce(lambda a, b: a + b, ws)
    acc = functools.reduce(lambda a, b: a + b, [w * o for w, o in zip(ws, outs)])
    return acc / tot


def _attn_prompt_kernel(sl_ref, *refs):
    in_refs = refs[:5 * A_GROUPS]
    a_ref, o_scr, l_scr = refs[5 * A_GROUPS + 1:]
    h = pl.program_id(0)
    n = pl.program_id(1)
    B = A_STEPS
    qi = lax.broadcasted_iota(jnp.int32, (B, B), 0)
    ki = lax.broadcasted_iota(jnp.int32, (B, B), 1)
    valid_prev, valid_cur = ki >= qi, ki <= qi
    step_prev = (qi - ki + B).astype(F32)
    step_cur = (qi - ki).astype(F32)

    for g, dil in enumerate(A_DILATIONS):
        q_ref, kc_ref, kp_ref, vc_ref, vp_ref = in_refs[5 * g:5 * g + 5]
        slope = sl_ref[g, h] * float(dil)
        bias_cur = jnp.where(valid_cur, -slope * step_cur, NEG_INF)
        bias_prev = jnp.where(valid_prev, -slope * step_prev, NEG_INF)
        bias_prev_first = jnp.where(n > 0, bias_prev, NEG_INF)
        nblk = A_SUPER // (B * dil)
        for r in range(dil):
            for blk in range(nblk):
                start = r + dil * B * blk
                rows = _rows(start, B, dil)
                qb = q_ref[rows, :].astype(BF16)
                kc = kc_ref[rows, :].astype(BF16)
                vc = vc_ref[rows, :].astype(BF16)
                if blk > 0:
                    prow = _rows(start - dil * B, B, dil)
                    kp, vp, bp = kc_ref[prow, :], vc_ref[prow, :], bias_prev
                else:
                    prow = _rows(r + dil * B * (nblk - 1), B, dil)
                    kp, vp, bp = kp_ref[prow, :], vp_ref[prow, :], bias_prev_first
                s_prev = _dot_nt(qb, kp.astype(BF16)) * ATT_SCALE + bp
                s_cur = _dot_nt(qb, kc) * ATT_SCALE + bias_cur
                mx = jnp.maximum(jnp.max(s_prev, axis=1, keepdims=True), jnp.max(s_cur, axis=1, keepdims=True))
                e_prev = jnp.exp(s_prev - mx)
                e_cur = jnp.exp(s_cur - mx)
                den = jnp.sum(e_prev, axis=1, keepdims=True) + jnp.sum(e_cur, axis=1, keepdims=True)
                o = (_dot(e_prev.astype(BF16), vp.astype(BF16)) + _dot(e_cur.astype(BF16), vc)) * (1.0 / den)
                o_scr[g, rows, :] = o
                l_scr[g, rows, :] = jnp.broadcast_to(mx + jnp.log(den), (B, LANES))

    def merge(i, carry):
        rows = pl.ds(pl.multiple_of(i * B, B), B)
        lses = [l_scr[g, rows, :] for g in range(A_GROUPS)]
        outs = [o_scr[g, rows, :] for g in range(A_GROUPS)]
        a_ref[rows, :] = _group_softmax_merge(lses, outs)
        return carry

    lax.fori_loop(0, A_SUPER // B, merge, 0)


def _attn_prompt(z, slopes, s_len, t_all, aq_blk, ak_blk, av_blk):
    R = A_SUPER
    in_specs = [pl.BlockSpec(memory_space=pltpu.SMEM)]
    args = [slopes]
    for g in range(A_GROUPS):
        off = g * A_HEADS
        cur = lambda h, n, base: (n, base + h)
        prev = lambda h, n, base: (jnp.maximum(n - 1, 0), base + h)
        for base, fn in ((aq_blk + off, cur), (ak_blk + off, cur), (ak_blk + off, prev), (av_blk + off, cur), (av_blk + off, prev)):
            in_specs.append(pl.BlockSpec((R, A_DH), functools.partial(fn, base=base)))
            args.append(z)
    in_specs.append(pl.BlockSpec(memory_space=pl.ANY))
    args.append(jnp.zeros((t_all, A_HEADS * A_DH), F32))
    return pl.pallas_call(
        _attn_prompt_kernel,
        grid=(A_HEADS, s_len // R),
        in_specs=in_specs,
        out_specs=pl.BlockSpec((R, A_DH), lambda h, n: (n, h)),
        out_shape=jax.ShapeDtypeStruct((t_all, A_HEADS * A_DH), F32),
        scratch_shapes=[pltpu.VMEM((A_GROUPS, R, A_DH), F32), pltpu.VMEM((A_GROUPS, R, LANES), F32)],
        input_output_aliases={len(args) - 1: 0},
        compiler_params=_cparams("parallel", "arbitrary"),
        name="attn_prompt",
    )(*args)


def _attn_sample_kernel(sl_ref, *refs, seq):
    G = A_GROUPS
    z_refs = refs[:3 * G]
    buf_refs = refs[3 * G:4 * G]
    new_refs = refs[4 * G:5 * G]
    a_ref = refs[5 * G + 1]
    cache_out_refs = refs[5 * G + 2:6 * G + 2]
    sems = refs[6 * G + 2]
    b = pl.program_id(0)
    half = b % (SUBLANES // seq)
    PADK = LANES

    rows_per_pos = 2 * A_HEADS
    copies = []
    for g in range(G):
        kept = buf_refs[g].shape[1] - seq * rows_per_pos
        copies.append(pltpu.make_async_copy(buf_refs[g].at[0, pl.ds(seq * rows_per_pos, kept), :],
                                            cache_out_refs[g].at[b, pl.ds(0, kept), :], sems.at[2 * g]))
        copies.append(pltpu.make_async_copy(new_refs[g].at[0],
                                            cache_out_refs[g].at[b, pl.ds(kept, seq * rows_per_pos), :],
                                            sems.at[2 * g + 1]))
    for cp in copies:
        cp.start()

    for h in range(A_HEADS):
        cols = slice(h * A_DH, (h + 1) * A_DH)
        lses, outs = [], []
        for g, dil in enumerate(A_DILATIONS):
            q_ref, kn_ref, vn_ref = z_refs[3 * g:3 * g + 3]
            buf_ref = buf_refs[g]
            lb = A_STEPS * dil
            slope = sl_ref[g, h]
            kbuf = buf_ref[0, pl.ds(h, lb, stride=2 * A_HEADS), :]
            vbuf = buf_ref[0, pl.ds(A_HEADS + h, lb, stride=2 * A_HEADS), :]
            pad = jnp.zeros((PADK - SUBLANES, A_DH), F32)
            kx = jnp.concatenate([kbuf, kn_ref[:, cols], pad], axis=0).astype(BF16)
            vx = jnp.concatenate([vbuf, vn_ref[:, cols], pad], axis=0).astype(BF16)
            s = _dot_nt(q_ref[:, cols].astype(BF16), kx) * ATT_SCALE
            col = lax.broadcasted_iota(jnp.int32, s.shape, 1)
            t = lax.broadcasted_iota(jnp.int32, s.shape, 0) - seq * half
            t_new = col - lb - seq * half
            in_buf = col < lb
            pos = jnp.where(in_buf, col, lb + t_new)
            delta = lb + t - pos
            valid = (in_buf | ((t_new >= 0) & (t_new < seq))) & (delta >= 0) & (delta <= A_STEPS * dil) \
                & ((delta & (dil - 1)) == 0)
            s = jnp.where(valid, s - slope * delta.astype(F32), NEG_INF)
            mx = jnp.max(s, axis=1, keepdims=True)
            e = jnp.exp(s - mx)
            den = jnp.sum(e, axis=1, keepdims=True)
            outs.append(_dot(e.astype(BF16), vx) * (1.0 / den))
            lses.append(mx + jnp.log(den))
        merged = _group_softmax_merge(lses, outs)
        for hv in range(SUBLANES // seq):
            @pl.when(half == hv)
            def _(merged=merged, hv=hv):
                a_ref[hv * seq:(hv + 1) * seq, cols] = merged[hv * seq:(hv + 1) * seq, :]

    for cp in copies:
        cp.wait()


def _attn_sample(z, slopes, bufs, kv_new, a_out, row0, aq_blk, ak_blk, av_blk, seq):
    B = bufs[0].shape[0]
    W = A_HEADS * A_DH
    per_blk = SUBLANES // seq
    rb0 = row0 // SUBLANES
    in_specs = [pl.BlockSpec(memory_space=pltpu.SMEM)]
    args = [slopes]
    for g in range(A_GROUPS):
        for base in (aq_blk + g, ak_blk + g, av_blk + g):
            in_specs.append(pl.BlockSpec((SUBLANES, W), functools.partial(lambda b, base: (rb0 + b // per_blk, base), base=base)))
            args.append(z)
    for arrs in (bufs, kv_new):
        for g in range(A_GROUPS):
            in_specs.append(pl.BlockSpec((1,) + arrs[g].shape[1:], lambda b: (b, 0, 0)))
            args.append(arrs[g])
    in_specs.append(pl.BlockSpec(memory_space=pl.ANY))
    args.append(a_out)
    outs = pl.pallas_call(
        functools.partial(_attn_sample_kernel, seq=seq),
        grid=(B,),
        in_specs=in_specs,
        out_specs=[pl.BlockSpec((SUBLANES, W), lambda b: (rb0 + b // per_blk, 0))]
        + [pl.BlockSpec(memory_space=pl.ANY)] * A_GROUPS,
        out_shape=[jax.ShapeDtypeStruct(a_out.shape, a_out.dtype)]
        + [jax.ShapeDtypeStruct(c.shape, c.dtype) for c in bufs],
        scratch_shapes=[pltpu.SemaphoreType.DMA((2 * A_GROUPS,))],
        input_output_aliases={len(args) - 1: 0},
        compiler_params=_cparams("arbitrary"),
        name="attn_sample",
    )(*args)
    return outs[0], outs[1:]


ST_TAU, ST_MAX0, ST_MAX1, ST_INVZ = 0, 1, 2, 3
ST_ROWS = SUBLANES
PEER_TOKEN_CHUNK = 256
PEER_KEY_ROWS = 32
PEER_KEY_GROUPS = PEER_KEY_ROWS // SUBLANES


def _take_max(sc, out, k):
    rows = lax.broadcasted_iota(jnp.int32, sc.shape, 0)
    out_rows = lax.broadcasted_iota(jnp.int32, out.shape, 0)
    m = jnp.max(sc, axis=0, keepdims=True)
    first = jnp.min(jnp.where(sc == m, rows, sc.shape[0]), axis=0, keepdims=True)
    return jnp.where(rows == first, NEG_INF, sc), jnp.where(out_rows == k, m, out)


def _pair_candidates(top0, top1):
    assert P_TOPK == 16 and SUBLANES == 8
    lo0, hi0, lo1, hi1 = top0[0:8], top0[8:16], top1[0:8], top1[8:16]
    skip = lax.broadcasted_iota(jnp.int32, lo0.shape, 0) < 4
    blocks = [top0[0:1] + lo1, top0[0:1] + hi1]
    blocks += [top0[i:i + 1] + lo1 for i in (1, 2, 3)]
    blocks += [jnp.where(skip, NEG_INF, lo0 + top1[0:1]), hi0 + top1[0:1]]
    blocks += [jnp.where(skip, NEG_INF, lo0 + top1[j:j + 1]) for j in (1, 2, 3)]
    return jnp.concatenate(blocks, axis=0)


N_PAIR_ROWS = 10 * SUBLANES


def _peer_scores_kernel(x_ref, wq_ref, sk_ref, sc_ref, st_ref, q_scr):
    tm = x_ref.shape[0]
    dkh = sk_ref.shape[2]
    q_scr[...] = _dot(x_ref[...], wq_ref[...]).astype(BF16)
    for hp in range(2 * P_HEADS):
        sc_ref[hp] = _dot_nt(sk_ref[hp], q_scr[:, hp * dkh:(hp + 1) * dkh])

    def per_chunk(ci, carry):
        lanes = pl.ds(pl.multiple_of(ci * LANES, LANES), LANES)
        empty = jnp.zeros((P_TOPK, LANES), F32)

        def per_head(h, carry):
            cand, max0, max1 = carry
            hh = jnp.minimum(h, P_HEADS - 1)
            sc0 = sc_ref[2 * hh, :, lanes]
            sc1 = sc_ref[2 * hh + 1, :, lanes]
            top0, top1, best = empty, empty, empty
            for k in range(P_TOPK):
                sc0, top0 = _take_max(sc0, top0, k)
                sc1, top1 = _take_max(sc1, top1, k)
                cand, best = _take_max(cand, best, k)
            z = jnp.sum(jnp.exp(best - best[0:1, :]), axis=0, keepdims=True)
            st_ref[jnp.maximum(h - 1, 0), :, lanes] = jnp.concatenate(
                [best[P_TOPK - 1:P_TOPK, :], max0, max1, 1.0 / z, jnp.zeros((ST_ROWS - 4, LANES), F32)], axis=0)
            return _pair_candidates(top0, top1), top0[0:1, :], top1[0:1, :]

        row = jnp.zeros((1, LANES), F32)
        lax.fori_loop(0, P_HEADS + 1, per_head, (jnp.zeros((N_PAIR_ROWS, LANES), F32), row, row))
        return carry

    lax.fori_loop(0, tm // LANES, per_chunk, 0)


def _peer_scores(xb, wq, sk, tm):
    T, D = xb.shape
    nq = wq.shape[1]
    return pl.pallas_call(
        _peer_scores_kernel,
        grid=(T // tm,),
        in_specs=[
            pl.BlockSpec((tm, D), lambda i: (i, 0)),
            pl.BlockSpec((D, nq), lambda i: (0, 0)),
            pl.BlockSpec(sk.shape, lambda i: (0, 0, 0)),
        ],
        out_specs=[
            pl.BlockSpec((2 * P_HEADS, P_NKEYS, tm), lambda i: (0, 0, i)),
            pl.BlockSpec((P_HEADS, ST_ROWS, tm), lambda i: (0, 0, i)),
        ],
        out_shape=[
            jax.ShapeDtypeStruct((2 * P_HEADS, P_NKEYS, T), F32),
            jax.ShapeDtypeStruct((P_HEADS, ST_ROWS, T), F32),
        ],
        scratch_shapes=[pltpu.VMEM((tm, nq), BF16)],
        compiler_params=_cparams("parallel"),
        name="peer_scores",
    )(xb, wq, sk)


def _gelu(a):
    return 0.5 * a * (1.0 + lax.erf(a * (1.0 / math.sqrt(2.0))))


def _peer_dense_kernel(x_ref, sc_ref, st_ref, u_ref, v_ref, y_ref,
                       e1_scr, tau_scr, row_scr, act_a, act_b, *at_scrs):
    e = pl.program_id(1)
    n_e = pl.num_programs(1) - 1
    tm = x_ref.shape[0]
    te = u_ref.shape[0]
    K = P_NKEYS
    n_ig = te // K

    @pl.when(e == 0)
    def _():
        y_ref[...] = jnp.zeros_like(y_ref)
        act_b[...] = jnp.zeros_like(act_b)
        for h in range(P_HEADS):
            e1_scr[h] = jnp.exp(sc_ref[2 * h + 1] - st_ref[h, ST_MAX1:ST_MAX1 + 1, :])
            tau_scr[h] = jnp.broadcast_to(st_ref[h, ST_TAU:ST_TAU + 1, :], (SUBLANES, tm))

    chunks = [(c0, min(PEER_TOKEN_CHUNK, tm - c0)) for c0 in range(0, tm, PEER_TOKEN_CHUNK)]
    assert len(chunks) == len(at_scrs)
    d_model = v_ref.shape[1]
    n_parts = max(len(chunks) - 1, 1)
    part = d_model // n_parts

    def step(act_cur, act_prev):
        def pre_activations(k):
            c0, cw = chunks[k]
            at_scrs[k][...] = _dot_nt(u_ref[...], x_ref[c0:c0 + cw, :])

        def second_matmul(p):
            cols = slice(p * part, (p + 1) * part)
            y_ref[:, cols] += _dot_tn(act_prev[...], v_ref[:, cols])

        pre_activations(0)
        for p in range(n_parts):
            second_matmul(p)
            if p + 1 < len(chunks):
                pre_activations(p + 1)

        i0 = jnp.minimum(e, n_e - 1) * n_ig
        sub = lax.broadcasted_iota(jnp.int32, (SUBLANES, tm), 0)
        for ig in range(n_ig):
            i = i0 + ig
            i_grp = i // SUBLANES
            for h in range(P_HEADS):
                s0 = jnp.sum(jnp.where(sub == i - i_grp * SUBLANES, sc_ref[2 * h, i_grp], 0.0), axis=0, keepdims=True)
                w0 = jnp.exp(s0 - st_ref[h, ST_MAX0:ST_MAX0 + 1, :]) * st_ref[h, ST_INVZ:ST_INVZ + 1, :]
                row_scr[h, ig, 0] = jnp.broadcast_to(s0, (SUBLANES, tm))
                row_scr[h, ig, 1] = jnp.broadcast_to(w0, (SUBLANES, tm))

        for k, (c0, cw) in enumerate(chunks):
            at_scr = at_scrs[k]
            for l0 in range(c0, c0 + cw, LANES):
                lanes = slice(l0, l0 + LANES)
                at_lanes = slice(l0 - c0, l0 - c0 + LANES)
                for jq in range(K // PEER_KEY_ROWS):
                    jg = slice(jq * PEER_KEY_GROUPS, (jq + 1) * PEER_KEY_GROUPS)
                    gates = [jnp.zeros((PEER_KEY_GROUPS, SUBLANES, LANES), F32) for _ in range(n_ig)]
                    for h in range(P_HEADS):
                        s1 = sc_ref[2 * h + 1, jg, :, lanes]
                        e1 = e1_scr[h, jg, :, lanes]
                        tau = tau_scr[h, :, lanes]
                        for ig in range(n_ig):
                            total = s1 + row_scr[h, ig, 0, :, lanes]
                            gates[ig] = gates[ig] + jnp.where(total >= tau, e1 * row_scr[h, ig, 1, :, lanes], 0.0)
                    for ig in range(n_ig):
                        rows = slice(ig * K + jq * PEER_KEY_ROWS, ig * K + (jq + 1) * PEER_KEY_ROWS)
                        gate = gates[ig].reshape(PEER_KEY_ROWS, LANES)
                        act_cur[rows, lanes] = (_gelu(at_scr[rows, at_lanes]) * gate).astype(BF16)

    pl.when(e % 2 == 0)(lambda: step(act_a, act_b))
    pl.when(e % 2 == 1)(lambda: step(act_b, act_a))


def _peer_dense(xb, sc, st, ub, vb, tm, te):
    T, D = xb.shape
    E = ub.shape[0]
    n_e = E // te
    once = pl.Buffered(1)
    return pl.pallas_call(
        _peer_dense_kernel,
        grid=(T // tm, n_e + 1),
        in_specs=[
            pl.BlockSpec((tm, D), lambda i, e: (i, 0), pipeline_mode=once),
            pl.BlockSpec((2 * P_HEADS, P_NKEYS // SUBLANES, SUBLANES, tm), lambda i, e: (0, 0, 0, i), pipeline_mode=once),
            pl.BlockSpec((P_HEADS, ST_ROWS, tm), lambda i, e: (0, 0, i)),
            pl.BlockSpec((te, D), lambda i, e: (jnp.minimum(e, n_e - 1), 0)),
            pl.BlockSpec((te, D), lambda i, e: (jnp.maximum(e - 1, 0), 0)),
        ],
        out_specs=pl.BlockSpec((tm, D), lambda i, e: (i, 0)),
        out_shape=jax.ShapeDtypeStruct((T, D), F32),
        scratch_shapes=[
            pltpu.VMEM((P_HEADS, P_NKEYS // SUBLANES, SUBLANES, tm), F32),
            pltpu.VMEM((P_HEADS, SUBLANES, tm), F32),
            pltpu.VMEM((P_HEADS, te // P_NKEYS, 2, SUBLANES, tm), F32),
            pltpu.VMEM((te, tm), BF16),
            pltpu.VMEM((te, tm), BF16),
        ] + [pltpu.VMEM((te, min(PEER_TOKEN_CHUNK, tm - c0)), F32) for c0 in range(0, tm, PEER_TOKEN_CHUNK)],
        compiler_params=_cparams("parallel", "arbitrary"),
        name="peer_dense",
    )(xb, sc.reshape(2 * P_HEADS, P_NKEYS // SUBLANES, SUBLANES, T), st, ub, vb)


def _tile(n, *candidates):
    for c in candidates:
        if n % c == 0:
            return c
    raise ValueError(f"no tile of {candidates} divides {n}")


def kernel(x_prompt, x_sample, state_C, state_n, state_m, cache_kv_w128, cache_kv_w512, cache_kv_w2048,
           w_in, b_in, mlstm_norm_w, w_branch_m, w_branch_a, w_out, ln1_g, ln1_b,
           w_query, sub_keys, expert_u, expert_v, ln2_g, ln2_b):
    bp, S, D = x_prompt.shape
    DB, DS, _ = x_sample.shape
    H = M_HEADS
    dv, dk = D // 8, D // 16
    AW = A_HEADS * A_DH
    caches = (cache_kv_w128, cache_kv_w512, cache_kv_w2048)
    assert bp == 1 and DB * DS == M_CHUNK and S % A_SUPER == 0
    assert all(c.shape[1] == A_STEPS * d for c, d in zip(caches, A_DILATIONS))
    T = S + DB * DS

    x = jnp.concatenate([x_prompt.reshape(S, D), x_sample.reshape(DB * DS, D)], axis=0)
    xb = x.astype(BF16)

    g_lo = 2 * H * dk + 2 * H * dv
    g_hi = g_lo + 2 * H
    w_in_t = w_in.T
    b_main = jnp.concatenate([b_in[:g_lo], b_in[g_hi:]])[None, :]
    w_gate_t = jnp.pad(w_in_t[g_lo:g_hi], ((0, LANES - 2 * H), (0, 0)))
    b_gate = jnp.pad(b_in[g_lo:g_hi], (0, LANES - 2 * H))[None, :]
    tm = _tile(T, 640, 128)
    z = _proj(xb, w_in_t, b_main, tm, 1024, g_lo, g_hi)
    gates = _gates(x, w_gate_t, b_gate, tm)
    gates_t = gates[:, :2 * H].T
    aq_col, ak_col, av_col = g_lo, g_lo + A_GROUPS * AW, g_lo + 2 * A_GROUPS * AW
    gm_col = g_lo + 3 * A_GROUPS * AW
    ga_col = gm_col + D

    nw = mlstm_norm_w[None, :]
    m_out, c_p, n_p, m_p = _mlstm_prompt(z, gates, gates_t, nw, S, T)
    m0_rows = jnp.pad(jnp.repeat(state_m, DS, axis=0), ((0, 0), (0, LANES - H)))
    m_out, c_s, n_s, mt_s = _mlstm_sample(z, gates, gates_t, nw, m0_rows, state_C,
                                          state_n.reshape(DB * H, 1, dk), m_out, S)

    n_slopes = A_GROUPS * A_HEADS
    slopes = (2.0 ** (-8.0 * (jnp.arange(n_slopes, dtype=F32) + 1.0) / n_slopes)).reshape(A_GROUPS, A_HEADS)
    a_out = _attn_prompt(z, slopes, S, T, aq_col // A_DH, ak_col // A_DH, av_col // A_DH)
    bufs = [c.reshape(DB, c.shape[1] * 2 * A_HEADS, A_DH) for c in caches]
    kv_new = []
    for g in range(A_GROUPS):
        kg = z[S:, ak_col + g * AW:ak_col + (g + 1) * AW]
        vg = z[S:, av_col + g * AW:av_col + (g + 1) * AW]
        kv_new.append(jnp.stack([kg, vg], axis=1).reshape(DB, DS * 2 * A_HEADS, A_DH))
    a_out, kv_s = _attn_sample(z, slopes, bufs, kv_new, a_out, S, aq_col // AW, ak_col // AW, av_col // AW, DS)
    kv_s = [k.reshape(c.shape) for k, c in zip(kv_s, caches)]

    merged = _merge(m_out, a_out, w_branch_m.astype(BF16), w_branch_a.astype(BF16), z,
                    gm_col // 1024, ga_col // 1024, tm, 1024)
    x1, x1b = _outproj_ln(merged, w_out.astype(BF16), x, ln1_g[None, :], ln1_b[None, :], tm, 512)
    sk = sub_keys.reshape(2 * P_HEADS, P_NKEYS, sub_keys.shape[-1]).astype(BF16)
    sc, st = _peer_scores(x1b, w_query.astype(BF16), sk, tm)
    y = _peer_dense(x1b, sc, st, expert_u.astype(BF16), expert_v.astype(BF16), tm, 512)
    x2_prompt, x2_sample = _residual_ln(x1, y, ln2_g[None, :], ln2_b[None, :], M_CHUNK, S)

    y_prompt = x2_prompt.reshape(1, S, D)
    y_sample = x2_sample.reshape(DB, DS, D)
    kv_p = []
    for g, cache in enumerate(caches):
        win = cache.shape[1]
        kg = z[S - win:S, ak_col + g * AW:ak_col + (g + 1) * AW]
        vg = z[S - win:S, av_col + g * AW:av_col + (g + 1) * AW]
        kv_p.append(jnp.stack([kg, vg], axis=1).reshape(1, win, 2, A_HEADS, A_DH))
    m_s = mt_s[:, DS - 1::DS, 0].T
    return (y_prompt, y_sample, c_p, n_p.reshape(1, H, dk), m_p[:, 0, 0][None, :],
            kv_p[0], kv_p[1], kv_p[2],
            c_s, n_s.reshape(DB, H, dk), m_s, kv_s[0], kv_s[1], kv_s[2])
```

```python
import functools
import math

import jax
import jax.numpy as jnp
from jax import lax
from jax.experimental import pallas as pl
from jax.experimental.pallas import tpu as pltpu

F32 = jnp.float32
BF16 = jnp.bfloat16

M_HEADS = 8
M_CHUNK = 128
M_HEADS_PER_STEP = 2
A_DILATIONS = (1, 4, 16)
A_GROUPS = len(A_DILATIONS)
A_HEADS = 8
A_DH = 128
A_STEPS = 128
P_HEADS = 8
P_NKEYS = 128
P_TOPK = 16
DEPTH = 1
ALPHA = (2 * DEPTH) ** 0.25
LN_EPS = 1e-5

LANES = 128
SUBLANES = 8
VMEM_LIMIT = 56 * 1024 * 1024
NEG_INF = float("-inf")


def _cparams(*sem):
    return pltpu.CompilerParams(dimension_semantics=sem, vmem_limit_bytes=VMEM_LIMIT)


def _dot(a, b):
    return jnp.dot(a, b, preferred_element_type=F32)


def _dot_nt(a, b):
    return lax.dot_general(a, b, (((1,), (1,)), ((), ())), preferred_element_type=F32)


def _dot_tn(a, b):
    return lax.dot_general(a, b, (((0,), (0,)), ((), ())), preferred_element_type=F32)


def _proj_kernel(x_ref, wt_hbm, b_ref, o_ref, stage, w_scr, sem, *, first_shifted, skip):
    j = pl.program_id(0)
    i = pl.program_id(1)
    tn = stage.shape[0]

    def fetch(jj):
        row0 = pl.multiple_of(jj * tn + jnp.where(jj >= first_shifted, skip, 0), SUBLANES)
        return pltpu.make_async_copy(wt_hbm.at[pl.ds(row0, tn), :], stage, sem)

    @pl.when((i == 0) & (j == 0))
    def _():
        fetch(j).start()

    @pl.when(i == 0)
    def _():
        fetch(j).wait()
        w_scr[...] = stage[...].astype(BF16)

        @pl.when(j + 1 < pl.num_programs(0))
        def _():
            fetch(j + 1).start()

    o_ref[...] = _dot_nt(x_ref[...], w_scr[...]) + b_ref[...]


def _proj(xb, wt, b, tm, tn, skip_lo, skip_hi):
    T, K = xb.shape
    skip = skip_hi - skip_lo
    N = wt.shape[0] - skip
    assert skip_lo % tn == 0 and skip % SUBLANES == 0 and N % tn == 0 and b.shape == (1, N)
    return pl.pallas_call(
        functools.partial(_proj_kernel, first_shifted=skip_lo // tn, skip=skip),
        grid=(N // tn, T // tm),
        in_specs=[
            pl.BlockSpec((tm, K), lambda j, i: (i, 0)),
            pl.BlockSpec(memory_space=pl.ANY),
            pl.BlockSpec((1, tn), lambda j, i: (0, j)),
        ],
        out_specs=pl.BlockSpec((tm, tn), lambda j, i: (i, j)),
        out_shape=jax.ShapeDtypeStruct((T, N), F32),
        scratch_shapes=[pltpu.VMEM((tn, K), F32), pltpu.VMEM((tn, K), BF16), pltpu.SemaphoreType.DMA(())],
        compiler_params=_cparams("arbitrary", "arbitrary"),
        name="in_proj",
    )(xb, wt, b)


def _split_bf16(a):
    hi = a.astype(BF16)
    lo = (a - hi.astype(F32)).astype(BF16)
    return hi, lo


def _gates_kernel(xa_ref, xb_ref, w_ref, b_ref, x_ref, xh_ref, o_ref, *, n_head):
    x = jnp.where(pl.program_id(0) < n_head, xa_ref[...], xb_ref[...])
    x_ref[...] = x
    xh, xl = _split_bf16(x)
    xh_ref[...] = xh
    wh, wl = _split_bf16(w_ref[...])
    o_ref[...] = _dot_nt(xh, wh) + _dot_nt(xh, wl) + _dot_nt(xl, wh) + b_ref[...]


def _gates(x_first, x_second, wgt, bg, tm):
    S, K = x_first.shape
    R = x_second.shape[0]
    assert S % tm == 0 and R % tm == 0
    n_head = S // tm
    T = S + R
    return pl.pallas_call(
        functools.partial(_gates_kernel, n_head=n_head),
        grid=(T // tm,),
        in_specs=[
            pl.BlockSpec((tm, K), lambda i: (jnp.minimum(i, n_head - 1), 0)),
            pl.BlockSpec((tm, K), lambda i: (jnp.maximum(i - n_head, 0), 0)),
            pl.BlockSpec((LANES, K), lambda i: (0, 0)),
            pl.BlockSpec((1, LANES), lambda i: (0, 0)),
        ],
        out_specs=[
            pl.BlockSpec((tm, K), lambda i: (i, 0)),
            pl.BlockSpec((tm, K), lambda i: (i, 0)),
            pl.BlockSpec((tm, LANES), lambda i: (i, 0)),
        ],
        out_shape=[
            jax.ShapeDtypeStruct((T, K), F32),
            jax.ShapeDtypeStruct((T, K), BF16),
            jax.ShapeDtypeStruct((T, LANES), F32),
        ],
        compiler_params=_cparams("parallel"),
        name="gate_proj",
    )(x_first, x_second, wgt, bg)


def _merge_kernel(m_ref, a_ref, wm_ref, wa_ref, gm_ref, ga_ref, o_ref):
    bm = _dot(m_ref[...], wm_ref[...])
    ba = _dot(a_ref[...].astype(BF16), wa_ref[...])
    o_ref[...] = (jax.nn.sigmoid(gm_ref[...]) * bm + jax.nn.sigmoid(ga_ref[...]) * ba).astype(o_ref.dtype)


def _merge(m_out, a_out, wm, wa, z, gm_blk, ga_blk, tm, tn):
    T, Km = m_out.shape
    Ka = a_out.shape[1]
    N = wm.shape[1]
    return pl.pallas_call(
        _merge_kernel,
        grid=(N // tn, T // tm),
        in_specs=[
            pl.BlockSpec((tm, Km), lambda j, i: (i, 0)),
            pl.BlockSpec((tm, Ka), lambda j, i: (i, 0)),
            pl.BlockSpec((Km, tn), lambda j, i: (0, j)),
            pl.BlockSpec((Ka, tn), lambda j, i: (0, j)),
            pl.BlockSpec((tm, tn), lambda j, i: (i, gm_blk + j)),
            pl.BlockSpec((tm, tn), lambda j, i: (i, ga_blk + j)),
        ],
        out_specs=pl.BlockSpec((tm, tn), lambda j, i: (i, j)),
        out_shape=jax.ShapeDtypeStruct((T, N), BF16),
        compiler_params=_cparams("parallel", "parallel"),
        name="branch_merge",
    )(m_out, a_out, wm, wa, z, z)


LN_ROWS = 32


def _layer_norm_rows(v, g, b):
    mu = jnp.mean(v, axis=-1, keepdims=True)
    c = v - mu
    var = jnp.mean(c * c, axis=-1, keepdims=True)
    return c * lax.rsqrt(var + LN_EPS) * g + b


def _outproj_ln_kernel(a_ref, w_ref, x_ref, g_ref, b_ref, o_ref, ob_ref):
    k = pl.program_id(1)

    @pl.when(k == 0)
    def _():
        o_ref[...] = ALPHA * x_ref[...] + _dot(a_ref[...], w_ref[...])

    @pl.when(k > 0)
    def _():
        o_ref[...] += _dot(a_ref[...], w_ref[...])

    @pl.when(k == pl.num_programs(1) - 1)
    def _():
        def norm_rows(r, carry):
            rows = pl.ds(pl.multiple_of(r * LN_ROWS, LN_ROWS), LN_ROWS)
            y = _layer_norm_rows(o_ref[rows, :], g_ref[...], b_ref[...])
            o_ref[rows, :] = y
            ob_ref[rows, :] = y.astype(BF16)
            return carry

        lax.fori_loop(0, o_ref.shape[0] // LN_ROWS, norm_rows, 0)


def _outproj_ln(a, w, x, g, b, tm, tk):
    T, K = a.shape
    D = w.shape[1]
    return pl.pallas_call(
        _outproj_ln_kernel,
        grid=(T // tm, K // tk),
        in_specs=[
            pl.BlockSpec((tm, tk), lambda i, k: (i, k)),
            pl.BlockSpec((tk, D), lambda i, k: (k, 0)),
            pl.BlockSpec((tm, D), lambda i, k: (i, 0), pipeline_mode=pl.Buffered(1)),
            pl.BlockSpec((1, D), lambda i, k: (0, 0)),
            pl.BlockSpec((1, D), lambda i, k: (0, 0)),
        ],
        out_specs=[
            pl.BlockSpec((tm, D), lambda i, k: (i, 0)),
            pl.BlockSpec((tm, D), lambda i, k: (i, 0)),
        ],
        out_shape=[jax.ShapeDtypeStruct((T, D), F32), jax.ShapeDtypeStruct((T, D), BF16)],
        compiler_params=_cparams("parallel", "arbitrary"),
        name="out_proj_ln",
    )(a, w, x, g, b)


def _residual_ln_kernel(x_ref, y_ref, g_ref, b_ref, head_ref, tail_ref, *, n_head):
    i = pl.program_id(0)

    def emit(dst_ref):
        def norm_rows(r, carry):
            rows = pl.ds(pl.multiple_of(r * LN_ROWS, LN_ROWS), LN_ROWS)
            dst_ref[rows, :] = _layer_norm_rows(ALPHA * x_ref[rows, :] + y_ref[rows, :], g_ref[...], b_ref[...])
            return carry

        lax.fori_loop(0, x_ref.shape[0] // LN_ROWS, norm_rows, 0)

    pl.when(i < n_head)(lambda: emit(head_ref))
    pl.when(i >= n_head)(lambda: emit(tail_ref))


def _residual_ln(x, y, g, b, tm, n_first):
    T, D = x.shape
    assert n_first % tm == 0 and (T - n_first) % tm == 0
    n_head = n_first // tm
    return pl.pallas_call(
        functools.partial(_residual_ln_kernel, n_head=n_head),
        grid=(T // tm,),
        in_specs=[
            pl.BlockSpec((tm, D), lambda i: (i, 0)),
            pl.BlockSpec((tm, D), lambda i: (i, 0)),
            pl.BlockSpec((1, D), lambda i: (0, 0)),
            pl.BlockSpec((1, D), lambda i: (0, 0)),
        ],
        out_specs=[
            pl.BlockSpec((tm, D), lambda i: (jnp.minimum(i, n_head - 1), 0)),
            pl.BlockSpec((tm, D), lambda i: (jnp.maximum(i - n_head, 0), 0)),
        ],
        out_shape=[jax.ShapeDtypeStruct((n_first, D), F32), jax.ShapeDtypeStruct((T - n_first, D), F32)],
        compiler_params=_cparams("arbitrary"),
        name="residual_ln",
    )(x, y, g, b)


def _log_sigmoid(x):
    return jnp.minimum(x, 0.0) - jnp.log1p(jnp.exp(-jnp.abs(x)))


def _lane_column(a, col):
    lane = lax.broadcasted_iota(jnp.int32, a.shape, 1)
    return jnp.sum(jnp.where(lane == col, a, 0.0), axis=1, keepdims=True)


def _mlstm_gate_math(ig_col, fg_col, ig_row, fg_row, m0_col, seg_shift):
    L = M_CHUNK
    t = lax.broadcasted_iota(jnp.int32, (L, L), 0)
    s = lax.broadcasted_iota(jnp.int32, (L, L), 1)
    lower, upper = s <= t, t <= s
    if seg_shift is not None:
        same = (t >> seg_shift) == (s >> seg_shift)
        lower, upper = lower & same, upper & same
    lf_col, lf_row = _log_sigmoid(fg_col), _log_sigmoid(fg_row)
    f_col = jnp.sum(jnp.where(lower, lf_row, 0.0), axis=1, keepdims=True)
    f_row = jnp.sum(jnp.where(upper, lf_col, 0.0), axis=0, keepdims=True)
    d = jnp.where(lower, f_col - f_row + ig_row, NEG_INF)
    inter = f_col + m0_col
    m_t = jnp.maximum(inter, jnp.max(d, axis=1, keepdims=True))
    w_intra = jnp.exp(d - m_t)
    w_inter = jnp.exp(inter - m_t)
    return f_col, f_row, m_t, w_intra, w_inter


def _mlstm_head_out(num, den, m_t, nw_row, mo):
    hid = num * (1.0 / jnp.maximum(jnp.abs(den), jnp.exp(-m_t)))
    mu = jnp.mean(hid, axis=-1, keepdims=True)
    c = hid - mu
    var = jnp.mean(c * c, axis=-1, keepdims=True)
    return c * lax.rsqrt(var + LN_EPS) * nw_row * jax.nn.sigmoid(mo)


def _mlstm_prompt_kernel(q_ref, k_ref, v_ref, mo_ref, g_ref, gt_ref, nw_ref, init_ref,
                         mout_ref, c_out_ref, n_out_ref, m_out_ref, c_scr, n_scr, m_scr):
    del init_ref
    hp = pl.program_id(0)
    c = pl.program_id(1)
    L = M_CHUNK
    HP = M_HEADS_PER_STEP
    dk = q_ref.shape[1] // HP
    dv = v_ref.shape[1] // HP

    @pl.when(c == 0)
    def _():
        c_scr[...] = jnp.zeros_like(c_scr)
        n_scr[...] = jnp.zeros_like(n_scr)
        m_scr[...] = jnp.zeros_like(m_scr)

    g = g_ref[...]
    for hh in range(HP):
        h = hp * HP + hh
        kcols = slice(hh * dk, (hh + 1) * dk)
        vcols = slice(hh * dv, (hh + 1) * dv)
        ig_col = _lane_column(g, h)
        fg_col = _lane_column(g, M_HEADS + h)
        ig_row = gt_ref[pl.ds(h, 1), :]
        fg_row = gt_ref[pl.ds(M_HEADS + h, 1), :]
        f_col, _, m_t, w_intra, w_inter = _mlstm_gate_math(ig_col, fg_col, ig_row, fg_row, m_scr[hh], None)

        q = q_ref[:, kcols]
        kf = k_ref[:, kcols] * (1.0 / math.sqrt(dk))
        qb, kb, vb = q.astype(BF16), kf.astype(BF16), v_ref[:, vcols].astype(BF16)
        c_prev = c_scr[hh]
        n_prev = n_scr[hh]
        s = _dot_nt(qb, kb) * w_intra
        num = w_inter * _dot(qb, c_prev.astype(BF16)) + _dot(s.astype(BF16), vb)
        den = w_inter * jnp.sum(q * n_prev, axis=1, keepdims=True) + jnp.sum(s, axis=1, keepdims=True)
        mout_ref[:, vcols] = _mlstm_head_out(num, den, m_t, nw_ref[:, vcols], mo_ref[:, vcols]).astype(mout_ref.dtype)

        m_end = m_t[L - 1:L, :]
        w_state = w_inter[L - 1:L, :]
        w_tok = jnp.exp(f_col[L - 1:L, :] - f_col + ig_col - m_end)
        kw = kf * w_tok
        c_scr[hh] = w_state * c_prev + _dot_tn(kw.astype(BF16), vb)
        n_scr[hh] = w_state * n_prev + jnp.sum(kw, axis=0, keepdims=True)
        m_scr[hh] = m_end

    @pl.when(c == pl.num_programs(1) - 1)
    def _():
        for hh in range(HP):
            c_out_ref[0, hh] = c_scr[hh]
            n_out_ref[hh] = n_scr[hh]
            m_out_ref[hh] = jnp.broadcast_to(m_scr[hh], (SUBLANES, LANES))


def _mlstm_prompt(z, g, gt, nw, s_len, t_all):
    dv = nw.shape[1] // M_HEADS
    dk = dv // 2
    L = M_CHUNK
    H = M_HEADS
    HP = M_HEADS_PER_STEP
    G = H // HP
    return pl.pallas_call(
        _mlstm_prompt_kernel,
        grid=(G, s_len // L),
        in_specs=[
            pl.BlockSpec((L, HP * dk), lambda h, c: (c, h)),
            pl.BlockSpec((L, HP * dk), lambda h, c: (c, G + h)),
            pl.BlockSpec((L, HP * dv), lambda h, c: (c, G + h)),
            pl.BlockSpec((L, HP * dv), lambda h, c: (c, 2 * G + h)),
            pl.BlockSpec((L, LANES), lambda h, c: (c, 0)),
            pl.BlockSpec((2 * H, L), lambda h, c: (0, c)),
            pl.BlockSpec((1, HP * dv), lambda h, c: (0, h)),
            pl.BlockSpec(memory_space=pl.ANY),
        ],
        out_specs=[
            pl.BlockSpec((L, HP * dv), lambda h, c: (c, h)),
            pl.BlockSpec((1, HP, dk, dv), lambda h, c: (0, h, 0, 0)),
            pl.BlockSpec((HP, 1, dk), lambda h, c: (h, 0, 0)),
            pl.BlockSpec((HP, SUBLANES, LANES), lambda h, c: (h, 0, 0)),
        ],
        out_shape=[
            jax.ShapeDtypeStruct((t_all, H * dv), BF16),
            jax.ShapeDtypeStruct((1, H, dk, dv), F32),
            jax.ShapeDtypeStruct((H, 1, dk), F32),
            jax.ShapeDtypeStruct((H, SUBLANES, LANES), F32),
        ],
        scratch_shapes=[pltpu.VMEM((HP, dk, dv), F32), pltpu.VMEM((HP, 1, dk), F32), pltpu.VMEM((HP, 1, 1), F32)],
        input_output_aliases={7: 0},
        compiler_params=_cparams("parallel", "arbitrary"),
        name="mlstm_prompt",
    )(z, z, z, z, g, gt, nw, jnp.zeros((t_all, H * dv), BF16))


def _mlstm_sample_kernel(q_ref, k_ref, v_ref, mo_ref, g_ref, gt_ref, nw_ref, m0_ref, c0_ref, n0_ref, alias_ref,
                         mout_ref, c_out_ref, n_out_ref, mt_out_ref,
                         num_scr, den_scr, kw_scr, vb_scr, qb_scr, wi_scr, mt_scr, *, seg_shift):
    del alias_ref
    h = pl.program_id(0)
    b = pl.program_id(1)
    L = M_CHUNK
    seg = 1 << seg_shift

    @pl.when(b == 0)
    def _():
        g = g_ref[...]
        ig_col = _lane_column(g, h)
        fg_col = _lane_column(g, M_HEADS + h)
        ig_row = gt_ref[pl.ds(h, 1), :]
        fg_row = gt_ref[pl.ds(M_HEADS + h, 1), :]
        m0_col = _lane_column(m0_ref[...], h)
        f_col, f_row, m_t, w_intra, w_inter = _mlstm_gate_math(ig_col, fg_col, ig_row, fg_row, m0_col, seg_shift)
        q = q_ref[...]
        kf = k_ref[...] * (1.0 / math.sqrt(q.shape[1]))
        qb, kb, vb = q.astype(BF16), kf.astype(BF16), v_ref[...].astype(BF16)
        s = _dot_nt(qb, kb) * w_intra
        num_scr[...] = _dot(s.astype(BF16), vb)
        den_scr[...] = jnp.sum(s, axis=1, keepdims=True)
        t = lax.broadcasted_iota(jnp.int32, (L, L), 0)
        sidx = lax.broadcasted_iota(jnp.int32, (L, L), 1)
        is_last = sidx == (t | (seg - 1))
        m_row = jnp.sum(jnp.where(t == sidx, m_t, 0.0), axis=0, keepdims=True)
        f_last = jnp.sum(jnp.where(is_last, f_row, 0.0), axis=1, keepdims=True)
        m_end = jnp.sum(jnp.where(is_last, m_row, 0.0), axis=1, keepdims=True)
        w_tok = jnp.exp(f_last - f_col + ig_col - m_end)
        kw_scr[...] = kf * w_tok
        vb_scr[...] = vb
        qb_scr[...] = qb
        wi_scr[...] = w_inter
        mt_scr[...] = m_t

    c_b = c0_ref[0, 0]
    n_b = n0_ref[0]
    row = lax.broadcasted_iota(jnp.int32, (L, 1), 0)
    in_b = (row >> seg_shift) == b
    wi = wi_scr[...]
    qc = _dot(qb_scr[...], c_b.astype(BF16))
    num_scr[...] += jnp.where(in_b, wi * qc, 0.0)
    qn = jnp.sum(q_ref[...] * n_b, axis=1, keepdims=True)
    den_scr[...] += jnp.where(in_b, wi * qn, 0.0)
    w_state = jnp.sum(jnp.where(row == seg * b + (seg - 1), wi, 0.0), axis=0, keepdims=True)
    kw_b = jnp.where(in_b, kw_scr[...], 0.0)
    c_out_ref[0, 0] = w_state * c_b + _dot_tn(kw_b.astype(BF16), vb_scr[...])
    n_out_ref[0] = w_state * n_b + jnp.sum(kw_b, axis=0, keepdims=True)

    @pl.when(b == pl.num_programs(1) - 1)
    def _():
        m_t = mt_scr[...]
        out = _mlstm_head_out(num_scr[...], den_scr[...], m_t, nw_ref[...], mo_ref[...])
        mout_ref[...] = out.astype(mout_ref.dtype)
        mt_out_ref[0] = jnp.broadcast_to(m_t, (L, LANES))


def _mlstm_sample(z, g, gt, nw, m0_rows, c0, n0, m_out, row0):
    dv = nw.shape[1] // M_HEADS
    dk = dv // 2
    L = M_CHUNK
    H = M_HEADS
    B = c0.shape[0]
    seg = L // B
    rb = row0 // L
    kern = functools.partial(_mlstm_sample_kernel, seg_shift=seg.bit_length() - 1)
    return pl.pallas_call(
        kern,
        grid=(H, B),
        in_specs=[
            pl.BlockSpec((L, dk), lambda h, b: (rb, h)),
            pl.BlockSpec((L, dk), lambda h, b: (rb, H + h)),
            pl.BlockSpec((L, dv), lambda h, b: (rb, H + h)),
            pl.BlockSpec((L, dv), lambda h, b: (rb, 2 * H + h)),
            pl.BlockSpec((L, LANES), lambda h, b: (rb, 0)),
            pl.BlockSpec((2 * H, L), lambda h, b: (0, rb)),
            pl.BlockSpec((1, dv), lambda h, b: (0, h)),
            pl.BlockSpec((L, LANES), lambda h, b: (0, 0)),
            pl.BlockSpec((1, 1, dk, dv), lambda h, b: (b, h, 0, 0)),
            pl.BlockSpec((1, 1, dk), lambda h, b: (b * H + h, 0, 0)),
            pl.BlockSpec(memory_space=pl.ANY),
        ],
        out_specs=[
            pl.BlockSpec((L, dv), lambda h, b: (rb, h)),
            pl.BlockSpec((1, 1, dk, dv), lambda h, b: (b, h, 0, 0)),
            pl.BlockSpec((1, 1, dk), lambda h, b: (b * H + h, 0, 0)),
            pl.BlockSpec((1, L, LANES), lambda h, b: (h, 0, 0)),
        ],
        out_shape=[
            jax.ShapeDtypeStruct(m_out.shape, m_out.dtype),
            jax.ShapeDtypeStruct(c0.shape, F32),
            jax.ShapeDtypeStruct(n0.shape, F32),
            jax.ShapeDtypeStruct((H, L, LANES), F32),
        ],
        scratch_shapes=[
            pltpu.VMEM((L, dv), F32), pltpu.VMEM((L, 1), F32), pltpu.VMEM((L, dk), F32), pltpu.VMEM((L, dv), BF16),
            pltpu.VMEM((L, dk), BF16), pltpu.VMEM((L, 1), F32), pltpu.VMEM((L, 1), F32),
        ],
        input_output_aliases={10: 0},
        compiler_params=_cparams("parallel", "arbitrary"),
        name="mlstm_sample",
    )(z, z, z, z, g, gt, nw, m0_rows, c0, n0, m_out)


A_SUPER = A_STEPS * max(A_DILATIONS)
ATT_SCALE = 1.0 / math.sqrt(A_DH)


def _rows(start, size, stride):
    return pl.ds(start, size) if stride == 1 else pl.ds(start, size, stride=stride)


def _group_softmax_merge(lses, outs):
    mx = functools.reduce(jnp.maximum, lses)
    ws = [jnp.exp(l - mx) for l in lses]
    tot = functools.reduce(lambda a, b: a + b, ws)
    acc = functools.reduce(lambda a, b: a + b, [w * o for w, o in zip(ws, outs)])
    return acc / tot


def _attn_prompt_kernel(sl_ref, *refs):
    in_refs = refs[:5 * A_GROUPS]
    a_ref, o_scr, l_scr = refs[5 * A_GROUPS + 1:]
    h = pl.program_id(0)
    n = pl.program_id(1)
    B = A_STEPS
    qi = lax.broadcasted_iota(jnp.int32, (B, B), 0)
    ki = lax.broadcasted_iota(jnp.int32, (B, B), 1)
    valid_prev, valid_cur = ki >= qi, ki <= qi
    step_prev = (qi - ki + B).astype(F32)
    step_cur = (qi - ki).astype(F32)

    ones = jnp.ones((2 * B, LANES), BF16)
    for g, dil in enumerate(A_DILATIONS):
        q_ref, kc_ref, kp_ref, vc_ref, vp_ref = in_refs[5 * g:5 * g + 5]
        slope = sl_ref[g, h] * float(dil)
        bias_cur = jnp.where(valid_cur, -slope * step_cur, NEG_INF)
        bias_prev = jnp.where(valid_prev, -slope * step_prev, NEG_INF)
        bias = jnp.concatenate([bias_prev, bias_cur], axis=1)
        bias_first = jnp.concatenate([jnp.where(n > 0, bias_prev, NEG_INF), bias_cur], axis=1)
        nblk = A_SUPER // (B * dil)
        for r in range(dil):
            for blk in range(nblk):
                start = r + dil * B * blk
                rows = _rows(start, B, dil)
                qb = q_ref[rows, :].astype(BF16)
                if blk > 0:
                    prow = _rows(start - dil * B, B, dil)
                    kp, vp, bs = kc_ref[prow, :], vc_ref[prow, :], bias
                else:
                    prow = _rows(r + dil * B * (nblk - 1), B, dil)
                    kp, vp, bs = kp_ref[prow, :], vp_ref[prow, :], bias_first
                kcat = jnp.concatenate([kp, kc_ref[rows, :]], axis=0).astype(BF16)
                vcat = jnp.concatenate([vp, vc_ref[rows, :]], axis=0).astype(BF16)
                s = _dot_nt(qb, kcat) * ATT_SCALE + bs
                mx = jnp.max(s, axis=1, keepdims=True)
                e = jnp.exp(s - mx).astype(BF16)
                pv = _dot(e, jnp.concatenate([vcat, ones], axis=1))
                den = pv[:, A_DH:]
                o_scr[g, rows, :] = pv[:, :A_DH] * (1.0 / den)
                l_scr[g, rows, :] = mx + jnp.log(den)

    def merge(i, carry):
        rows = pl.ds(pl.multiple_of(i * B, B), B)
        lses = [l_scr[g, rows, :] for g in range(A_GROUPS)]
        outs = [o_scr[g, rows, :] for g in range(A_GROUPS)]
        a_ref[rows, :] = _group_softmax_merge(lses, outs)
        return carry

    lax.fori_loop(0, A_SUPER // B, merge, 0)


def _attn_prompt(z, slopes, s_len, t_all, aq_blk, ak_blk, av_blk):
    R = A_SUPER
    in_specs = [pl.BlockSpec(memory_space=pltpu.SMEM)]
    args = [slopes]
    for g in range(A_GROUPS):
        off = g * A_HEADS
        cur = lambda h, n, base: (n, base + h)
        prev = lambda h, n, base: (jnp.maximum(n - 1, 0), base + h)
        for base, fn in ((aq_blk + off, cur), (ak_blk + off, cur), (ak_blk + off, prev), (av_blk + off, cur), (av_blk + off, prev)):
            in_specs.append(pl.BlockSpec((R, A_DH), functools.partial(fn, base=base)))
            args.append(z)
    in_specs.append(pl.BlockSpec(memory_space=pl.ANY))
    args.append(jnp.zeros((t_all, A_HEADS * A_DH), F32))
    return pl.pallas_call(
        _attn_prompt_kernel,
        grid=(A_HEADS, s_len // R),
        in_specs=in_specs,
        out_specs=pl.BlockSpec((R, A_DH), lambda h, n: (n, h)),
        out_shape=jax.ShapeDtypeStruct((t_all, A_HEADS * A_DH), F32),
        scratch_shapes=[pltpu.VMEM((A_GROUPS, R, A_DH), F32), pltpu.VMEM((A_GROUPS, R, LANES), F32)],
        input_output_aliases={len(args) - 1: 0},
        compiler_params=_cparams("parallel", "arbitrary"),
        name="attn_prompt",
    )(*args)


def _attn_sample_kernel(sl_ref, *refs, seq):
    G = A_GROUPS
    z_refs = refs[:3 * G]
    buf_refs = refs[3 * G:4 * G]
    new_refs = refs[4 * G:5 * G]
    a_ref = refs[5 * G + 1]
    cache_out_refs = refs[5 * G + 2:6 * G + 2]
    sems = refs[6 * G + 2]
    b = pl.program_id(0)
    half = b % (SUBLANES // seq)
    PADK = LANES

    rows_per_pos = 2 * A_HEADS
    copies = []
    for g in range(G):
        kept = buf_refs[g].shape[1] - seq * rows_per_pos
        copies.append(pltpu.make_async_copy(buf_refs[g].at[0, pl.ds(seq * rows_per_pos, kept), :],
                                            cache_out_refs[g].at[b, pl.ds(0, kept), :], sems.at[2 * g]))
        copies.append(pltpu.make_async_copy(new_refs[g].at[0],
                                            cache_out_refs[g].at[b, pl.ds(kept, seq * rows_per_pos), :],
                                            sems.at[2 * g + 1]))
    for cp in copies:
        cp.start()

    for h in range(A_HEADS):
        cols = slice(h * A_DH, (h + 1) * A_DH)
        lses, outs = [], []
        for g, dil in enumerate(A_DILATIONS):
            q_ref, kn_ref, vn_ref = z_refs[3 * g:3 * g + 3]
            buf_ref = buf_refs[g]
            lb = A_STEPS * dil
            slope = sl_ref[g, h]
            kbuf = buf_ref[0, pl.ds(h, lb, stride=2 * A_HEADS), :]
            vbuf = buf_ref[0, pl.ds(A_HEADS + h, lb, stride=2 * A_HEADS), :]
            pad = jnp.zeros((PADK - SUBLANES, A_DH), F32)
            kx = jnp.concatenate([kbuf, kn_ref[:, cols], pad], axis=0).astype(BF16)
            vx = jnp.concatenate([vbuf, vn_ref[:, cols], pad], axis=0).astype(BF16)
            s = _dot_nt(q_ref[:, cols].astype(BF16), kx) * ATT_SCALE
            col = lax.broadcasted_iota(jnp.int32, s.shape, 1)
            t = lax.broadcasted_iota(jnp.int32, s.shape, 0) - seq * half
            t_new = col - lb - seq * half
            in_buf = col < lb
            pos = jnp.where(in_buf, col, lb + t_new)
            delta = lb + t - pos
            valid = (in_buf | ((t_new >= 0) & (t_new < seq))) & (delta >= 0) & (delta <= A_STEPS * dil) \
                & ((delta & (dil - 1)) == 0)
            s = jnp.where(valid, s - slope * delta.astype(F32), NEG_INF)
            mx = jnp.max(s, axis=1, keepdims=True)
            e = jnp.exp(s - mx)
            den = jnp.sum(e, axis=1, keepdims=True)
            outs.append(_dot(e.astype(BF16), vx) * (1.0 / den))
            lses.append(mx + jnp.log(den))
        merged = _group_softmax_merge(lses, outs)
        for hv in range(SUBLANES // seq):
            @pl.when(half == hv)
            def _(merged=merged, hv=hv):
                a_ref[hv * seq:(hv + 1) * seq, cols] = merged[hv * seq:(hv + 1) * seq, :]

    for cp in copies:
        cp.wait()


def _attn_sample(z, slopes, bufs, kv_new, a_out, row0, aq_blk, ak_blk, av_blk, seq):
    B = bufs[0].shape[0]
    W = A_HEADS * A_DH
    per_blk = SUBLANES // seq
    rb0 = row0 // SUBLANES
    in_specs = [pl.BlockSpec(memory_space=pltpu.SMEM)]
    args = [slopes]
    for g in range(A_GROUPS):
        for base in (aq_blk + g, ak_blk + g, av_blk + g):
            in_specs.append(pl.BlockSpec((SUBLANES, W), functools.partial(lambda b, base: (rb0 + b // per_blk, base), base=base)))
            args.append(z)
    for arrs in (bufs, kv_new):
        for g in range(A_GROUPS):
            in_specs.append(pl.BlockSpec((1,) + arrs[g].shape[1:], lambda b: (b, 0, 0)))
            args.append(arrs[g])
    in_specs.append(pl.BlockSpec(memory_space=pl.ANY))
    args.append(a_out)
    outs = pl.pallas_call(
        functools.partial(_attn_sample_kernel, seq=seq),
        grid=(B,),
        in_specs=in_specs,
        out_specs=[pl.BlockSpec((SUBLANES, W), lambda b: (rb0 + b // per_blk, 0))]
        + [pl.BlockSpec(memory_space=pl.ANY)] * A_GROUPS,
        out_shape=[jax.ShapeDtypeStruct(a_out.shape, a_out.dtype)]
        + [jax.ShapeDtypeStruct(c.shape, c.dtype) for c in bufs],
        scratch_shapes=[pltpu.SemaphoreType.DMA((2 * A_GROUPS,))],
        input_output_aliases={len(args) - 1: 0},
        compiler_params=_cparams("arbitrary"),
        name="attn_sample",
    )(*args)
    return outs[0], outs[1:]


ST_TAU, ST_MAX0, ST_MAX1, ST_INVZ = 0, 1, 2, 3
ST_ROWS = SUBLANES
PEER_TOKEN_CHUNK = 256
PEER_KEY_ROWS = 32
PEER_KEY_GROUPS = PEER_KEY_ROWS // SUBLANES


def _take_max(sc, out, k):
    rows = lax.broadcasted_iota(jnp.int32, sc.shape, 0)
    out_rows = lax.broadcasted_iota(jnp.int32, out.shape, 0)
    m = jnp.max(sc, axis=0, keepdims=True)
    first = jnp.min(jnp.where(sc == m, rows, sc.shape[0]), axis=0, keepdims=True)
    return jnp.where(rows == first, NEG_INF, sc), jnp.where(out_rows == k, m, out)


def _pair_candidates(top0, top1):
    assert P_TOPK == 16 and SUBLANES == 8
    lo0, hi0, lo1, hi1 = top0[0:8], top0[8:16], top1[0:8], top1[8:16]
    skip = lax.broadcasted_iota(jnp.int32, lo0.shape, 0) < 4
    blocks = [top0[0:1] + lo1, top0[0:1] + hi1]
    blocks += [top0[i:i + 1] + lo1 for i in (1, 2, 3)]
    blocks += [jnp.where(skip, NEG_INF, lo0 + top1[0:1]), hi0 + top1[0:1]]
    blocks += [jnp.where(skip, NEG_INF, lo0 + top1[j:j + 1]) for j in (1, 2, 3)]
    return jnp.concatenate(blocks, axis=0)


N_PAIR_ROWS = 10 * SUBLANES


def _peer_scores_kernel(x_ref, wq_ref, sk_ref, sc_ref, st_ref, q_scr):
    tm = x_ref.shape[0]
    dkh = sk_ref.shape[2]
    q_scr[...] = _dot(x_ref[...], wq_ref[...]).astype(BF16)
    for hp in range(2 * P_HEADS):
        sc_ref[hp] = _dot_nt(sk_ref[hp], q_scr[:, hp * dkh:(hp + 1) * dkh])

    def per_chunk(ci, carry):
        lanes = pl.ds(pl.multiple_of(ci * LANES, LANES), LANES)
        empty = jnp.zeros((P_TOPK, LANES), F32)

        def per_head(h, carry):
            cand, max0, max1 = carry
            hh = jnp.minimum(h, P_HEADS - 1)
            sc0 = sc_ref[2 * hh, :, lanes]
            sc1 = sc_ref[2 * hh + 1, :, lanes]
            top0, top1, best = empty, empty, empty
            for k in range(P_TOPK):
                sc0, top0 = _take_max(sc0, top0, k)
                sc1, top1 = _take_max(sc1, top1, k)
                cand, best = _take_max(cand, best, k)
            z = jnp.sum(jnp.exp(best - best[0:1, :]), axis=0, keepdims=True)
            st_ref[jnp.maximum(h - 1, 0), :, lanes] = jnp.concatenate(
                [best[P_TOPK - 1:P_TOPK, :], max0, max1, 1.0 / z, jnp.zeros((ST_ROWS - 4, LANES), F32)], axis=0)
            return _pair_candidates(top0, top1), top0[0:1, :], top1[0:1, :]

        row = jnp.zeros((1, LANES), F32)
        lax.fori_loop(0, P_HEADS + 1, per_head, (jnp.zeros((N_PAIR_ROWS, LANES), F32), row, row))
        return carry

    lax.fori_loop(0, tm // LANES, per_chunk, 0)


def _peer_scores(xb, wq, sk, tm):
    T, D = xb.shape
    nq = wq.shape[1]
    return pl.pallas_call(
        _peer_scores_kernel,
        grid=(T // tm,),
        in_specs=[
            pl.BlockSpec((tm, D), lambda i: (i, 0)),
            pl.BlockSpec((D, nq), lambda i: (0, 0)),
            pl.BlockSpec(sk.shape, lambda i: (0, 0, 0)),
        ],
        out_specs=[
            pl.BlockSpec((2 * P_HEADS, P_NKEYS, tm), lambda i: (0, 0, i)),
            pl.BlockSpec((P_HEADS, ST_ROWS, tm), lambda i: (0, 0, i)),
        ],
        out_shape=[
            jax.ShapeDtypeStruct((2 * P_HEADS, P_NKEYS, T), F32),
            jax.ShapeDtypeStruct((P_HEADS, ST_ROWS, T), F32),
        ],
        scratch_shapes=[pltpu.VMEM((tm, nq), BF16)],
        compiler_params=_cparams("parallel"),
        name="peer_scores",
    )(xb, wq, sk)


def _gelu(a):
    return 0.5 * a * (1.0 + lax.erf(a * (1.0 / math.sqrt(2.0))))


def _peer_dense_kernel(x_ref, sc_ref, st_ref, u_ref, v_ref, y_ref,
                       e1_scr, tau_scr, row_scr, act_a, act_b, *at_scrs, n_e):
    e = pl.program_id(1)
    tm = x_ref.shape[0]
    te = u_ref.shape[0]
    K = P_NKEYS
    n_ig = te // K

    @pl.when(e == 0)
    def _():
        y_ref[...] = jnp.zeros_like(y_ref)
        act_b[...] = jnp.zeros_like(act_b)
        for h in range(P_HEADS):
            e1_scr[h] = jnp.exp(sc_ref[2 * h + 1] - st_ref[h, ST_MAX1:ST_MAX1 + 1, :])
            tau_scr[h] = jnp.broadcast_to(st_ref[h, ST_TAU:ST_TAU + 1, :], (SUBLANES, tm))

    chunks = [(c0, min(PEER_TOKEN_CHUNK, tm - c0)) for c0 in range(0, tm, PEER_TOKEN_CHUNK)]
    assert len(chunks) == len(at_scrs)
    d_model = v_ref.shape[1]
    n_parts = max(len(chunks) - 1, 1)
    part = d_model // n_parts

    def step(act_cur, act_prev):
        def pre_activations(k):
            c0, cw = chunks[k]
            at_scrs[k][...] = _dot_nt(u_ref[...], x_ref[c0:c0 + cw, :])

        def second_matmul(p):
            cols = slice(p * part, (p + 1) * part)
            y_ref[:, cols] += _dot_tn(act_prev[...], v_ref[:, cols])

        pre_activations(0)
        for p in range(n_parts):
            second_matmul(p)
            if p + 1 < len(chunks):
                pre_activations(p + 1)

        i0 = e * n_ig
        sub = lax.broadcasted_iota(jnp.int32, (SUBLANES, tm), 0)
        for ig in range(n_ig):
            i = i0 + ig
            i_grp = i // SUBLANES
            for h in range(P_HEADS):
                s0 = jnp.sum(jnp.where(sub == i - i_grp * SUBLANES, sc_ref[2 * h, i_grp], 0.0), axis=0, keepdims=True)
                w0 = jnp.exp(s0 - st_ref[h, ST_MAX0:ST_MAX0 + 1, :]) * st_ref[h, ST_INVZ:ST_INVZ + 1, :]
                row_scr[h, ig, 0] = jnp.broadcast_to(s0, (SUBLANES, tm))
                row_scr[h, ig, 1] = jnp.broadcast_to(w0, (SUBLANES, tm))

        for k, (c0, cw) in enumerate(chunks):
            at_scr = at_scrs[k]
            for l0 in range(c0, c0 + cw, LANES):
                lanes = slice(l0, l0 + LANES)
                at_lanes = slice(l0 - c0, l0 - c0 + LANES)
                for jq in range(K // PEER_KEY_ROWS):
                    jg = slice(jq * PEER_KEY_GROUPS, (jq + 1) * PEER_KEY_GROUPS)
                    gates = [jnp.zeros((PEER_KEY_GROUPS, SUBLANES, LANES), F32) for _ in range(n_ig)]
                    for h in range(P_HEADS):
                        s1 = sc_ref[2 * h + 1, jg, :, lanes]
                        e1 = e1_scr[h, jg, :, lanes]
                        tau = tau_scr[h, :, lanes]
                        for ig in range(n_ig):
                            total = s1 + row_scr[h, ig, 0, :, lanes]
                            gates[ig] = gates[ig] + jnp.where(total >= tau, e1 * row_scr[h, ig, 1, :, lanes], 0.0)
                    for ig in range(n_ig):
                        rows = slice(ig * K + jq * PEER_KEY_ROWS, ig * K + (jq + 1) * PEER_KEY_ROWS)
                        gate = gates[ig].reshape(PEER_KEY_ROWS, LANES)
                        act_cur[rows, lanes] = (_gelu(at_scr[rows, at_lanes]) * gate).astype(BF16)

    pl.when((e < n_e) & (e % 2 == 0))(lambda: step(act_a, act_b))
    pl.when((e < n_e) & (e % 2 == 1))(lambda: step(act_b, act_a))
    last_written = act_a if (n_e - 1) % 2 == 0 else act_b

    @pl.when(e == n_e)
    def _():
        y_ref[...] += _dot_tn(last_written[...], v_ref[...])


def _peer_dense(xb, sc, st, ub, vb, tm, te):
    T, D = xb.shape
    E = ub.shape[0]
    n_e = E // te
    once = pl.Buffered(1)
    return pl.pallas_call(
        functools.partial(_peer_dense_kernel, n_e=n_e),
        grid=(T // tm, n_e + 1),
        in_specs=[
            pl.BlockSpec((tm, D), lambda i, e: (i, 0), pipeline_mode=once),
            pl.BlockSpec((2 * P_HEADS, P_NKEYS // SUBLANES, SUBLANES, tm), lambda i, e: (0, 0, 0, i), pipeline_mode=once),
            pl.BlockSpec((P_HEADS, ST_ROWS, tm), lambda i, e: (0, 0, i)),
            pl.BlockSpec((te, D), lambda i, e: (jnp.minimum(e, n_e - 1), 0)),
            pl.BlockSpec((te, D), lambda i, e: (jnp.maximum(e - 1, 0), 0)),
        ],
        out_specs=pl.BlockSpec((tm, D), lambda i, e: (i, 0)),
        out_shape=jax.ShapeDtypeStruct((T, D), F32),
        scratch_shapes=[
            pltpu.VMEM((P_HEADS, P_NKEYS // SUBLANES, SUBLANES, tm), F32),
            pltpu.VMEM((P_HEADS, SUBLANES, tm), F32),
            pltpu.VMEM((P_HEADS, te // P_NKEYS, 2, SUBLANES, tm), F32),
            pltpu.VMEM((te, tm), BF16),
            pltpu.VMEM((te, tm), BF16),
        ] + [pltpu.VMEM((te, min(PEER_TOKEN_CHUNK, tm - c0)), F32) for c0 in range(0, tm, PEER_TOKEN_CHUNK)],
        compiler_params=_cparams("parallel", "arbitrary"),
        name="peer_dense",
    )(xb, sc.reshape(2 * P_HEADS, P_NKEYS // SUBLANES, SUBLANES, T), st, ub, vb)


def _tile(n, *candidates):
    for c in candidates:
        if n % c == 0:
            return c
    raise ValueError(f"no tile of {candidates} divides {n}")


def kernel(x_prompt, x_sample, state_C, state_n, state_m, cache_kv_w128, cache_kv_w512, cache_kv_w2048,
           w_in, b_in, mlstm_norm_w, w_branch_m, w_branch_a, w_out, ln1_g, ln1_b,
           w_query, sub_keys, expert_u, expert_v, ln2_g, ln2_b):
    bp, S, D = x_prompt.shape
    DB, DS, _ = x_sample.shape
    H = M_HEADS
    dv, dk = D // 8, D // 16
    AW = A_HEADS * A_DH
    caches = (cache_kv_w128, cache_kv_w512, cache_kv_w2048)
    assert bp == 1 and DB * DS == M_CHUNK and S % A_SUPER == 0
    assert all(c.shape[1] == A_STEPS * d for c, d in zip(caches, A_DILATIONS))
    T = S + DB * DS

    g_lo = 2 * H * dk + 2 * H * dv
    g_hi = g_lo + 2 * H
    w_in_t = w_in.T
    b_main = jnp.concatenate([b_in[:g_lo], b_in[g_hi:]])[None, :]
    w_gate_t = jnp.pad(w_in_t[g_lo:g_hi], ((0, LANES - 2 * H), (0, 0)))
    b_gate = jnp.pad(b_in[g_lo:g_hi], (0, LANES - 2 * H))[None, :]
    tm = _tile(T, 640, 128)
    x, xb, gates = _gates(x_prompt.reshape(S, D), x_sample.reshape(DB * DS, D), w_gate_t, b_gate, M_CHUNK)
    z = _proj(xb, w_in_t, b_main, tm, 1024, g_lo, g_hi)
    gates_t = gates[:, :2 * H].T
    aq_col, ak_col, av_col = g_lo, g_lo + A_GROUPS * AW, g_lo + 2 * A_GROUPS * AW
    gm_col = g_lo + 3 * A_GROUPS * AW
    ga_col = gm_col + D

    nw = mlstm_norm_w[None, :]
    m_out, c_p, n_p, m_p = _mlstm_prompt(z, gates, gates_t, nw, S, T)
    m0_rows = jnp.pad(jnp.repeat(state_m, DS, axis=0), ((0, 0), (0, LANES - H)))
    m_out, c_s, n_s, mt_s = _mlstm_sample(z, gates, gates_t, nw, m0_rows, state_C,
                                          state_n.reshape(DB * H, 1, dk), m_out, S)

    n_slopes = A_GROUPS * A_HEADS
    slopes = (2.0 ** (-8.0 * (jnp.arange(n_slopes, dtype=F32) + 1.0) / n_slopes)).reshape(A_GROUPS, A_HEADS)
    a_out = _attn_prompt(z, slopes, S, T, aq_col // A_DH, ak_col // A_DH, av_col // A_DH)
    bufs = [c.reshape(DB, c.shape[1] * 2 * A_HEADS, A_DH) for c in caches]
    kv_new = []
    for g in range(A_GROUPS):
        kg = z[S:, ak_col + g * AW:ak_col + (g + 1) * AW]
        vg = z[S:, av_col + g * AW:av_col + (g + 1) * AW]
        kv_new.append(jnp.stack([kg, vg], axis=1).reshape(DB, DS * 2 * A_HEADS, A_DH))
    a_out, kv_s = _attn_sample(z, slopes, bufs, kv_new, a_out, S, aq_col // AW, ak_col // AW, av_col // AW, DS)
    kv_s = [k.reshape(c.shape) for k, c in zip(kv_s, caches)]

    merged = _merge(m_out, a_out, w_branch_m.astype(BF16), w_branch_a.astype(BF16), z,
                    gm_col // 1024, ga_col // 1024, tm, 1024)
    x1, x1b = _outproj_ln(merged, w_out.astype(BF16), x, ln1_g[None, :], ln1_b[None, :], tm, 512)
    sk = sub_keys.reshape(2 * P_HEADS, P_NKEYS, sub_keys.shape[-1]).astype(BF16)
    sc, st = _peer_scores(x1b, w_query.astype(BF16), sk, tm)
    y = _peer_dense(x1b, sc, st, expert_u.astype(BF16), expert_v.astype(BF16), tm, 512)
    x2_prompt, x2_sample = _residual_ln(x1, y, ln2_g[None, :], ln2_b[None, :], M_CHUNK, S)

    y_prompt = x2_prompt.reshape(1, S, D)
    y_sample = x2_sample.reshape(DB, DS, D)
    kv_p = []
    for g, cache in enumerate(caches):
        win = cache.shape[1]
        kg = z[S - win:S, ak_col + g * AW:ak_col + (g + 1) * AW]
        vg = z[S - win:S, av_col + g * AW:av_col + (g + 1) * AW]
        kv_p.append(jnp.stack([kg, vg], axis=1).reshape(1, win, 2, A_HEADS, A_DH))
    m_s = mt_s[:, DS - 1::DS, 0].T
    return (y_prompt, y_sample, c_p, n_p.reshape(1, H, dk), m_p[:, 0, 0][None, :],
            kv_p[0], kv_p[1], kv_p[2],
            c_s, n_s.reshape(DB, H, dk), m_s, kv_s[0], kv_s[1], kv_s[2])
```

```python
import functools
import math

import jax
import jax.numpy as jnp
from jax import lax
from jax.experimental import pallas as pl
from jax.experimental.pallas import tpu as pltpu

F32 = jnp.float32
BF16 = jnp.bfloat16

M_HEADS = 8
M_CHUNK = 128
M_HEADS_PER_STEP = 2
A_DILATIONS = (1, 4, 16)
A_GROUPS = len(A_DILATIONS)
A_HEADS = 8
A_DH = 128
A_STEPS = 128
P_HEADS = 8
P_NKEYS = 128
P_TOPK = 16
DEPTH = 1
ALPHA = (2 * DEPTH) ** 0.25
LN_EPS = 1e-5

LANES = 128
SUBLANES = 8
VMEM_LIMIT = 58 * 1024 * 1024
NEG_INF = float("-inf")


def _cparams(*sem):
    return pltpu.CompilerParams(dimension_semantics=sem, vmem_limit_bytes=VMEM_LIMIT)


def _dot(a, b):
    return jnp.dot(a, b, preferred_element_type=F32)


def _dot_nt(a, b):
    return lax.dot_general(a, b, (((1,), (1,)), ((), ())), preferred_element_type=F32)


def _dot_tn(a, b):
    return lax.dot_general(a, b, (((0,), (0,)), ((), ())), preferred_element_type=F32)


def _proj_kernel(x_ref, wt_hbm, b_ref, o_ref, stage, w_scr, sem, *, first_shifted, skip):
    j = pl.program_id(0)
    i = pl.program_id(1)
    tn = stage.shape[0]

    def fetch(jj):
        row0 = pl.multiple_of(jj * tn + jnp.where(jj >= first_shifted, skip, 0), SUBLANES)
        return pltpu.make_async_copy(wt_hbm.at[pl.ds(row0, tn), :], stage, sem)

    @pl.when((i == 0) & (j == 0))
    def _():
        fetch(j).start()

    @pl.when(i == 0)
    def _():
        fetch(j).wait()
        w_scr[...] = stage[...].astype(BF16)

        @pl.when(j + 1 < pl.num_programs(0))
        def _():
            fetch(j + 1).start()

    o_ref[...] = _dot_nt(x_ref[...], w_scr[...]) + b_ref[...]


def _proj(xb, wt, b, tm, tn, skip_lo, skip_hi):
    T, K = xb.shape
    skip = skip_hi - skip_lo
    N = wt.shape[0] - skip
    assert skip_lo % tn == 0 and skip % SUBLANES == 0 and N % tn == 0 and b.shape == (1, N)
    return pl.pallas_call(
        functools.partial(_proj_kernel, first_shifted=skip_lo // tn, skip=skip),
        grid=(N // tn, T // tm),
        in_specs=[
            pl.BlockSpec((tm, K), lambda j, i: (i, 0)),
            pl.BlockSpec(memory_space=pl.ANY),
            pl.BlockSpec((1, tn), lambda j, i: (0, j)),
        ],
        out_specs=pl.BlockSpec((tm, tn), lambda j, i: (i, j)),
        out_shape=jax.ShapeDtypeStruct((T, N), F32),
        scratch_shapes=[pltpu.VMEM((tn, K), F32), pltpu.VMEM((tn, K), BF16), pltpu.SemaphoreType.DMA(())],
        compiler_params=_cparams("arbitrary", "arbitrary"),
        name="in_proj",
    )(xb, wt, b)


def _split_bf16(a):
    hi = a.astype(BF16)
    lo = (a - hi.astype(F32)).astype(BF16)
    return hi, lo


def _gates_kernel(xa_ref, xb_ref, w_ref, b_ref, x_ref, xh_ref, o_ref, *, n_head):
    x = jnp.where(pl.program_id(0) < n_head, xa_ref[...], xb_ref[...])
    x_ref[...] = x
    xh, xl = _split_bf16(x)
    xh_ref[...] = xh
    wh, wl = _split_bf16(w_ref[...])
    o_ref[...] = _dot_nt(xh, wh) + _dot_nt(xh, wl) + _dot_nt(xl, wh) + b_ref[...]


def _gates(x_first, x_second, wgt, bg, tm):
    S, K = x_first.shape
    R = x_second.shape[0]
    assert S % tm == 0 and R % tm == 0
    n_head = S // tm
    T = S + R
    return pl.pallas_call(
        functools.partial(_gates_kernel, n_head=n_head),
        grid=(T // tm,),
        in_specs=[
            pl.BlockSpec((tm, K), lambda i: (jnp.minimum(i, n_head - 1), 0)),
            pl.BlockSpec((tm, K), lambda i: (jnp.maximum(i - n_head, 0), 0)),
            pl.BlockSpec((LANES, K), lambda i: (0, 0)),
            pl.BlockSpec((1, LANES), lambda i: (0, 0)),
        ],
        out_specs=[
            pl.BlockSpec((tm, K), lambda i: (i, 0)),
            pl.BlockSpec((tm, K), lambda i: (i, 0)),
            pl.BlockSpec((tm, LANES), lambda i: (i, 0)),
        ],
        out_shape=[
            jax.ShapeDtypeStruct((T, K), F32),
            jax.ShapeDtypeStruct((T, K), BF16),
            jax.ShapeDtypeStruct((T, LANES), F32),
        ],
        compiler_params=_cparams("parallel"),
        name="gate_proj",
    )(x_first, x_second, wgt, bg)


def _merge_kernel(m_ref, a_ref, wm_ref, wa_ref, gm_ref, ga_ref, o_ref):
    bm = _dot(m_ref[...], wm_ref[...])
    ba = _dot(a_ref[...].astype(BF16), wa_ref[...])
    o_ref[...] = (jax.nn.sigmoid(gm_ref[...]) * bm + jax.nn.sigmoid(ga_ref[...]) * ba).astype(o_ref.dtype)


def _merge(m_out, a_out, wm, wa, z, gm_blk, ga_blk, tm, tn):
    T, Km = m_out.shape
    Ka = a_out.shape[1]
    N = wm.shape[1]
    return pl.pallas_call(
        _merge_kernel,
        grid=(N // tn, T // tm),
        in_specs=[
            pl.BlockSpec((tm, Km), lambda j, i: (i, 0)),
            pl.BlockSpec((tm, Ka), lambda j, i: (i, 0)),
            pl.BlockSpec((Km, tn), lambda j, i: (0, j)),
            pl.BlockSpec((Ka, tn), lambda j, i: (0, j)),
            pl.BlockSpec((tm, tn), lambda j, i: (i, gm_blk + j)),
            pl.BlockSpec((tm, tn), lambda j, i: (i, ga_blk + j)),
        ],
        out_specs=pl.BlockSpec((tm, tn), lambda j, i: (i, j)),
        out_shape=jax.ShapeDtypeStruct((T, N), BF16),
        compiler_params=_cparams("parallel", "parallel"),
        name="branch_merge",
    )(m_out, a_out, wm, wa, z, z)


LN_ROWS = 32


def _layer_norm_rows(v, g, b):
    mu = jnp.mean(v, axis=-1, keepdims=True)
    c = v - mu
    var = jnp.mean(c * c, axis=-1, keepdims=True)
    return c * lax.rsqrt(var + LN_EPS) * g + b


def _outproj_ln_kernel(a_ref, w_ref, x_ref, g_ref, b_ref, o_ref, ob_ref):
    k = pl.program_id(1)

    @pl.when(k == 0)
    def _():
        o_ref[...] = ALPHA * x_ref[...] + _dot(a_ref[...], w_ref[...])

    @pl.when(k > 0)
    def _():
        o_ref[...] += _dot(a_ref[...], w_ref[...])

    @pl.when(k == pl.num_programs(1) - 1)
    def _():
        def norm_rows(r, carry):
            rows = pl.ds(pl.multiple_of(r * LN_ROWS, LN_ROWS), LN_ROWS)
            y = _layer_norm_rows(o_ref[rows, :], g_ref[...], b_ref[...])
            o_ref[rows, :] = y
            ob_ref[rows, :] = y.astype(BF16)
            return carry

        lax.fori_loop(0, o_ref.shape[0] // LN_ROWS, norm_rows, 0)


def _outproj_ln(a, w, x, g, b, tm, tk):
    T, K = a.shape
    D = w.shape[1]
    return pl.pallas_call(
        _outproj_ln_kernel,
        grid=(T // tm, K // tk),
        in_specs=[
            pl.BlockSpec((tm, tk), lambda i, k: (i, k)),
            pl.BlockSpec((tk, D), lambda i, k: (k, 0)),
            pl.BlockSpec((tm, D), lambda i, k: (i, 0), pipeline_mode=pl.Buffered(1)),
            pl.BlockSpec((1, D), lambda i, k: (0, 0)),
            pl.BlockSpec((1, D), lambda i, k: (0, 0)),
        ],
        out_specs=[
            pl.BlockSpec((tm, D), lambda i, k: (i, 0)),
            pl.BlockSpec((tm, D), lambda i, k: (i, 0)),
        ],
        out_shape=[jax.ShapeDtypeStruct((T, D), F32), jax.ShapeDtypeStruct((T, D), BF16)],
        compiler_params=_cparams("parallel", "arbitrary"),
        name="out_proj_ln",
    )(a, w, x, g, b)


def _residual_ln_kernel(x_ref, y_ref, g_ref, b_ref, head_ref, tail_ref, *, n_head):
    i = pl.program_id(0)

    def emit(dst_ref):
        def norm_rows(r, carry):
            rows = pl.ds(pl.multiple_of(r * LN_ROWS, LN_ROWS), LN_ROWS)
            dst_ref[rows, :] = _layer_norm_rows(ALPHA * x_ref[rows, :] + y_ref[rows, :], g_ref[...], b_ref[...])
            return carry

        lax.fori_loop(0, x_ref.shape[0] // LN_ROWS, norm_rows, 0)

    pl.when(i < n_head)(lambda: emit(head_ref))
    pl.when(i >= n_head)(lambda: emit(tail_ref))


def _residual_ln(x, y, g, b, tm, n_first):
    T, D = x.shape
    assert n_first % tm == 0 and (T - n_first) % tm == 0
    n_head = n_first // tm
    return pl.pallas_call(
        functools.partial(_residual_ln_kernel, n_head=n_head),
        grid=(T // tm,),
        in_specs=[
            pl.BlockSpec((tm, D), lambda i: (i, 0)),
            pl.BlockSpec((tm, D), lambda i: (i, 0)),
            pl.BlockSpec((1, D), lambda i: (0, 0)),
            pl.BlockSpec((1, D), lambda i: (0, 0)),
        ],
        out_specs=[
            pl.BlockSpec((tm, D), lambda i: (jnp.minimum(i, n_head - 1), 0)),
            pl.BlockSpec((tm, D), lambda i: (jnp.maximum(i - n_head, 0), 0)),
        ],
        out_shape=[jax.ShapeDtypeStruct((n_first, D), F32), jax.ShapeDtypeStruct((T - n_first, D), F32)],
        compiler_params=_cparams("arbitrary"),
        name="residual_ln",
    )(x, y, g, b)


def _log_sigmoid(x):
    return jnp.minimum(x, 0.0) - jnp.log1p(jnp.exp(-jnp.abs(x)))


def _lane_column(a, col):
    lane = lax.broadcasted_iota(jnp.int32, a.shape, 1)
    return jnp.sum(jnp.where(lane == col, a, 0.0), axis=1, keepdims=True)


def _mlstm_gate_math(ig_col, fg_col, ig_row, fg_row, m0_col, seg_shift):
    L = M_CHUNK
    t = lax.broadcasted_iota(jnp.int32, (L, L), 0)
    s = lax.broadcasted_iota(jnp.int32, (L, L), 1)
    lower, upper = s <= t, t <= s
    if seg_shift is not None:
        same = (t >> seg_shift) == (s >> seg_shift)
        lower, upper = lower & same, upper & same
    lf_col, lf_row = _log_sigmoid(fg_col), _log_sigmoid(fg_row)
    f_col = jnp.sum(jnp.where(lower, lf_row, 0.0), axis=1, keepdims=True)
    f_row = jnp.sum(jnp.where(upper, lf_col, 0.0), axis=0, keepdims=True)
    d = jnp.where(lower, f_col - f_row + ig_row, NEG_INF)
    inter = f_col + m0_col
    m_t = jnp.maximum(inter, jnp.max(d, axis=1, keepdims=True))
    w_intra = jnp.exp(d - m_t)
    w_inter = jnp.exp(inter - m_t)
    return f_col, f_row, m_t, w_intra, w_inter


def _mlstm_head_out(num, den, m_t, nw_row, mo):
    hid = num * (1.0 / jnp.maximum(jnp.abs(den), jnp.exp(-m_t)))
    mu = jnp.mean(hid, axis=-1, keepdims=True)
    c = hid - mu
    var = jnp.mean(c * c, axis=-1, keepdims=True)
    return c * lax.rsqrt(var + LN_EPS) * nw_row * jax.nn.sigmoid(mo)


def _mlstm_prompt_kernel(q_ref, k_ref, v_ref, mo_ref, g_ref, gt_ref, nw_ref, init_ref,
                         mout_ref, c_out_ref, n_out_ref, m_out_ref, c_scr, n_scr, m_scr):
    del init_ref
    hp = pl.program_id(0)
    c = pl.program_id(1)
    L = M_CHUNK
    HP = M_HEADS_PER_STEP
    dk = q_ref.shape[1] // HP
    dv = v_ref.shape[1] // HP

    @pl.when(c == 0)
    def _():
        c_scr[...] = jnp.zeros_like(c_scr)
        n_scr[...] = jnp.zeros_like(n_scr)
        m_scr[...] = jnp.zeros_like(m_scr)

    g = g_ref[...]
    for hh in range(HP):
        h = hp * HP + hh
        kcols = slice(hh * dk, (hh + 1) * dk)
        vcols = slice(hh * dv, (hh + 1) * dv)
        ig_col = _lane_column(g, h)
        fg_col = _lane_column(g, M_HEADS + h)
        ig_row = gt_ref[pl.ds(h, 1), :]
        fg_row = gt_ref[pl.ds(M_HEADS + h, 1), :]
        f_col, _, m_t, w_intra, w_inter = _mlstm_gate_math(ig_col, fg_col, ig_row, fg_row, m_scr[hh], None)

        q = q_ref[:, kcols]
        kf = k_ref[:, kcols] * (1.0 / math.sqrt(dk))
        qb, kb, vb = q.astype(BF16), kf.astype(BF16), v_ref[:, vcols].astype(BF16)
        c_prev = c_scr[hh]
        n_prev = n_scr[hh]
        s = _dot_nt(qb, kb) * w_intra
        num = w_inter * _dot(qb, c_prev.astype(BF16)) + _dot(s.astype(BF16), vb)
        den = w_inter * jnp.sum(q * n_prev, axis=1, keepdims=True) + jnp.sum(s, axis=1, keepdims=True)
        mout_ref[:, vcols] = _mlstm_head_out(num, den, m_t, nw_ref[:, vcols], mo_ref[:, vcols]).astype(mout_ref.dtype)

        m_end = m_t[L - 1:L, :]
        w_state = w_inter[L - 1:L, :]
        w_tok = jnp.exp(f_col[L - 1:L, :] - f_col + ig_col - m_end)
        kw = kf * w_tok
        c_scr[hh] = w_state * c_prev + _dot_tn(kw.astype(BF16), vb)
        n_scr[hh] = w_state * n_prev + jnp.sum(kw, axis=0, keepdims=True)
        m_scr[hh] = m_end

    @pl.when(c == pl.num_programs(1) - 1)
    def _():
        for hh in range(HP):
            c_out_ref[0, hh] = c_scr[hh]
            n_out_ref[hh] = n_scr[hh]
            m_out_ref[hh] = jnp.broadcast_to(m_scr[hh], (SUBLANES, LANES))


def _mlstm_prompt(z, g, gt, nw, s_len, t_all):
    dv = nw.shape[1] // M_HEADS
    dk = dv // 2
    L = M_CHUNK
    H = M_HEADS
    HP = M_HEADS_PER_STEP
    G = H // HP
    return pl.pallas_call(
        _mlstm_prompt_kernel,
        grid=(G, s_len // L),
        in_specs=[
            pl.BlockSpec((L, HP * dk), lambda h, c: (c, h)),
            pl.BlockSpec((L, HP * dk), lambda h, c: (c, G + h)),
            pl.BlockSpec((L, HP * dv), lambda h, c: (c, G + h)),
            pl.BlockSpec((L, HP * dv), lambda h, c: (c, 2 * G + h)),
            pl.BlockSpec((L, LANES), lambda h, c: (c, 0)),
            pl.BlockSpec((2 * H, L), lambda h, c: (0, c)),
            pl.BlockSpec((1, HP * dv), lambda h, c: (0, h)),
            pl.BlockSpec(memory_space=pl.ANY),
        ],
        out_specs=[
            pl.BlockSpec((L, HP * dv), lambda h, c: (c, h)),
            pl.BlockSpec((1, HP, dk, dv), lambda h, c: (0, h, 0, 0)),
            pl.BlockSpec((HP, 1, dk), lambda h, c: (h, 0, 0)),
            pl.BlockSpec((HP, SUBLANES, LANES), lambda h, c: (h, 0, 0)),
        ],
        out_shape=[
            jax.ShapeDtypeStruct((t_all, H * dv), BF16),
            jax.ShapeDtypeStruct((1, H, dk, dv), F32),
            jax.ShapeDtypeStruct((H, 1, dk), F32),
            jax.ShapeDtypeStruct((H, SUBLANES, LANES), F32),
        ],
        scratch_shapes=[pltpu.VMEM((HP, dk, dv), F32), pltpu.VMEM((HP, 1, dk), F32), pltpu.VMEM((HP, 1, 1), F32)],
        input_output_aliases={7: 0},
        compiler_params=_cparams("parallel", "arbitrary"),
        name="mlstm_prompt",
    )(z, z, z, z, g, gt, nw, jnp.zeros((t_all, H * dv), BF16))


def _mlstm_sample_kernel(q_ref, k_ref, v_ref, mo_ref, g_ref, gt_ref, nw_ref, m0_ref, c0_ref, n0_ref, alias_ref,
                         mout_ref, c_out_ref, n_out_ref, mt_out_ref,
                         num_scr, den_scr, kw_scr, vb_scr, qb_scr, wi_scr, mt_scr, *, seg_shift):
    del alias_ref
    h = pl.program_id(0)
    b = pl.program_id(1)
    L = M_CHUNK
    seg = 1 << seg_shift

    @pl.when(b == 0)
    def _():
        g = g_ref[...]
        ig_col = _lane_column(g, h)
        fg_col = _lane_column(g, M_HEADS + h)
        ig_row = gt_ref[pl.ds(h, 1), :]
        fg_row = gt_ref[pl.ds(M_HEADS + h, 1), :]
        m0_col = _lane_column(m0_ref[...], h)
        f_col, f_row, m_t, w_intra, w_inter = _mlstm_gate_math(ig_col, fg_col, ig_row, fg_row, m0_col, seg_shift)
        q = q_ref[...]
        kf = k_ref[...] * (1.0 / math.sqrt(q.shape[1]))
        qb, kb, vb = q.astype(BF16), kf.astype(BF16), v_ref[...].astype(BF16)
        s = _dot_nt(qb, kb) * w_intra
        num_scr[...] = _dot(s.astype(BF16), vb)
        den_scr[...] = jnp.sum(s, axis=1, keepdims=True)
        t = lax.broadcasted_iota(jnp.int32, (L, L), 0)
        sidx = lax.broadcasted_iota(jnp.int32, (L, L), 1)
        is_last = sidx == (t | (seg - 1))
        m_row = jnp.sum(jnp.where(t == sidx, m_t, 0.0), axis=0, keepdims=True)
        f_last = jnp.sum(jnp.where(is_last, f_row, 0.0), axis=1, keepdims=True)
        m_end = jnp.sum(jnp.where(is_last, m_row, 0.0), axis=1, keepdims=True)
        w_tok = jnp.exp(f_last - f_col + ig_col - m_end)
        kw_scr[...] = kf * w_tok
        vb_scr[...] = vb
        qb_scr[...] = qb
        wi_scr[...] = w_inter
        mt_scr[...] = m_t

    c_b = c0_ref[0, 0]
    n_b = n0_ref[0]
    row = lax.broadcasted_iota(jnp.int32, (L, 1), 0)
    in_b = (row >> seg_shift) == b
    wi = wi_scr[...]
    qc = _dot(qb_scr[...], c_b.astype(BF16))
    num_scr[...] += jnp.where(in_b, wi * qc, 0.0)
    qn = jnp.sum(q_ref[...] * n_b, axis=1, keepdims=True)
    den_scr[...] += jnp.where(in_b, wi * qn, 0.0)
    w_state = jnp.sum(jnp.where(row == seg * b + (seg - 1), wi, 0.0), axis=0, keepdims=True)
    kw_b = jnp.where(in_b, kw_scr[...], 0.0)
    c_out_ref[0, 0] = w_state * c_b + _dot_tn(kw_b.astype(BF16), vb_scr[...])
    n_out_ref[0] = w_state * n_b + jnp.sum(kw_b, axis=0, keepdims=True)

    @pl.when(b == pl.num_programs(1) - 1)
    def _():
        m_t = mt_scr[...]
        out = _mlstm_head_out(num_scr[...], den_scr[...], m_t, nw_ref[...], mo_ref[...])
        mout_ref[...] = out.astype(mout_ref.dtype)
        mt_out_ref[0] = jnp.broadcast_to(m_t, (L, LANES))


def _mlstm_sample(z, g, gt, nw, m0_rows, c0, n0, m_out, row0):
    dv = nw.shape[1] // M_HEADS
    dk = dv // 2
    L = M_CHUNK
    H = M_HEADS
    B = c0.shape[0]
    seg = L // B
    rb = row0 // L
    kern = functools.partial(_mlstm_sample_kernel, seg_shift=seg.bit_length() - 1)
    return pl.pallas_call(
        kern,
        grid=(H, B),
        in_specs=[
            pl.BlockSpec((L, dk), lambda h, b: (rb, h)),
            pl.BlockSpec((L, dk), lambda h, b: (rb, H + h)),
            pl.BlockSpec((L, dv), lambda h, b: (rb, H + h)),
            pl.BlockSpec((L, dv), lambda h, b: (rb, 2 * H + h)),
            pl.BlockSpec((L, LANES), lambda h, b: (rb, 0)),
            pl.BlockSpec((2 * H, L), lambda h, b: (0, rb)),
            pl.BlockSpec((1, dv), lambda h, b: (0, h)),
            pl.BlockSpec((L, LANES), lambda h, b: (0, 0)),
            pl.BlockSpec((1, 1, dk, dv), lambda h, b: (b, h, 0, 0)),
            pl.BlockSpec((1, 1, dk), lambda h, b: (b * H + h, 0, 0)),
            pl.BlockSpec(memory_space=pl.ANY),
        ],
        out_specs=[
            pl.BlockSpec((L, dv), lambda h, b: (rb, h)),
            pl.BlockSpec((1, 1, dk, dv), lambda h, b: (b, h, 0, 0)),
            pl.BlockSpec((1, 1, dk), lambda h, b: (b * H + h, 0, 0)),
            pl.BlockSpec((1, L, LANES), lambda h, b: (h, 0, 0)),
        ],
        out_shape=[
            jax.ShapeDtypeStruct(m_out.shape, m_out.dtype),
            jax.ShapeDtypeStruct(c0.shape, F32),
            jax.ShapeDtypeStruct(n0.shape, F32),
            jax.ShapeDtypeStruct((H, L, LANES), F32),
        ],
        scratch_shapes=[
            pltpu.VMEM((L, dv), F32), pltpu.VMEM((L, 1), F32), pltpu.VMEM((L, dk), F32), pltpu.VMEM((L, dv), BF16),
            pltpu.VMEM((L, dk), BF16), pltpu.VMEM((L, 1), F32), pltpu.VMEM((L, 1), F32),
        ],
        input_output_aliases={10: 0},
        compiler_params=_cparams("parallel", "arbitrary"),
        name="mlstm_sample",
    )(z, z, z, z, g, gt, nw, m0_rows, c0, n0, m_out)


A_SUPER = A_STEPS * max(A_DILATIONS)
ATT_SCALE = 1.0 / math.sqrt(A_DH)


def _rows(start, size, stride):
    return pl.ds(start, size) if stride == 1 else pl.ds(start, size, stride=stride)


def _group_softmax_merge(lses, outs):
    mx = functools.reduce(jnp.maximum, lses)
    ws = [jnp.exp(l - mx) for l in lses]
    tot = functools.reduce(lambda a, b: a + b, ws)
    acc = functools.reduce(lambda a, b: a + b, [w * o for w, o in zip(ws, outs)])
    return acc / tot


def _attn_prompt_kernel(sl_ref, *refs):
    in_refs = refs[:5 * A_GROUPS]
    a_ref, o_scr, l_scr = refs[5 * A_GROUPS + 1:]
    h = pl.program_id(0)
    n = pl.program_id(1)
    B = A_STEPS
    qi = lax.broadcasted_iota(jnp.int32, (B, B), 0)
    ki = lax.broadcasted_iota(jnp.int32, (B, B), 1)
    valid_prev, valid_cur = ki >= qi, ki <= qi
    step_prev = (qi - ki + B).astype(F32)
    step_cur = (qi - ki).astype(F32)

    ones = jnp.ones((2 * B, LANES), BF16)
    for g, dil in enumerate(A_DILATIONS):
        q_ref, kc_ref, kp_ref, vc_ref, vp_ref = in_refs[5 * g:5 * g + 5]
        slope = sl_ref[g, h] * float(dil)
        bias_cur = jnp.where(valid_cur, -slope * step_cur, NEG_INF)
        bias_prev = jnp.where(valid_prev, -slope * step_prev, NEG_INF)
        bias = jnp.concatenate([bias_prev, bias_cur], axis=1)
        bias_first = jnp.concatenate([jnp.where(n > 0, bias_prev, NEG_INF), bias_cur], axis=1)
        nblk = A_SUPER // (B * dil)
        for r in range(dil):
            for blk in range(nblk):
                start = r + dil * B * blk
                rows = _rows(start, B, dil)
                qb = q_ref[rows, :].astype(BF16)
                if blk > 0:
                    prow = _rows(start - dil * B, B, dil)
                    kp, vp, bs = kc_ref[prow, :], vc_ref[prow, :], bias
                else:
                    prow = _rows(r + dil * B * (nblk - 1), B, dil)
                    kp, vp, bs = kp_ref[prow, :], vp_ref[prow, :], bias_first
                kcat = jnp.concatenate([kp, kc_ref[rows, :]], axis=0).astype(BF16)
                vcat = jnp.concatenate([vp, vc_ref[rows, :]], axis=0).astype(BF16)
                s = _dot_nt(qb, kcat) * ATT_SCALE + bs
                mx = jnp.max(s, axis=1, keepdims=True)
                e = jnp.exp(s - mx).astype(BF16)
                pv = _dot(e, jnp.concatenate([vcat, ones], axis=1))
                den = pv[:, A_DH:]
                o_scr[g, rows, :] = pv[:, :A_DH] * (1.0 / den)
                l_scr[g, rows, :] = mx + jnp.log(den)

    def merge(i, carry):
        rows = pl.ds(pl.multiple_of(i * B, B), B)
        lses = [l_scr[g, rows, :] for g in range(A_GROUPS)]
        outs = [o_scr[g, rows, :] for g in range(A_GROUPS)]
        a_ref[rows, :] = _group_softmax_merge(lses, outs)
        return carry

    lax.fori_loop(0, A_SUPER // B, merge, 0)


def _attn_prompt(z, slopes, s_len, t_all, aq_blk, ak_blk, av_blk):
    R = A_SUPER
    in_specs = [pl.BlockSpec(memory_space=pltpu.SMEM)]
    args = [slopes]
    for g in range(A_GROUPS):
        off = g * A_HEADS
        cur = lambda h, n, base: (n, base + h)
        prev = lambda h, n, base: (jnp.maximum(n - 1, 0), base + h)
        for base, fn in ((aq_blk + off, cur), (ak_blk + off, cur), (ak_blk + off, prev), (av_blk + off, cur), (av_blk + off, prev)):
            in_specs.append(pl.BlockSpec((R, A_DH), functools.partial(fn, base=base)))
            args.append(z)
    in_specs.append(pl.BlockSpec(memory_space=pl.ANY))
    args.append(jnp.zeros((t_all, A_HEADS * A_DH), F32))
    return pl.pallas_call(
        _attn_prompt_kernel,
        grid=(A_HEADS, s_len // R),
        in_specs=in_specs,
        out_specs=pl.BlockSpec((R, A_DH), lambda h, n: (n, h)),
        out_shape=jax.ShapeDtypeStruct((t_all, A_HEADS * A_DH), F32),
        scratch_shapes=[pltpu.VMEM((A_GROUPS, R, A_DH), F32), pltpu.VMEM((A_GROUPS, R, LANES), F32)],
        input_output_aliases={len(args) - 1: 0},
        compiler_params=_cparams("parallel", "arbitrary"),
        name="attn_prompt",
    )(*args)


def _attn_sample_kernel(sl_ref, *refs, seq):
    G = A_GROUPS
    z_refs = refs[:3 * G]
    buf_refs = refs[3 * G:4 * G]
    new_refs = refs[4 * G:5 * G]
    a_ref = refs[5 * G + 1]
    cache_out_refs = refs[5 * G + 2:6 * G + 2]
    sems = refs[6 * G + 2]
    b = pl.program_id(0)
    half = b % (SUBLANES // seq)
    PADK = LANES

    rows_per_pos = 2 * A_HEADS
    copies = []
    for g in range(G):
        kept = buf_refs[g].shape[1] - seq * rows_per_pos
        copies.append(pltpu.make_async_copy(buf_refs[g].at[0, pl.ds(seq * rows_per_pos, kept), :],
                                            cache_out_refs[g].at[b, pl.ds(0, kept), :], sems.at[2 * g]))
        copies.append(pltpu.make_async_copy(new_refs[g].at[0],
                                            cache_out_refs[g].at[b, pl.ds(kept, seq * rows_per_pos), :],
                                            sems.at[2 * g + 1]))
    for cp in copies:
        cp.start()

    for h in range(A_HEADS):
        cols = slice(h * A_DH, (h + 1) * A_DH)
        lses, outs = [], []
        for g, dil in enumerate(A_DILATIONS):
            q_ref, kn_ref, vn_ref = z_refs[3 * g:3 * g + 3]
            buf_ref = buf_refs[g]
            lb = A_STEPS * dil
            slope = sl_ref[g, h]
            kbuf = buf_ref[0, pl.ds(h, lb, stride=2 * A_HEADS), :]
            vbuf = buf_ref[0, pl.ds(A_HEADS + h, lb, stride=2 * A_HEADS), :]
            pad = jnp.zeros((PADK - SUBLANES, A_DH), F32)
            kx = jnp.concatenate([kbuf, kn_ref[:, cols], pad], axis=0).astype(BF16)
            vx = jnp.concatenate([vbuf, vn_ref[:, cols], pad], axis=0).astype(BF16)
            s = _dot_nt(q_ref[:, cols].astype(BF16), kx) * ATT_SCALE
            col = lax.broadcasted_iota(jnp.int32, s.shape, 1)
            t = lax.broadcasted_iota(jnp.int32, s.shape, 0) - seq * half
            t_new = col - lb - seq * half
            in_buf = col < lb
            pos = jnp.where(in_buf, col, lb + t_new)
            delta = lb + t - pos
            valid = (in_buf | ((t_new >= 0) & (t_new < seq))) & (delta >= 0) & (delta <= A_STEPS * dil) \
                & ((delta & (dil - 1)) == 0)
            s = jnp.where(valid, s - slope * delta.astype(F32), NEG_INF)
            mx = jnp.max(s, axis=1, keepdims=True)
            e = jnp.exp(s - mx)
            den = jnp.sum(e, axis=1, keepdims=True)
            outs.append(_dot(e.astype(BF16), vx) * (1.0 / den))
            lses.append(mx + jnp.log(den))
        merged = _group_softmax_merge(lses, outs)
        for hv in range(SUBLANES // seq):
            @pl.when(half == hv)
            def _(merged=merged, hv=hv):
                a_ref[hv * seq:(hv + 1) * seq, cols] = merged[hv * seq:(hv + 1) * seq, :]

    for cp in copies:
        cp.wait()


def _attn_sample(z, slopes, bufs, kv_new, a_out, row0, aq_blk, ak_blk, av_blk, seq):
    B = bufs[0].shape[0]
    W = A_HEADS * A_DH
    per_blk = SUBLANES // seq
    rb0 = row0 // SUBLANES
    in_specs = [pl.BlockSpec(memory_space=pltpu.SMEM)]
    args = [slopes]
    for g in range(A_GROUPS):
        for base in (aq_blk + g, ak_blk + g, av_blk + g):
            in_specs.append(pl.BlockSpec((SUBLANES, W), functools.partial(lambda b, base: (rb0 + b // per_blk, base), base=base)))
            args.append(z)
    for arrs in (bufs, kv_new):
        for g in range(A_GROUPS):
            in_specs.append(pl.BlockSpec((1,) + arrs[g].shape[1:], lambda b: (b, 0, 0)))
            args.append(arrs[g])
    in_specs.append(pl.BlockSpec(memory_space=pl.ANY))
    args.append(a_out)
    outs = pl.pallas_call(
        functools.partial(_attn_sample_kernel, seq=seq),
        grid=(B,),
        in_specs=in_specs,
        out_specs=[pl.BlockSpec((SUBLANES, W), lambda b: (rb0 + b // per_blk, 0))]
        + [pl.BlockSpec(memory_space=pl.ANY)] * A_GROUPS,
        out_shape=[jax.ShapeDtypeStruct(a_out.shape, a_out.dtype)]
        + [jax.ShapeDtypeStruct(c.shape, c.dtype) for c in bufs],
        scratch_shapes=[pltpu.SemaphoreType.DMA((2 * A_GROUPS,))],
        input_output_aliases={len(args) - 1: 0},
        compiler_params=_cparams("arbitrary"),
        name="attn_sample",
    )(*args)
    return outs[0], outs[1:]


ST_TAU, ST_MAX0, ST_MAX1, ST_INVZ = 0, 1, 2, 3
ST_ROWS = SUBLANES
PEER_TOKEN_CHUNK = 256
PEER_KEY_ROWS = 32
PEER_KEY_GROUPS = PEER_KEY_ROWS // SUBLANES


def _take_max(sc, out, k):
    rows = lax.broadcasted_iota(jnp.int32, sc.shape, 0)
    out_rows = lax.broadcasted_iota(jnp.int32, out.shape, 0)
    m = jnp.max(sc, axis=0, keepdims=True)
    first = jnp.min(jnp.where(sc == m, rows, sc.shape[0]), axis=0, keepdims=True)
    return jnp.where(rows == first, NEG_INF, sc), jnp.where(out_rows == k, m, out)


def _pair_candidates(top0, top1):
    assert P_TOPK == 16 and SUBLANES == 8
    lo0, hi0, lo1, hi1 = top0[0:8], top0[8:16], top1[0:8], top1[8:16]
    skip = lax.broadcasted_iota(jnp.int32, lo0.shape, 0) < 4
    blocks = [top0[0:1] + lo1, top0[0:1] + hi1]
    blocks += [top0[i:i + 1] + lo1 for i in (1, 2, 3)]
    blocks += [jnp.where(skip, NEG_INF, lo0 + top1[0:1]), hi0 + top1[0:1]]
    blocks += [jnp.where(skip, NEG_INF, lo0 + top1[j:j + 1]) for j in (1, 2, 3)]
    return jnp.concatenate(blocks, axis=0)


N_PAIR_ROWS = 10 * SUBLANES


def _peer_scores_kernel(x_ref, wq_ref, sk_ref, sc_ref, st_ref, q_scr):
    tm = x_ref.shape[0]
    dkh = sk_ref.shape[2]
    q_scr[...] = _dot(x_ref[...], wq_ref[...]).astype(BF16)
    for hp in range(2 * P_HEADS):
        sc_ref[hp] = _dot_nt(sk_ref[hp], q_scr[:, hp * dkh:(hp + 1) * dkh])

    def per_chunk(ci, carry):
        lanes = pl.ds(pl.multiple_of(ci * LANES, LANES), LANES)
        empty = jnp.zeros((P_TOPK, LANES), F32)

        def per_head(h, carry):
            cand, max0, max1 = carry
            hh = jnp.minimum(h, P_HEADS - 1)
            sc0 = sc_ref[2 * hh, :, lanes]
            sc1 = sc_ref[2 * hh + 1, :, lanes]
            top0, top1, best = empty, empty, empty
            for k in range(P_TOPK):
                sc0, top0 = _take_max(sc0, top0, k)
                sc1, top1 = _take_max(sc1, top1, k)
                cand, best = _take_max(cand, best, k)
            z = jnp.sum(jnp.exp(best - best[0:1, :]), axis=0, keepdims=True)
            st_ref[jnp.maximum(h - 1, 0), :, lanes] = jnp.concatenate(
                [best[P_TOPK - 1:P_TOPK, :], max0, max1, 1.0 / z, jnp.zeros((ST_ROWS - 4, LANES), F32)], axis=0)
            return _pair_candidates(top0, top1), top0[0:1, :], top1[0:1, :]

        row = jnp.zeros((1, LANES), F32)
        lax.fori_loop(0, P_HEADS + 1, per_head, (jnp.zeros((N_PAIR_ROWS, LANES), F32), row, row))
        return carry

    lax.fori_loop(0, tm // LANES, per_chunk, 0)


def _peer_scores(xb, wq, sk, tm):
    T, D = xb.shape
    nq = wq.shape[1]
    return pl.pallas_call(
        _peer_scores_kernel,
        grid=(T // tm,),
        in_specs=[
            pl.BlockSpec((tm, D), lambda i: (i, 0)),
            pl.BlockSpec((D, nq), lambda i: (0, 0)),
            pl.BlockSpec(sk.shape, lambda i: (0, 0, 0)),
        ],
        out_specs=[
            pl.BlockSpec((2 * P_HEADS, P_NKEYS, tm), lambda i: (0, 0, i)),
            pl.BlockSpec((P_HEADS, ST_ROWS, tm), lambda i: (0, 0, i)),
        ],
        out_shape=[
            jax.ShapeDtypeStruct((2 * P_HEADS, P_NKEYS, T), F32),
            jax.ShapeDtypeStruct((P_HEADS, ST_ROWS, T), F32),
        ],
        scratch_shapes=[pltpu.VMEM((tm, nq), BF16)],
        compiler_params=_cparams("parallel"),
        name="peer_scores",
    )(xb, wq, sk)


def _gelu(a):
    return 0.5 * a * (1.0 + lax.erf(a * (1.0 / math.sqrt(2.0))))


def _peer_dense_kernel(x_ref, sc_ref, st_ref, u_ref, v_ref, y_ref,
                       e1_scr, tau_scr, row_scr, act_a, act_b, *at_scrs, n_e):
    e = pl.program_id(1)
    tm = x_ref.shape[0]
    te = u_ref.shape[0]
    K = P_NKEYS
    n_ig = te // K

    @pl.when(e == 0)
    def _():
        y_ref[...] = jnp.zeros_like(y_ref)
        act_b[...] = jnp.zeros_like(act_b)
        for h in range(P_HEADS):
            e1_scr[h] = jnp.exp(sc_ref[2 * h + 1] - st_ref[h, ST_MAX1:ST_MAX1 + 1, :])
            tau_scr[h] = jnp.broadcast_to(st_ref[h, ST_TAU:ST_TAU + 1, :], (SUBLANES, tm))

    chunks = [(c0, min(PEER_TOKEN_CHUNK, tm - c0)) for c0 in range(0, tm, PEER_TOKEN_CHUNK)]
    assert len(chunks) == len(at_scrs)
    d_model = v_ref.shape[1]
    n_parts = max(len(chunks) - 1, 1)
    part = d_model // n_parts

    def step(act_cur, act_prev):
        def pre_activations(k):
            c0, cw = chunks[k]
            at_scrs[k][...] = _dot_nt(u_ref[...], x_ref[c0:c0 + cw, :])

        def second_matmul(p):
            cols = slice(p * part, (p + 1) * part)
            y_ref[:, cols] += _dot_tn(act_prev[...], v_ref[:, cols])

        pre_activations(0)
        for p in range(n_parts):
            second_matmul(p)
            if p + 1 < len(chunks):
                pre_activations(p + 1)

        i0 = e * n_ig
        sub = lax.broadcasted_iota(jnp.int32, (SUBLANES, tm), 0)
        for ig in range(n_ig):
            i = i0 + ig
            i_grp = i // SUBLANES
            for h in range(P_HEADS):
                s0 = jnp.sum(jnp.where(sub == i - i_grp * SUBLANES, sc_ref[2 * h, i_grp], 0.0), axis=0, keepdims=True)
                w0 = jnp.exp(s0 - st_ref[h, ST_MAX0:ST_MAX0 + 1, :]) * st_ref[h, ST_INVZ:ST_INVZ + 1, :]
                row_scr[h, ig, 0] = jnp.broadcast_to(s0, (SUBLANES, tm))
                row_scr[h, ig, 1] = jnp.broadcast_to(w0, (SUBLANES, tm))

        for k, (c0, cw) in enumerate(chunks):
            at_scr = at_scrs[k]
            for l0 in range(c0, c0 + cw, LANES):
                lanes = slice(l0, l0 + LANES)
                at_lanes = slice(l0 - c0, l0 - c0 + LANES)
                for jq in range(K // PEER_KEY_ROWS):
                    jg = slice(jq * PEER_KEY_GROUPS, (jq + 1) * PEER_KEY_GROUPS)
                    gates = [jnp.zeros((PEER_KEY_GROUPS, SUBLANES, LANES), F32) for _ in range(n_ig)]
                    for h in range(P_HEADS):
                        s1 = sc_ref[2 * h + 1, jg, :, lanes]
                        e1 = e1_scr[h, jg, :, lanes]
                        tau = tau_scr[h, :, lanes]
                        for ig in range(n_ig):
                            total = s1 + row_scr[h, ig, 0, :, lanes]
                            gates[ig] = gates[ig] + jnp.where(total >= tau, e1 * row_scr[h, ig, 1, :, lanes], 0.0)
                    for ig in range(n_ig):
                        rows = slice(ig * K + jq * PEER_KEY_ROWS, ig * K + (jq + 1) * PEER_KEY_ROWS)
                        gate = gates[ig].reshape(PEER_KEY_ROWS, LANES)
                        act_cur[rows, lanes] = (_gelu(at_scr[rows, at_lanes]) * gate).astype(BF16)

    pl.when((e < n_e) & (e % 2 == 0))(lambda: step(act_a, act_b))
    pl.when((e < n_e) & (e % 2 == 1))(lambda: step(act_b, act_a))
    last_written = act_a if (n_e - 1) % 2 == 0 else act_b

    @pl.when(e == n_e)
    def _():
        y_ref[...] += _dot_tn(last_written[...], v_ref[...])


def _peer_dense(xb, sc, st, ub, vb, tm, te):
    T, D = xb.shape
    E = ub.shape[0]
    n_e = E // te
    once = pl.Buffered(1)
    return pl.pallas_call(
        functools.partial(_peer_dense_kernel, n_e=n_e),
        grid=(T // tm, n_e + 1),
        in_specs=[
            pl.BlockSpec((tm, D), lambda i, e: (i, 0), pipeline_mode=once),
            pl.BlockSpec((2 * P_HEADS, P_NKEYS // SUBLANES, SUBLANES, tm), lambda i, e: (0, 0, 0, i), pipeline_mode=once),
            pl.BlockSpec((P_HEADS, ST_ROWS, tm), lambda i, e: (0, 0, i)),
            pl.BlockSpec((te, D), lambda i, e: (jnp.minimum(e, n_e - 1), 0)),
            pl.BlockSpec((te, D), lambda i, e: (jnp.maximum(e - 1, 0), 0)),
        ],
        out_specs=pl.BlockSpec((tm, D), lambda i, e: (i, 0)),
        out_shape=jax.ShapeDtypeStruct((T, D), F32),
        scratch_shapes=[
            pltpu.VMEM((P_HEADS, P_NKEYS // SUBLANES, SUBLANES, tm), F32),
            pltpu.VMEM((P_HEADS, SUBLANES, tm), F32),
            pltpu.VMEM((P_HEADS, te // P_NKEYS, 2, SUBLANES, tm), F32),
            pltpu.VMEM((te, tm), BF16),
            pltpu.VMEM((te, tm), BF16),
        ] + [pltpu.VMEM((te, min(PEER_TOKEN_CHUNK, tm - c0)), F32) for c0 in range(0, tm, PEER_TOKEN_CHUNK)],
        compiler_params=_cparams("parallel", "arbitrary"),
        name="peer_dense",
    )(xb, sc.reshape(2 * P_HEADS, P_NKEYS // SUBLANES, SUBLANES, T), st, ub, vb)


def _tile(n, *candidates):
    for c in candidates:
        if n % c == 0:
            return c
    raise ValueError(f"no tile of {candidates} divides {n}")


def kernel(x_prompt, x_sample, state_C, state_n, state_m, cache_kv_w128, cache_kv_w512, cache_kv_w2048,
           w_in, b_in, mlstm_norm_w, w_branch_m, w_branch_a, w_out, ln1_g, ln1_b,
           w_query, sub_keys, expert_u, expert_v, ln2_g, ln2_b):
    bp, S, D = x_prompt.shape
    DB, DS, _ = x_sample.shape
    H = M_HEADS
    dv, dk = D // 8, D // 16
    AW = A_HEADS * A_DH
    caches = (cache_kv_w128, cache_kv_w512, cache_kv_w2048)
    assert bp == 1 and DB * DS == M_CHUNK and S % A_SUPER == 0
    assert all(c.shape[1] == A_STEPS * d for c, d in zip(caches, A_DILATIONS))
    T = S + DB * DS

    g_lo = 2 * H * dk + 2 * H * dv
    g_hi = g_lo + 2 * H
    w_in_t = w_in.T
    b_main = jnp.concatenate([b_in[:g_lo], b_in[g_hi:]])[None, :]
    w_gate_t = jnp.pad(w_in_t[g_lo:g_hi], ((0, LANES - 2 * H), (0, 0)))
    b_gate = jnp.pad(b_in[g_lo:g_hi], (0, LANES - 2 * H))[None, :]
    tm = _tile(T, 640, 128)
    x, xb, gates = _gates(x_prompt.reshape(S, D), x_sample.reshape(DB * DS, D), w_gate_t, b_gate, M_CHUNK)
    z = _proj(xb, w_in_t, b_main, _tile(T, 1040, 128), 1024, g_lo, g_hi)
    gates_t = gates[:, :2 * H].T
    aq_col, ak_col, av_col = g_lo, g_lo + A_GROUPS * AW, g_lo + 2 * A_GROUPS * AW
    gm_col = g_lo + 3 * A_GROUPS * AW
    ga_col = gm_col + D

    nw = mlstm_norm_w[None, :]
    m_out, c_p, n_p, m_p = _mlstm_prompt(z, gates, gates_t, nw, S, T)
    m0_rows = jnp.pad(jnp.repeat(state_m, DS, axis=0), ((0, 0), (0, LANES - H)))
    m_out, c_s, n_s, mt_s = _mlstm_sample(z, gates, gates_t, nw, m0_rows, state_C,
                                          state_n.reshape(DB * H, 1, dk), m_out, S)

    n_slopes = A_GROUPS * A_HEADS
    slopes = (2.0 ** (-8.0 * (jnp.arange(n_slopes, dtype=F32) + 1.0) / n_slopes)).reshape(A_GROUPS, A_HEADS)
    a_out = _attn_prompt(z, slopes, S, T, aq_col // A_DH, ak_col // A_DH, av_col // A_DH)
    bufs = [c.reshape(DB, c.shape[1] * 2 * A_HEADS, A_DH) for c in caches]
    kv_new = []
    for g in range(A_GROUPS):
        kg = z[S:, ak_col + g * AW:ak_col + (g + 1) * AW]
        vg = z[S:, av_col + g * AW:av_col + (g + 1) * AW]
        kv_new.append(jnp.stack([kg, vg], axis=1).reshape(DB, DS * 2 * A_HEADS, A_DH))
    a_out, kv_s = _attn_sample(z, slopes, bufs, kv_new, a_out, S, aq_col // AW, ak_col // AW, av_col // AW, DS)
    kv_s = [k.reshape(c.shape) for k, c in zip(kv_s, caches)]

    merged = _merge(m_out, a_out, w_branch_m.astype(BF16), w_branch_a.astype(BF16), z,
                    gm_col // 1024, ga_col // 1024, tm, 1024)
    x1, x1b = _outproj_ln(merged, w_out.astype(BF16), x, ln1_g[None, :], ln1_b[None, :], tm, 512)
    sk = sub_keys.reshape(2 * P_HEADS, P_NKEYS, sub_keys.shape[-1]).astype(BF16)
    sc, st = _peer_scores(x1b, w_query.astype(BF16), sk, tm)
    y = _peer_dense(x1b, sc, st, expert_u.astype(BF16), expert_v.astype(BF16), tm, 512)
    x2_prompt, x2_sample = _residual_ln(x1, y, ln2_g[None, :], ln2_b[None, :], M_CHUNK, S)

    y_prompt = x2_prompt.reshape(1, S, D)
    y_sample = x2_sample.reshape(DB, DS, D)
    kv_p = []
    for g, cache in enumerate(caches):
        win = cache.shape[1]
        kg = z[S - win:S, ak_col + g * AW:ak_col + (g + 1) * AW]
        vg = z[S - win:S, av_col + g * AW:av_col + (g + 1) * AW]
        kv_p.append(jnp.stack([kg, vg], axis=1).reshape(1, win, 2, A_HEADS, A_DH))
    m_s = mt_s[:, DS - 1::DS, 0].T
    return (y_prompt, y_sample, c_p, n_p.reshape(1, H, dk), m_p[:, 0, 0][None, :],
            kv_p[0], kv_p[1], kv_p[2],
            c_s, n_s.reshape(DB, H, dk), m_s, kv_s[0], kv_s[1], kv_s[2])
```

```python
import functools
import math

import jax
import jax.numpy as jnp
from jax import lax
from jax.experimental import pallas as pl
from jax.experimental.pallas import tpu as pltpu

F32 = jnp.float32
BF16 = jnp.bfloat16

M_HEADS = 8
M_CHUNK = 128
M_HEADS_PER_STEP = 2
A_DILATIONS = (1, 4, 16)
A_GROUPS = len(A_DILATIONS)
A_HEADS = 8
A_DH = 128
A_STEPS = 128
P_HEADS = 8
P_NKEYS = 128
P_TOPK = 16
DEPTH = 1
ALPHA = (2 * DEPTH) ** 0.25
LN_EPS = 1e-5

LANES = 128
SUBLANES = 8
VMEM_LIMIT = 58 * 1024 * 1024
NEG_INF = float("-inf")


def _cparams(*sem):
    return pltpu.CompilerParams(dimension_semantics=sem, vmem_limit_bytes=VMEM_LIMIT)


def _dot(a, b):
    return jnp.dot(a, b, preferred_element_type=F32)


def _dot_nt(a, b):
    return lax.dot_general(a, b, (((1,), (1,)), ((), ())), preferred_element_type=F32)


def _dot_tn(a, b):
    return lax.dot_general(a, b, (((0,), (0,)), ((), ())), preferred_element_type=F32)


def _proj_kernel(x_ref, wt_hbm, b_ref, o_ref, stage, w_scr, sem, *, first_shifted, skip):
    j = pl.program_id(0)
    i = pl.program_id(1)
    tn = stage.shape[0]

    def fetch(jj):
        row0 = pl.multiple_of(jj * tn + jnp.where(jj >= first_shifted, skip, 0), SUBLANES)
        return pltpu.make_async_copy(wt_hbm.at[pl.ds(row0, tn), :], stage, sem)

    @pl.when((i == 0) & (j == 0))
    def _():
        fetch(j).start()

    @pl.when(i == 0)
    def _():
        fetch(j).wait()
        w_scr[...] = stage[...].astype(BF16)

        @pl.when(j + 1 < pl.num_programs(0))
        def _():
            fetch(j + 1).start()

    o_ref[...] = _dot_nt(x_ref[...], w_scr[...]) + b_ref[...]


def _proj(xb, wt, b, tm, tn, skip_lo, skip_hi):
    T, K = xb.shape
    skip = skip_hi - skip_lo
    N = wt.shape[0] - skip
    assert skip_lo % tn == 0 and skip % SUBLANES == 0 and N % tn == 0 and b.shape == (1, N)
    return pl.pallas_call(
        functools.partial(_proj_kernel, first_shifted=skip_lo // tn, skip=skip),
        grid=(N // tn, T // tm),
        in_specs=[
            pl.BlockSpec((tm, K), lambda j, i: (i, 0)),
            pl.BlockSpec(memory_space=pl.ANY),
            pl.BlockSpec((1, tn), lambda j, i: (0, j)),
        ],
        out_specs=pl.BlockSpec((tm, tn), lambda j, i: (i, j)),
        out_shape=jax.ShapeDtypeStruct((T, N), F32),
        scratch_shapes=[pltpu.VMEM((tn, K), F32), pltpu.VMEM((tn, K), BF16), pltpu.SemaphoreType.DMA(())],
        compiler_params=_cparams("arbitrary", "arbitrary"),
        name="in_proj",
    )(xb, wt, b)


def _split_bf16(a):
    hi = a.astype(BF16)
    lo = (a - hi.astype(F32)).astype(BF16)
    return hi, lo


def _gates_kernel(xa_ref, xb_ref, w_ref, b_ref, x_ref, xh_ref, o_ref, *, n_head):
    x = jnp.where(pl.program_id(0) < n_head, xa_ref[...], xb_ref[...])
    x_ref[...] = x
    xh, xl = _split_bf16(x)
    xh_ref[...] = xh
    wh, wl = _split_bf16(w_ref[...])
    o_ref[...] = _dot_nt(xh, wh) + _dot_nt(xh, wl) + _dot_nt(xl, wh) + b_ref[...]


def _gates(x_first, x_second, wgt, bg, tm):
    S, K = x_first.shape
    R = x_second.shape[0]
    assert S % tm == 0 and R % tm == 0
    n_head = S // tm
    T = S + R
    return pl.pallas_call(
        functools.partial(_gates_kernel, n_head=n_head),
        grid=(T // tm,),
        in_specs=[
            pl.BlockSpec((tm, K), lambda i: (jnp.minimum(i, n_head - 1), 0)),
            pl.BlockSpec((tm, K), lambda i: (jnp.maximum(i - n_head, 0), 0)),
            pl.BlockSpec((LANES, K), lambda i: (0, 0)),
            pl.BlockSpec((1, LANES), lambda i: (0, 0)),
        ],
        out_specs=[
            pl.BlockSpec((tm, K), lambda i: (i, 0)),
            pl.BlockSpec((tm, K), lambda i: (i, 0)),
            pl.BlockSpec((tm, LANES), lambda i: (i, 0)),
        ],
        out_shape=[
            jax.ShapeDtypeStruct((T, K), F32),
            jax.ShapeDtypeStruct((T, K), BF16),
            jax.ShapeDtypeStruct((T, LANES), F32),
        ],
        compiler_params=_cparams("parallel"),
        name="gate_proj",
    )(x_first, x_second, wgt, bg)


def _merge_kernel(m_ref, a_ref, wm_ref, wa_ref, gm_ref, ga_ref, o_ref):
    bm = _dot(m_ref[...], wm_ref[...])
    ba = _dot(a_ref[...].astype(BF16), wa_ref[...])
    o_ref[...] = (jax.nn.sigmoid(gm_ref[...]) * bm + jax.nn.sigmoid(ga_ref[...]) * ba).astype(o_ref.dtype)


def _merge(m_out, a_out, wm, wa, z, gm_blk, ga_blk, tm, tn):
    T, Km = m_out.shape
    Ka = a_out.shape[1]
    N = wm.shape[1]
    return pl.pallas_call(
        _merge_kernel,
        grid=(N // tn, T // tm),
        in_specs=[
            pl.BlockSpec((tm, Km), lambda j, i: (i, 0)),
            pl.BlockSpec((tm, Ka), lambda j, i: (i, 0)),
            pl.BlockSpec((Km, tn), lambda j, i: (0, j)),
            pl.BlockSpec((Ka, tn), lambda j, i: (0, j)),
            pl.BlockSpec((tm, tn), lambda j, i: (i, gm_blk + j)),
            pl.BlockSpec((tm, tn), lambda j, i: (i, ga_blk + j)),
        ],
        out_specs=pl.BlockSpec((tm, tn), lambda j, i: (i, j)),
        out_shape=jax.ShapeDtypeStruct((T, N), BF16),
        compiler_params=_cparams("parallel", "parallel"),
        name="branch_merge",
    )(m_out, a_out, wm, wa, z, z)


LN_ROWS = 64


def _layer_norm_rows(v, g, b):
    mu = jnp.mean(v, axis=-1, keepdims=True)
    c = v - mu
    var = jnp.mean(c * c, axis=-1, keepdims=True)
    return c * lax.rsqrt(var + LN_EPS) * g + b


def _outproj_ln_kernel(a_ref, w_ref, x_ref, g_ref, b_ref, o_ref, ob_ref):
    k = pl.program_id(1)

    @pl.when(k == 0)
    def _():
        o_ref[...] = ALPHA * x_ref[...] + _dot(a_ref[...], w_ref[...])

    @pl.when(k > 0)
    def _():
        o_ref[...] += _dot(a_ref[...], w_ref[...])

    @pl.when(k == pl.num_programs(1) - 1)
    def _():
        def norm_rows(r, carry):
            rows = pl.ds(pl.multiple_of(r * LN_ROWS, LN_ROWS), LN_ROWS)
            y = _layer_norm_rows(o_ref[rows, :], g_ref[...], b_ref[...])
            o_ref[rows, :] = y
            ob_ref[rows, :] = y.astype(BF16)
            return carry

        lax.fori_loop(0, o_ref.shape[0] // LN_ROWS, norm_rows, 0)


def _outproj_ln(a, w, x, g, b, tm, tk):
    T, K = a.shape
    D = w.shape[1]
    return pl.pallas_call(
        _outproj_ln_kernel,
        grid=(T // tm, K // tk),
        in_specs=[
            pl.BlockSpec((tm, tk), lambda i, k: (i, k)),
            pl.BlockSpec((tk, D), lambda i, k: (k, 0)),
            pl.BlockSpec((tm, D), lambda i, k: (i, 0), pipeline_mode=pl.Buffered(1)),
            pl.BlockSpec((1, D), lambda i, k: (0, 0)),
            pl.BlockSpec((1, D), lambda i, k: (0, 0)),
        ],
        out_specs=[
            pl.BlockSpec((tm, D), lambda i, k: (i, 0)),
            pl.BlockSpec((tm, D), lambda i, k: (i, 0)),
        ],
        out_shape=[jax.ShapeDtypeStruct((T, D), F32), jax.ShapeDtypeStruct((T, D), BF16)],
        compiler_params=_cparams("parallel", "arbitrary"),
        name="out_proj_ln",
    )(a, w, x, g, b)


def _residual_ln_kernel(x_ref, y_ref, g_ref, b_ref, head_ref, tail_ref, *, n_head):
    i = pl.program_id(0)

    def emit(dst_ref):
        def norm_rows(r, carry):
            rows = pl.ds(pl.multiple_of(r * LN_ROWS, LN_ROWS), LN_ROWS)
            dst_ref[rows, :] = _layer_norm_rows(ALPHA * x_ref[rows, :] + y_ref[rows, :], g_ref[...], b_ref[...])
            return carry

        lax.fori_loop(0, x_ref.shape[0] // LN_ROWS, norm_rows, 0)

    pl.when(i < n_head)(lambda: emit(head_ref))
    pl.when(i >= n_head)(lambda: emit(tail_ref))


def _residual_ln(x, y, g, b, tm, n_first):
    T, D = x.shape
    assert n_first % tm == 0 and (T - n_first) % tm == 0
    n_head = n_first // tm
    return pl.pallas_call(
        functools.partial(_residual_ln_kernel, n_head=n_head),
        grid=(T // tm,),
        in_specs=[
            pl.BlockSpec((tm, D), lambda i: (i, 0)),
            pl.BlockSpec((tm, D), lambda i: (i, 0)),
            pl.BlockSpec((1, D), lambda i: (0, 0)),
            pl.BlockSpec((1, D), lambda i: (0, 0)),
        ],
        out_specs=[
            pl.BlockSpec((tm, D), lambda i: (jnp.minimum(i, n_head - 1), 0)),
            pl.BlockSpec((tm, D), lambda i: (jnp.maximum(i - n_head, 0), 0)),
        ],
        out_shape=[jax.ShapeDtypeStruct((n_first, D), F32), jax.ShapeDtypeStruct((T - n_first, D), F32)],
        compiler_params=_cparams("arbitrary"),
        name="residual_ln",
    )(x, y, g, b)


def _log_sigmoid(x):
    return jnp.minimum(x, 0.0) - jnp.log1p(jnp.exp(-jnp.abs(x)))


def _lane_column(a, col):
    lane = lax.broadcasted_iota(jnp.int32, a.shape, 1)
    return jnp.sum(jnp.where(lane == col, a, 0.0), axis=1, keepdims=True)


def _mlstm_gate_math(ig_col, fg_col, ig_row, fg_row, m0_col, seg_shift):
    L = M_CHUNK
    t = lax.broadcasted_iota(jnp.int32, (L, L), 0)
    s = lax.broadcasted_iota(jnp.int32, (L, L), 1)
    lower, upper = s <= t, t <= s
    if seg_shift is not None:
        same = (t >> seg_shift) == (s >> seg_shift)
        lower, upper = lower & same, upper & same
    lf_col, lf_row = _log_sigmoid(fg_col), _log_sigmoid(fg_row)
    f_col = jnp.sum(jnp.where(lower, lf_row, 0.0), axis=1, keepdims=True)
    f_row = jnp.sum(jnp.where(upper, lf_col, 0.0), axis=0, keepdims=True)
    d = jnp.where(lower, f_col - f_row + ig_row, NEG_INF)
    inter = f_col + m0_col
    m_t = jnp.maximum(inter, jnp.max(d, axis=1, keepdims=True))
    w_intra = jnp.exp(d - m_t)
    w_inter = jnp.exp(inter - m_t)
    return f_col, f_row, m_t, w_intra, w_inter


def _mlstm_head_out(num, den, m_t, nw_row, mo):
    hid = num * (1.0 / jnp.maximum(jnp.abs(den), jnp.exp(-m_t)))
    mu = jnp.mean(hid, axis=-1, keepdims=True)
    c = hid - mu
    var = jnp.mean(c * c, axis=-1, keepdims=True)
    return c * lax.rsqrt(var + LN_EPS) * nw_row * jax.nn.sigmoid(mo)


def _mlstm_prompt_kernel(q_ref, k_ref, v_ref, mo_ref, g_ref, gt_ref, nw_ref, init_ref,
                         mout_ref, c_out_ref, n_out_ref, m_out_ref, c_scr, n_scr, m_scr):
    del init_ref
    hp = pl.program_id(0)
    c = pl.program_id(1)
    L = M_CHUNK
    HP = M_HEADS_PER_STEP
    dk = q_ref.shape[1] // HP
    dv = v_ref.shape[1] // HP

    @pl.when(c == 0)
    def _():
        c_scr[...] = jnp.zeros_like(c_scr)
        n_scr[...] = jnp.zeros_like(n_scr)
        m_scr[...] = jnp.zeros_like(m_scr)

    g = g_ref[...]
    for hh in range(HP):
        h = hp * HP + hh
        kcols = slice(hh * dk, (hh + 1) * dk)
        vcols = slice(hh * dv, (hh + 1) * dv)
        ig_col = _lane_column(g, h)
        fg_col = _lane_column(g, M_HEADS + h)
        ig_row = gt_ref[pl.ds(h, 1), :]
        fg_row = gt_ref[pl.ds(M_HEADS + h, 1), :]
        f_col, _, m_t, w_intra, w_inter = _mlstm_gate_math(ig_col, fg_col, ig_row, fg_row, m_scr[hh], None)

        q = q_ref[:, kcols]
        kf = k_ref[:, kcols] * (1.0 / math.sqrt(dk))
        qb, kb, vb = q.astype(BF16), kf.astype(BF16), v_ref[:, vcols].astype(BF16)
        c_prev = c_scr[hh]
        n_prev = n_scr[hh]
        s = _dot_nt(qb, kb) * w_intra
        num = w_inter * _dot(qb, c_prev.astype(BF16)) + _dot(s.astype(BF16), vb)
        den = w_inter * jnp.sum(q * n_prev, axis=1, keepdims=True) + jnp.sum(s, axis=1, keepdims=True)
        mout_ref[:, vcols] = _mlstm_head_out(num, den, m_t, nw_ref[:, vcols], mo_ref[:, vcols]).astype(mout_ref.dtype)

        m_end = m_t[L - 1:L, :]
        w_state = w_inter[L - 1:L, :]
        w_tok = jnp.exp(f_col[L - 1:L, :] - f_col + ig_col - m_end)
        kw = kf * w_tok
        c_scr[hh] = w_state * c_prev + _dot_tn(kw.astype(BF16), vb)
        n_scr[hh] = w_state * n_prev + jnp.sum(kw, axis=0, keepdims=True)
        m_scr[hh] = m_end

    @pl.when(c == pl.num_programs(1) - 1)
    def _():
        for hh in range(HP):
            c_out_ref[0, hh] = c_scr[hh]
            n_out_ref[hh] = n_scr[hh]
            m_out_ref[hh] = jnp.broadcast_to(m_scr[hh], (SUBLANES, LANES))


def _mlstm_prompt(z, g, gt, nw, s_len, t_all):
    dv = nw.shape[1] // M_HEADS
    dk = dv // 2
    L = M_CHUNK
    H = M_HEADS
    HP = M_HEADS_PER_STEP
    G = H // HP
    return pl.pallas_call(
        _mlstm_prompt_kernel,
        grid=(G, s_len // L),
        in_specs=[
            pl.BlockSpec((L, HP * dk), lambda h, c: (c, h)),
            pl.BlockSpec((L, HP * dk), lambda h, c: (c, G + h)),
            pl.BlockSpec((L, HP * dv), lambda h, c: (c, G + h)),
            pl.BlockSpec((L, HP * dv), lambda h, c: (c, 2 * G + h)),
            pl.BlockSpec((L, LANES), lambda h, c: (c, 0)),
            pl.BlockSpec((2 * H, L), lambda h, c: (0, c)),
            pl.BlockSpec((1, HP * dv), lambda h, c: (0, h)),
            pl.BlockSpec(memory_space=pl.ANY),
        ],
        out_specs=[
            pl.BlockSpec((L, HP * dv), lambda h, c: (c, h)),
            pl.BlockSpec((1, HP, dk, dv), lambda h, c: (0, h, 0, 0)),
            pl.BlockSpec((HP, 1, dk), lambda h, c: (h, 0, 0)),
            pl.BlockSpec((HP, SUBLANES, LANES), lambda h, c: (h, 0, 0)),
        ],
        out_shape=[
            jax.ShapeDtypeStruct((t_all, H * dv), BF16),
            jax.ShapeDtypeStruct((1, H, dk, dv), F32),
            jax.ShapeDtypeStruct((H, 1, dk), F32),
            jax.ShapeDtypeStruct((H, SUBLANES, LANES), F32),
        ],
        scratch_shapes=[pltpu.VMEM((HP, dk, dv), F32), pltpu.VMEM((HP, 1, dk), F32), pltpu.VMEM((HP, 1, 1), F32)],
        input_output_aliases={7: 0},
        compiler_params=_cparams("parallel", "arbitrary"),
        name="mlstm_prompt",
    )(z, z, z, z, g, gt, nw, jnp.zeros((t_all, H * dv), BF16))


def _mlstm_sample_kernel(q_ref, k_ref, v_ref, mo_ref, g_ref, gt_ref, nw_ref, m0_ref, c0_ref, n0_ref, alias_ref,
                         mout_ref, c_out_ref, n_out_ref, mt_out_ref,
                         num_scr, den_scr, kw_scr, vb_scr, qb_scr, wi_scr, mt_scr, *, seg_shift):
    del alias_ref
    h = pl.program_id(0)
    b = pl.program_id(1)
    L = M_CHUNK
    seg = 1 << seg_shift

    @pl.when(b == 0)
    def _():
        g = g_ref[...]
        ig_col = _lane_column(g, h)
        fg_col = _lane_column(g, M_HEADS + h)
        ig_row = gt_ref[pl.ds(h, 1), :]
        fg_row = gt_ref[pl.ds(M_HEADS + h, 1), :]
        m0_col = _lane_column(m0_ref[...], h)
        f_col, f_row, m_t, w_intra, w_inter = _mlstm_gate_math(ig_col, fg_col, ig_row, fg_row, m0_col, seg_shift)
        q = q_ref[...]
        kf = k_ref[...] * (1.0 / math.sqrt(q.shape[1]))
        qb, kb, vb = q.astype(BF16), kf.astype(BF16), v_ref[...].astype(BF16)
        s = _dot_nt(qb, kb) * w_intra
        num_scr[...] = _dot(s.astype(BF16), vb)
        den_scr[...] = jnp.sum(s, axis=1, keepdims=True)
        t = lax.broadcasted_iota(jnp.int32, (L, L), 0)
        sidx = lax.broadcasted_iota(jnp.int32, (L, L), 1)
        is_last = sidx == (t | (seg - 1))
        m_row = jnp.sum(jnp.where(t == sidx, m_t, 0.0), axis=0, keepdims=True)
        f_last = jnp.sum(jnp.where(is_last, f_row, 0.0), axis=1, keepdims=True)
        m_end = jnp.sum(jnp.where(is_last, m_row, 0.0), axis=1, keepdims=True)
        w_tok = jnp.exp(f_last - f_col + ig_col - m_end)
        kw_scr[...] = kf * w_tok
        vb_scr[...] = vb
        qb_scr[...] = qb
        wi_scr[...] = w_inter
        mt_scr[...] = m_t

    c_b = c0_ref[0, 0]
    n_b = n0_ref[0]
    row = lax.broadcasted_iota(jnp.int32, (L, 1), 0)
    in_b = (row >> seg_shift) == b
    wi = wi_scr[...]
    qc = _dot(qb_scr[...], c_b.astype(BF16))
    num_scr[...] += jnp.where(in_b, wi * qc, 0.0)
    qn = jnp.sum(q_ref[...] * n_b, axis=1, keepdims=True)
    den_scr[...] += jnp.where(in_b, wi * qn, 0.0)
    w_state = jnp.sum(jnp.where(row == seg * b + (seg - 1), wi, 0.0), axis=0, keepdims=True)
    kw_b = jnp.where(in_b, kw_scr[...], 0.0)
    c_out_ref[0, 0] = w_state * c_b + _dot_tn(kw_b.astype(BF16), vb_scr[...])
    n_out_ref[0] = w_state * n_b + jnp.sum(kw_b, axis=0, keepdims=True)

    @pl.when(b == pl.num_programs(1) - 1)
    def _():
        m_t = mt_scr[...]
        out = _mlstm_head_out(num_scr[...], den_scr[...], m_t, nw_ref[...], mo_ref[...])
        mout_ref[...] = out.astype(mout_ref.dtype)
        mt_out_ref[0] = jnp.broadcast_to(m_t, (L, LANES))


def _mlstm_sample(z, g, gt, nw, m0_rows, c0, n0, m_out, row0):
    dv = nw.shape[1] // M_HEADS
    dk = dv // 2
    L = M_CHUNK
    H = M_HEADS
    B = c0.shape[0]
    seg = L // B
    rb = row0 // L
    kern = functools.partial(_mlstm_sample_kernel, seg_shift=seg.bit_length() - 1)
    return pl.pallas_call(
        kern,
        grid=(H, B),
        in_specs=[
            pl.BlockSpec((L, dk), lambda h, b: (rb, h)),
            pl.BlockSpec((L, dk), lambda h, b: (rb, H + h)),
            pl.BlockSpec((L, dv), lambda h, b: (rb, H + h)),
            pl.BlockSpec((L, dv), lambda h, b: (rb, 2 * H + h)),
            pl.BlockSpec((L, LANES), lambda h, b: (rb, 0)),
            pl.BlockSpec((2 * H, L), lambda h, b: (0, rb)),
            pl.BlockSpec((1, dv), lambda h, b: (0, h)),
            pl.BlockSpec((L, LANES), lambda h, b: (0, 0)),
            pl.BlockSpec((1, 1, dk, dv), lambda h, b: (b, h, 0, 0)),
            pl.BlockSpec((1, 1, dk), lambda h, b: (b * H + h, 0, 0)),
            pl.BlockSpec(memory_space=pl.ANY),
        ],
        out_specs=[
            pl.BlockSpec((L, dv), lambda h, b: (rb, h)),
            pl.BlockSpec((1, 1, dk, dv), lambda h, b: (b, h, 0, 0)),
            pl.BlockSpec((1, 1, dk), lambda h, b: (b * H + h, 0, 0)),
            pl.BlockSpec((1, L, LANES), lambda h, b: (h, 0, 0)),
        ],
        out_shape=[
            jax.ShapeDtypeStruct(m_out.shape, m_out.dtype),
            jax.ShapeDtypeStruct(c0.shape, F32),
            jax.ShapeDtypeStruct(n0.shape, F32),
            jax.ShapeDtypeStruct((H, L, LANES), F32),
        ],
        scratch_shapes=[
            pltpu.VMEM((L, dv), F32), pltpu.VMEM((L, 1), F32), pltpu.VMEM((L, dk), F32), pltpu.VMEM((L, dv), BF16),
            pltpu.VMEM((L, dk), BF16), pltpu.VMEM((L, 1), F32), pltpu.VMEM((L, 1), F32),
        ],
        input_output_aliases={10: 0},
        compiler_params=_cparams("parallel", "arbitrary"),
        name="mlstm_sample",
    )(z, z, z, z, g, gt, nw, m0_rows, c0, n0, m_out)


A_SUPER = A_STEPS * max(A_DILATIONS)
ATT_SCALE = 1.0 / math.sqrt(A_DH)


def _rows(start, size, stride):
    return pl.ds(start, size) if stride == 1 else pl.ds(start, size, stride=stride)


def _group_softmax_merge(lses, outs):
    mx = functools.reduce(jnp.maximum, lses)
    ws = [jnp.exp(l - mx) for l in lses]
    tot = functools.reduce(lambda a, b: a + b, ws)
    acc = functools.reduce(lambda a, b: a + b, [w * o for w, o in zip(ws, outs)])
    return acc / tot


def _attn_prompt_kernel(sl_ref, *refs):
    in_refs = refs[:5 * A_GROUPS]
    a_ref, o_scr, l_scr = refs[5 * A_GROUPS + 1:]
    h = pl.program_id(0)
    n = pl.program_id(1)
    B = A_STEPS
    qi = lax.broadcasted_iota(jnp.int32, (B, B), 0)
    ki = lax.broadcasted_iota(jnp.int32, (B, B), 1)
    valid_prev, valid_cur = ki >= qi, ki <= qi
    step_prev = (qi - ki + B).astype(F32)
    step_cur = (qi - ki).astype(F32)

    ones = jnp.ones((2 * B, LANES), BF16)
    for g, dil in enumerate(A_DILATIONS):
        q_ref, kc_ref, kp_ref, vc_ref, vp_ref = in_refs[5 * g:5 * g + 5]
        slope = sl_ref[g, h] * float(dil)
        bias_cur = jnp.where(valid_cur, -slope * step_cur, NEG_INF)
        bias_prev = jnp.where(valid_prev, -slope * step_prev, NEG_INF)
        bias = jnp.concatenate([bias_prev, bias_cur], axis=1)
        bias_first = jnp.concatenate([jnp.where(n > 0, bias_prev, NEG_INF), bias_cur], axis=1)
        nblk = A_SUPER // (B * dil)
        for r in range(dil):
            for blk in range(nblk):
                start = r + dil * B * blk
                rows = _rows(start, B, dil)
                qb = q_ref[rows, :].astype(BF16)
                if blk > 0:
                    prow = _rows(start - dil * B, B, dil)
                    kp, vp, bs = kc_ref[prow, :], vc_ref[prow, :], bias
                else:
                    prow = _rows(r + dil * B * (nblk - 1), B, dil)
                    kp, vp, bs = kp_ref[prow, :], vp_ref[prow, :], bias_first
                kcat = jnp.concatenate([kp, kc_ref[rows, :]], axis=0).astype(BF16)
                vcat = jnp.concatenate([vp, vc_ref[rows, :]], axis=0).astype(BF16)
                s = _dot_nt(qb, kcat) * ATT_SCALE + bs
                mx = jnp.max(s, axis=1, keepdims=True)
                e = jnp.exp(s - mx).astype(BF16)
                pv = _dot(e, jnp.concatenate([vcat, ones], axis=1))
                den = pv[:, A_DH:]
                o_scr[g, rows, :] = pv[:, :A_DH] * (1.0 / den)
                l_scr[g, rows, :] = mx + jnp.log(den)

    def merge(i, carry):
        rows = pl.ds(pl.multiple_of(i * B, B), B)
        lses = [l_scr[g, rows, :] for g in range(A_GROUPS)]
        outs = [o_scr[g, rows, :] for g in range(A_GROUPS)]
        a_ref[rows, :] = _group_softmax_merge(lses, outs)
        return carry

    lax.fori_loop(0, A_SUPER // B, merge, 0)


def _attn_prompt(z, slopes, s_len, t_all, aq_blk, ak_blk, av_blk):
    R = A_SUPER
    in_specs = [pl.BlockSpec(memory_space=pltpu.SMEM)]
    args = [slopes]
    for g in range(A_GROUPS):
        off = g * A_HEADS
        cur = lambda h, n, base: (n, base + h)
        prev = lambda h, n, base: (jnp.maximum(n - 1, 0), base + h)
        for base, fn in ((aq_blk + off, cur), (ak_blk + off, cur), (ak_blk + off, prev), (av_blk + off, cur), (av_blk + off, prev)):
            in_specs.append(pl.BlockSpec((R, A_DH), functools.partial(fn, base=base)))
            args.append(z)
    in_specs.append(pl.BlockSpec(memory_space=pl.ANY))
    args.append(jnp.zeros((t_all, A_HEADS * A_DH), F32))
    return pl.pallas_call(
        _attn_prompt_kernel,
        grid=(A_HEADS, s_len // R),
        in_specs=in_specs,
        out_specs=pl.BlockSpec((R, A_DH), lambda h, n: (n, h)),
        out_shape=jax.ShapeDtypeStruct((t_all, A_HEADS * A_DH), F32),
        scratch_shapes=[pltpu.VMEM((A_GROUPS, R, A_DH), F32), pltpu.VMEM((A_GROUPS, R, LANES), F32)],
        input_output_aliases={len(args) - 1: 0},
        compiler_params=_cparams("parallel", "arbitrary"),
        name="attn_prompt",
    )(*args)


def _attn_sample_kernel(sl_ref, *refs, seq):
    G = A_GROUPS
    z_refs = refs[:3 * G]
    buf_refs = refs[3 * G:4 * G]
    new_refs = refs[4 * G:5 * G]
    a_ref = refs[5 * G + 1]
    cache_out_refs = refs[5 * G + 2:6 * G + 2]
    sems = refs[6 * G + 2]
    b = pl.program_id(0)
    half = b % (SUBLANES // seq)
    PADK = LANES

    rows_per_pos = 2 * A_HEADS
    copies = []
    for g in range(G):
        kept = buf_refs[g].shape[1] - seq * rows_per_pos
        copies.append(pltpu.make_async_copy(buf_refs[g].at[0, pl.ds(seq * rows_per_pos, kept), :],
                                            cache_out_refs[g].at[b, pl.ds(0, kept), :], sems.at[2 * g]))
        copies.append(pltpu.make_async_copy(new_refs[g].at[0],
                                            cache_out_refs[g].at[b, pl.ds(kept, seq * rows_per_pos), :],
                                            sems.at[2 * g + 1]))
    for cp in copies:
        cp.start()

    for h in range(A_HEADS):
        cols = slice(h * A_DH, (h + 1) * A_DH)
        lses, outs = [], []
        for g, dil in enumerate(A_DILATIONS):
            q_ref, kn_ref, vn_ref = z_refs[3 * g:3 * g + 3]
            buf_ref = buf_refs[g]
            lb = A_STEPS * dil
            slope = sl_ref[g, h]
            kbuf = buf_ref[0, pl.ds(h, lb, stride=2 * A_HEADS), :]
            vbuf = buf_ref[0, pl.ds(A_HEADS + h, lb, stride=2 * A_HEADS), :]
            pad = jnp.zeros((PADK - SUBLANES, A_DH), F32)
            kx = jnp.concatenate([kbuf, kn_ref[:, cols], pad], axis=0).astype(BF16)
            vx = jnp.concatenate([vbuf, vn_ref[:, cols], pad], axis=0).astype(BF16)
            s = _dot_nt(q_ref[:, cols].astype(BF16), kx) * ATT_SCALE
            col = lax.broadcasted_iota(jnp.int32, s.shape, 1)
            t = lax.broadcasted_iota(jnp.int32, s.shape, 0) - seq * half
            t_new = col - lb - seq * half
            in_buf = col < lb
            pos = jnp.where(in_buf, col, lb + t_new)
            delta = lb + t - pos
            valid = (in_buf | ((t_new >= 0) & (t_new < seq))) & (delta >= 0) & (delta <= A_STEPS * dil) \
                & ((delta & (dil - 1)) == 0)
            s = jnp.where(valid, s - slope * delta.astype(F32), NEG_INF)
            mx = jnp.max(s, axis=1, keepdims=True)
            e = jnp.exp(s - mx)
            den = jnp.sum(e, axis=1, keepdims=True)
            outs.append(_dot(e.astype(BF16), vx) * (1.0 / den))
            lses.append(mx + jnp.log(den))
        merged = _group_softmax_merge(lses, outs)
        for hv in range(SUBLANES // seq):
            @pl.when(half == hv)
            def _(merged=merged, hv=hv):
                a_ref[hv * seq:(hv + 1) * seq, cols] = merged[hv * seq:(hv + 1) * seq, :]

    for cp in copies:
        cp.wait()


def _attn_sample(z, slopes, bufs, kv_new, a_out, row0, aq_blk, ak_blk, av_blk, seq):
    B = bufs[0].shape[0]
    W = A_HEADS * A_DH
    per_blk = SUBLANES // seq
    rb0 = row0 // SUBLANES
    in_specs = [pl.BlockSpec(memory_space=pltpu.SMEM)]
    args = [slopes]
    for g in range(A_GROUPS):
        for base in (aq_blk + g, ak_blk + g, av_blk + g):
            in_specs.append(pl.BlockSpec((SUBLANES, W), functools.partial(lambda b, base: (rb0 + b // per_blk, base), base=base)))
            args.append(z)
    for arrs in (bufs, kv_new):
        for g in range(A_GROUPS):
            in_specs.append(pl.BlockSpec((1,) + arrs[g].shape[1:], lambda b: (b, 0, 0)))
            args.append(arrs[g])
    in_specs.append(pl.BlockSpec(memory_space=pl.ANY))
    args.append(a_out)
    outs = pl.pallas_call(
        functools.partial(_attn_sample_kernel, seq=seq),
        grid=(B,),
        in_specs=in_specs,
        out_specs=[pl.BlockSpec((SUBLANES, W), lambda b: (rb0 + b // per_blk, 0))]
        + [pl.BlockSpec(memory_space=pl.ANY)] * A_GROUPS,
        out_shape=[jax.ShapeDtypeStruct(a_out.shape, a_out.dtype)]
        + [jax.ShapeDtypeStruct(c.shape, c.dtype) for c in bufs],
        scratch_shapes=[pltpu.SemaphoreType.DMA((2 * A_GROUPS,))],
        input_output_aliases={len(args) - 1: 0},
        compiler_params=_cparams("arbitrary"),
        name="attn_sample",
    )(*args)
    return outs[0], outs[1:]


ST_TAU, ST_MAX0, ST_MAX1, ST_INVZ = 0, 1, 2, 3
ST_ROWS = SUBLANES
PEER_TOKEN_CHUNK = 256
PEER_KEY_ROWS = 32
PEER_KEY_GROUPS = PEER_KEY_ROWS // SUBLANES


def _take_max(sc, out, k):
    rows = lax.broadcasted_iota(jnp.int32, sc.shape, 0)
    out_rows = lax.broadcasted_iota(jnp.int32, out.shape, 0)
    m = jnp.max(sc, axis=0, keepdims=True)
    first = jnp.min(jnp.where(sc == m, rows, sc.shape[0]), axis=0, keepdims=True)
    return jnp.where(rows == first, NEG_INF, sc), jnp.where(out_rows == k, m, out)


def _pair_candidates(top0, top1):
    assert P_TOPK == 16 and SUBLANES == 8
    lo0, hi0, lo1, hi1 = top0[0:8], top0[8:16], top1[0:8], top1[8:16]
    skip = lax.broadcasted_iota(jnp.int32, lo0.shape, 0) < 4
    blocks = [top0[0:1] + lo1, top0[0:1] + hi1]
    blocks += [top0[i:i + 1] + lo1 for i in (1, 2, 3)]
    blocks += [jnp.where(skip, NEG_INF, lo0 + top1[0:1]), hi0 + top1[0:1]]
    blocks += [jnp.where(skip, NEG_INF, lo0 + top1[j:j + 1]) for j in (1, 2, 3)]
    return jnp.concatenate(blocks, axis=0)


N_PAIR_ROWS = 10 * SUBLANES


def _peer_scores_kernel(x_ref, wq_ref, sk_ref, sc_ref, st_ref, q_scr):
    tm = x_ref.shape[0]
    dkh = sk_ref.shape[2]
    q_scr[...] = _dot(x_ref[...], wq_ref[...]).astype(BF16)
    for hp in range(2 * P_HEADS):
        sc_ref[hp] = _dot_nt(sk_ref[hp], q_scr[:, hp * dkh:(hp + 1) * dkh])

    def per_chunk(ci, carry):
        lanes = pl.ds(pl.multiple_of(ci * LANES, LANES), LANES)
        empty = jnp.zeros((P_TOPK, LANES), F32)

        def per_head(h, carry):
            cand, max0, max1 = carry
            hh = jnp.minimum(h, P_HEADS - 1)
            sc0 = sc_ref[2 * hh, :, lanes]
            sc1 = sc_ref[2 * hh + 1, :, lanes]
            top0, top1, best = empty, empty, empty
            for k in range(P_TOPK):
                sc0, top0 = _take_max(sc0, top0, k)
                sc1, top1 = _take_max(sc1, top1, k)
                cand, best = _take_max(cand, best, k)
            z = jnp.sum(jnp.exp(best - best[0:1, :]), axis=0, keepdims=True)
            st_ref[jnp.maximum(h - 1, 0), :, lanes] = jnp.concatenate(
                [best[P_TOPK - 1:P_TOPK, :], max0, max1, 1.0 / z, jnp.zeros((ST_ROWS - 4, LANES), F32)], axis=0)
            return _pair_candidates(top0, top1), top0[0:1, :], top1[0:1, :]

        row = jnp.zeros((1, LANES), F32)
        lax.fori_loop(0, P_HEADS + 1, per_head, (jnp.zeros((N_PAIR_ROWS, LANES), F32), row, row))
        return carry

    lax.fori_loop(0, tm // LANES, per_chunk, 0)


def _peer_scores(xb, wq, sk, tm):
    T, D = xb.shape
    nq = wq.shape[1]
    return pl.pallas_call(
        _peer_scores_kernel,
        grid=(T // tm,),
        in_specs=[
            pl.BlockSpec((tm, D), lambda i: (i, 0)),
            pl.BlockSpec((D, nq), lambda i: (0, 0)),
            pl.BlockSpec(sk.shape, lambda i: (0, 0, 0)),
        ],
        out_specs=[
            pl.BlockSpec((2 * P_HEADS, P_NKEYS, tm), lambda i: (0, 0, i)),
            pl.BlockSpec((P_HEADS, ST_ROWS, tm), lambda i: (0, 0, i)),
        ],
        out_shape=[
            jax.ShapeDtypeStruct((2 * P_HEADS, P_NKEYS, T), F32),
            jax.ShapeDtypeStruct((P_HEADS, ST_ROWS, T), F32),
        ],
        scratch_shapes=[pltpu.VMEM((tm, nq), BF16)],
        compiler_params=_cparams("parallel"),
        name="peer_scores",
    )(xb, wq, sk)


def _gelu(a):
    return 0.5 * a * (1.0 + lax.erf(a * (1.0 / math.sqrt(2.0))))


def _peer_dense_kernel(x_ref, sc_ref, st_ref, u_ref, v_ref, y_ref,
                       e1_scr, tau_scr, row_scr, act_a, act_b, *at_scrs, n_e):
    e = pl.program_id(1)
    tm = x_ref.shape[0]
    te = u_ref.shape[0]
    K = P_NKEYS
    n_ig = te // K

    @pl.when(e == 0)
    def _():
        y_ref[...] = jnp.zeros_like(y_ref)
        act_b[...] = jnp.zeros_like(act_b)
        for h in range(P_HEADS):
            e1_scr[h] = jnp.exp(sc_ref[2 * h + 1] - st_ref[h, ST_MAX1:ST_MAX1 + 1, :])
            tau_scr[h] = jnp.broadcast_to(st_ref[h, ST_TAU:ST_TAU + 1, :], (SUBLANES, tm))

    chunks = [(c0, min(PEER_TOKEN_CHUNK, tm - c0)) for c0 in range(0, tm, PEER_TOKEN_CHUNK)]
    assert len(chunks) == len(at_scrs)
    d_model = v_ref.shape[1]
    n_parts = max(len(chunks) - 1, 1)
    part = d_model // n_parts

    def step(act_cur, act_prev):
        def pre_activations(k):
            c0, cw = chunks[k]
            at_scrs[k][...] = _dot_nt(u_ref[...], x_ref[c0:c0 + cw, :])

        def second_matmul(p):
            cols = slice(p * part, (p + 1) * part)
            y_ref[:, cols] += _dot_tn(act_prev[...], v_ref[:, cols])

        pre_activations(0)
        for p in range(n_parts):
            second_matmul(p)
            if p + 1 < len(chunks):
                pre_activations(p + 1)

        i0 = e * n_ig
        sub = lax.broadcasted_iota(jnp.int32, (SUBLANES, tm), 0)
        for ig in range(n_ig):
            i = i0 + ig
            i_grp = i // SUBLANES
            for h in range(P_HEADS):
                s0 = jnp.sum(jnp.where(sub == i - i_grp * SUBLANES, sc_ref[2 * h, i_grp], 0.0), axis=0, keepdims=True)
                w0 = jnp.exp(s0 - st_ref[h, ST_MAX0:ST_MAX0 + 1, :]) * st_ref[h, ST_INVZ:ST_INVZ + 1, :]
                row_scr[h, ig, 0] = jnp.broadcast_to(s0, (SUBLANES, tm))
                row_scr[h, ig, 1] = jnp.broadcast_to(w0, (SUBLANES, tm))

        for k, (c0, cw) in enumerate(chunks):
            at_scr = at_scrs[k]
            for l0 in range(c0, c0 + cw, LANES):
                lanes = slice(l0, l0 + LANES)
                at_lanes = slice(l0 - c0, l0 - c0 + LANES)
                for jq in range(K // PEER_KEY_ROWS):
                    jg = slice(jq * PEER_KEY_GROUPS, (jq + 1) * PEER_KEY_GROUPS)
                    gates = [jnp.zeros((PEER_KEY_GROUPS, SUBLANES, LANES), F32) for _ in range(n_ig)]
                    for h in range(P_HEADS):
                        s1 = sc_ref[2 * h + 1, jg, :, lanes]
                        e1 = e1_scr[h, jg, :, lanes]
                        tau = tau_scr[h, :, lanes]
                        for ig in range(n_ig):
                            total = s1 + row_scr[h, ig, 0, :, lanes]
                            gates[ig] = gates[ig] + jnp.where(total >= tau, e1 * row_scr[h, ig, 1, :, lanes], 0.0)
                    for ig in range(n_ig):
                        rows = slice(ig * K + jq * PEER_KEY_ROWS, ig * K + (jq + 1) * PEER_KEY_ROWS)
                        gate = gates[ig].reshape(PEER_KEY_ROWS, LANES)
                        act_cur[rows, lanes] = (_gelu(at_scr[rows, at_lanes]) * gate).astype(BF16)

    pl.when((e < n_e) & (e % 2 == 0))(lambda: step(act_a, act_b))
    pl.when((e < n_e) & (e % 2 == 1))(lambda: step(act_b, act_a))
    last_written = act_a if (n_e - 1) % 2 == 0 else act_b

    @pl.when(e == n_e)
    def _():
        y_ref[...] += _dot_tn(last_written[...], v_ref[...])


def _peer_dense(xb, sc, st, ub, vb, tm, te):
    T, D = xb.shape
    E = ub.shape[0]
    n_e = E // te
    once = pl.Buffered(1)
    return pl.pallas_call(
        functools.partial(_peer_dense_kernel, n_e=n_e),
        grid=(T // tm, n_e + 1),
        in_specs=[
            pl.BlockSpec((tm, D), lambda i, e: (i, 0), pipeline_mode=once),
            pl.BlockSpec((2 * P_HEADS, P_NKEYS // SUBLANES, SUBLANES, tm), lambda i, e: (0, 0, 0, i), pipeline_mode=once),
            pl.BlockSpec((P_HEADS, ST_ROWS, tm), lambda i, e: (0, 0, i)),
            pl.BlockSpec((te, D), lambda i, e: (jnp.minimum(e, n_e - 1), 0)),
            pl.BlockSpec((te, D), lambda i, e: (jnp.maximum(e - 1, 0), 0)),
        ],
        out_specs=pl.BlockSpec((tm, D), lambda i, e: (i, 0)),
        out_shape=jax.ShapeDtypeStruct((T, D), F32),
        scratch_shapes=[
            pltpu.VMEM((P_HEADS, P_NKEYS // SUBLANES, SUBLANES, tm), F32),
            pltpu.VMEM((P_HEADS, SUBLANES, tm), F32),
            pltpu.VMEM((P_HEADS, te // P_NKEYS, 2, SUBLANES, tm), F32),
            pltpu.VMEM((te, tm), BF16),
            pltpu.VMEM((te, tm), BF16),
        ] + [pltpu.VMEM((te, min(PEER_TOKEN_CHUNK, tm - c0)), F32) for c0 in range(0, tm, PEER_TOKEN_CHUNK)],
        compiler_params=_cparams("parallel", "arbitrary"),
        name="peer_dense",
    )(xb, sc.reshape(2 * P_HEADS, P_NKEYS // SUBLANES, SUBLANES, T), st, ub, vb)


def _tile(n, *candidates):
    for c in candidates:
        if n % c == 0:
            return c
    raise ValueError(f"no tile of {candidates} divides {n}")


def kernel(x_prompt, x_sample, state_C, state_n, state_m, cache_kv_w128, cache_kv_w512, cache_kv_w2048,
           w_in, b_in, mlstm_norm_w, w_branch_m, w_branch_a, w_out, ln1_g, ln1_b,
           w_query, sub_keys, expert_u, expert_v, ln2_g, ln2_b):
    bp, S, D = x_prompt.shape
    DB, DS, _ = x_sample.shape
    H = M_HEADS
    dv, dk = D // 8, D // 16
    AW = A_HEADS * A_DH
    caches = (cache_kv_w128, cache_kv_w512, cache_kv_w2048)
    assert bp == 1 and DB * DS == M_CHUNK and S % A_SUPER == 0
    assert all(c.shape[1] == A_STEPS * d for c, d in zip(caches, A_DILATIONS))
    T = S + DB * DS

    g_lo = 2 * H * dk + 2 * H * dv
    g_hi = g_lo + 2 * H
    w_in_t = w_in.T
    b_main = jnp.concatenate([b_in[:g_lo], b_in[g_hi:]])[None, :]
    w_gate_t = jnp.pad(w_in_t[g_lo:g_hi], ((0, LANES - 2 * H), (0, 0)))
    b_gate = jnp.pad(b_in[g_lo:g_hi], (0, LANES - 2 * H))[None, :]
    tm = _tile(T, 640, 128)
    x, xb, gates = _gates(x_prompt.reshape(S, D), x_sample.reshape(DB * DS, D), w_gate_t, b_gate, M_CHUNK)
    z = _proj(xb, w_in_t, b_main, _tile(T, 1040, 128), 1024, g_lo, g_hi)
    gates_t = gates[:, :2 * H].T
    aq_col, ak_col, av_col = g_lo, g_lo + A_GROUPS * AW, g_lo + 2 * A_GROUPS * AW
    gm_col = g_lo + 3 * A_GROUPS * AW
    ga_col = gm_col + D

    nw = mlstm_norm_w[None, :]
    m_out, c_p, n_p, m_p = _mlstm_prompt(z, gates, gates_t, nw, S, T)
    m0_rows = jnp.pad(jnp.repeat(state_m, DS, axis=0), ((0, 0), (0, LANES - H)))
    m_out, c_s, n_s, mt_s = _mlstm_sample(z, gates, gates_t, nw, m0_rows, state_C,
                                          state_n.reshape(DB * H, 1, dk), m_out, S)

    n_slopes = A_GROUPS * A_HEADS
    slopes = (2.0 ** (-8.0 * (jnp.arange(n_slopes, dtype=F32) + 1.0) / n_slopes)).reshape(A_GROUPS, A_HEADS)
    a_out = _attn_prompt(z, slopes, S, T, aq_col // A_DH, ak_col // A_DH, av_col // A_DH)
    bufs = [c.reshape(DB, c.shape[1] * 2 * A_HEADS, A_DH) for c in caches]
    kv_new = []
    for g in range(A_GROUPS):
        kg = z[S:, ak_col + g * AW:ak_col + (g + 1) * AW]
        vg = z[S:, av_col + g * AW:av_col + (g + 1) * AW]
        kv_new.append(jnp.stack([kg, vg], axis=1).reshape(DB, DS * 2 * A_HEADS, A_DH))
    a_out, kv_s = _attn_sample(z, slopes, bufs, kv_new, a_out, S, aq_col // AW, ak_col // AW, av_col // AW, DS)
    kv_s = [k.reshape(c.shape) for k, c in zip(kv_s, caches)]

    merged = _merge(m_out, a_out, w_branch_m.astype(BF16), w_branch_a.astype(BF16), z,
                    gm_col // 1024, ga_col // 1024, tm, 1024)
    x1, x1b = _outproj_ln(merged, w_out.astype(BF16), x, ln1_g[None, :], ln1_b[None, :], tm, 512)
    sk = sub_keys.reshape(2 * P_HEADS, P_NKEYS, sub_keys.shape[-1]).astype(BF16)
    sc, st = _peer_scores(x1b, w_query.astype(BF16), sk, tm)
    y = _peer_dense(x1b, sc, st, expert_u.astype(BF16), expert_v.astype(BF16), tm, 512)
    x2_prompt, x2_sample = _residual_ln(x1, y, ln2_g[None, :], ln2_b[None, :], M_CHUNK, S)

    y_prompt = x2_prompt.reshape(1, S, D)
    y_sample = x2_sample.reshape(DB, DS, D)
    kv_p = []
    for g, cache in enumerate(caches):
        win = cache.shape[1]
        kg = z[S - win:S, ak_col + g * AW:ak_col + (g + 1) * AW]
        vg = z[S - win:S, av_col + g * AW:av_col + (g + 1) * AW]
        kv_p.append(jnp.stack([kg, vg], axis=1).reshape(1, win, 2, A_HEADS, A_DH))
    m_s = mt_s[:, DS - 1::DS, 0].T
    return (y_prompt, y_sample, c_p, n_p.reshape(1, H, dk), m_p[:, 0, 0][None, :],
            kv_p[0], kv_p[1], kv_p[2],
            c_s, n_s.reshape(DB, H, dk), m_s, kv_s[0], kv_s[1], kv_s[2])
```

```python
import functools
import math

import jax
import jax.numpy as jnp
from jax import lax
from jax.experimental import pallas as pl
from jax.experimental.pallas import tpu as pltpu

F32 = jnp.float32
BF16 = jnp.bfloat16

M_HEADS = 8
M_CHUNK = 128
M_HEADS_PER_STEP = 4
A_DILATIONS = (1, 4, 16)
A_GROUPS = len(A_DILATIONS)
A_HEADS = 8
A_DH = 128
A_STEPS = 128
P_HEADS = 8
P_NKEYS = 128
P_TOPK = 16
DEPTH = 1
ALPHA = (2 * DEPTH) ** 0.25
LN_EPS = 1e-5

LANES = 128
SUBLANES = 8
VMEM_LIMIT = 58 * 1024 * 1024
NEG_INF = float("-inf")


def _cparams(*sem):
    return pltpu.CompilerParams(dimension_semantics=sem, vmem_limit_bytes=VMEM_LIMIT)


def _dot(a, b):
    return jnp.dot(a, b, preferred_element_type=F32)


def _dot_nt(a, b):
    return lax.dot_general(a, b, (((1,), (1,)), ((), ())), preferred_element_type=F32)


def _dot_tn(a, b):
    return lax.dot_general(a, b, (((0,), (0,)), ((), ())), preferred_element_type=F32)


def _proj_kernel(x_ref, wt_hbm, b_ref, o_ref, stage, w_scr, sem, *, first_shifted, skip):
    j = pl.program_id(0)
    i = pl.program_id(1)
    tn = stage.shape[0]

    def fetch(jj):
        row0 = pl.multiple_of(jj * tn + jnp.where(jj >= first_shifted, skip, 0), SUBLANES)
        return pltpu.make_async_copy(wt_hbm.at[pl.ds(row0, tn), :], stage, sem)

    @pl.when((i == 0) & (j == 0))
    def _():
        fetch(j).start()

    @pl.when(i == 0)
    def _():
        fetch(j).wait()
        w_scr[...] = stage[...].astype(BF16)

        @pl.when(j + 1 < pl.num_programs(0))
        def _():
            fetch(j + 1).start()

    o_ref[...] = _dot_nt(x_ref[...], w_scr[...]) + b_ref[...]


def _proj(xb, wt, b, tm, tn, skip_lo, skip_hi):
    T, K = xb.shape
    skip = skip_hi - skip_lo
    N = wt.shape[0] - skip
    assert skip_lo % tn == 0 and skip % SUBLANES == 0 and N % tn == 0 and b.shape == (1, N)
    return pl.pallas_call(
        functools.partial(_proj_kernel, first_shifted=skip_lo // tn, skip=skip),
        grid=(N // tn, T // tm),
        in_specs=[
            pl.BlockSpec((tm, K), lambda j, i: (i, 0)),
            pl.BlockSpec(memory_space=pl.ANY),
            pl.BlockSpec((1, tn), lambda j, i: (0, j)),
        ],
        out_specs=pl.BlockSpec((tm, tn), lambda j, i: (i, j)),
        out_shape=jax.ShapeDtypeStruct((T, N), F32),
        scratch_shapes=[pltpu.VMEM((tn, K), F32), pltpu.VMEM((tn, K), BF16), pltpu.SemaphoreType.DMA(())],
        compiler_params=_cparams("arbitrary", "arbitrary"),
        name="in_proj",
    )(xb, wt, b)


def _split_bf16(a):
    hi = a.astype(BF16)
    lo = (a - hi.astype(F32)).astype(BF16)
    return hi, lo


def _gates_kernel(xa_ref, xb_ref, w_ref, b_ref, x_ref, xh_ref, o_ref, *, n_head):
    x = jnp.where(pl.program_id(0) < n_head, xa_ref[...], xb_ref[...])
    x_ref[...] = x
    xh, xl = _split_bf16(x)
    xh_ref[...] = xh
    wh, wl = _split_bf16(w_ref[...])
    o_ref[...] = _dot_nt(xh, wh) + _dot_nt(xh, wl) + _dot_nt(xl, wh) + b_ref[...]


def _gates(x_first, x_second, wgt, bg, tm):
    S, K = x_first.shape
    R = x_second.shape[0]
    assert S % tm == 0 and R % tm == 0
    n_head = S // tm
    T = S + R
    return pl.pallas_call(
        functools.partial(_gates_kernel, n_head=n_head),
        grid=(T // tm,),
        in_specs=[
            pl.BlockSpec((tm, K), lambda i: (jnp.minimum(i, n_head - 1), 0)),
            pl.BlockSpec((tm, K), lambda i: (jnp.maximum(i - n_head, 0), 0)),
            pl.BlockSpec((LANES, K), lambda i: (0, 0)),
            pl.BlockSpec((1, LANES), lambda i: (0, 0)),
        ],
        out_specs=[
            pl.BlockSpec((tm, K), lambda i: (i, 0)),
            pl.BlockSpec((tm, K), lambda i: (i, 0)),
            pl.BlockSpec((tm, LANES), lambda i: (i, 0)),
        ],
        out_shape=[
            jax.ShapeDtypeStruct((T, K), F32),
            jax.ShapeDtypeStruct((T, K), BF16),
            jax.ShapeDtypeStruct((T, LANES), F32),
        ],
        compiler_params=_cparams("parallel"),
        name="gate_proj",
    )(x_first, x_second, wgt, bg)


def _merge_kernel(m_ref, a_ref, wm_ref, wa_ref, gm_ref, ga_ref, o_ref):
    bm = _dot(m_ref[...], wm_ref[...])
    ba = _dot(a_ref[...].astype(BF16), wa_ref[...])
    o_ref[...] = (jax.nn.sigmoid(gm_ref[...]) * bm + jax.nn.sigmoid(ga_ref[...]) * ba).astype(o_ref.dtype)


def _merge(m_out, a_out, wm, wa, z, gm_blk, ga_blk, tm, tn):
    T, Km = m_out.shape
    Ka = a_out.shape[1]
    N = wm.shape[1]
    return pl.pallas_call(
        _merge_kernel,
        grid=(N // tn, T // tm),
        in_specs=[
            pl.BlockSpec((tm, Km), lambda j, i: (i, 0)),
            pl.BlockSpec((tm, Ka), lambda j, i: (i, 0)),
            pl.BlockSpec((Km, tn), lambda j, i: (0, j)),
            pl.BlockSpec((Ka, tn), lambda j, i: (0, j)),
            pl.BlockSpec((tm, tn), lambda j, i: (i, gm_blk + j)),
            pl.BlockSpec((tm, tn), lambda j, i: (i, ga_blk + j)),
        ],
        out_specs=pl.BlockSpec((tm, tn), lambda j, i: (i, j)),
        out_shape=jax.ShapeDtypeStruct((T, N), BF16),
        compiler_params=_cparams("parallel", "parallel"),
        name="branch_merge",
    )(m_out, a_out, wm, wa, z, z)


LN_ROWS = 64


def _layer_norm_rows(v, g, b):
    mu = jnp.mean(v, axis=-1, keepdims=True)
    c = v - mu
    var = jnp.mean(c * c, axis=-1, keepdims=True)
    return c * lax.rsqrt(var + LN_EPS) * g + b


def _outproj_ln_kernel(a_ref, w_ref, x_ref, g_ref, b_ref, o_ref, ob_ref):
    k = pl.program_id(1)

    @pl.when(k == 0)
    def _():
        o_ref[...] = ALPHA * x_ref[...] + _dot(a_ref[...], w_ref[...])

    @pl.when(k > 0)
    def _():
        o_ref[...] += _dot(a_ref[...], w_ref[...])

    @pl.when(k == pl.num_programs(1) - 1)
    def _():
        def norm_rows(r, carry):
            rows = pl.ds(pl.multiple_of(r * LN_ROWS, LN_ROWS), LN_ROWS)
            y = _layer_norm_rows(o_ref[rows, :], g_ref[...], b_ref[...])
            o_ref[rows, :] = y
            ob_ref[rows, :] = y.astype(BF16)
            return carry

        lax.fori_loop(0, o_ref.shape[0] // LN_ROWS, norm_rows, 0)


def _outproj_ln(a, w, x, g, b, tm, tk):
    T, K = a.shape
    D = w.shape[1]
    return pl.pallas_call(
        _outproj_ln_kernel,
        grid=(T // tm, K // tk),
        in_specs=[
            pl.BlockSpec((tm, tk), lambda i, k: (i, k)),
            pl.BlockSpec((tk, D), lambda i, k: (k, 0)),
            pl.BlockSpec((tm, D), lambda i, k: (i, 0), pipeline_mode=pl.Buffered(1)),
            pl.BlockSpec((1, D), lambda i, k: (0, 0)),
            pl.BlockSpec((1, D), lambda i, k: (0, 0)),
        ],
        out_specs=[
            pl.BlockSpec((tm, D), lambda i, k: (i, 0)),
            pl.BlockSpec((tm, D), lambda i, k: (i, 0)),
        ],
        out_shape=[jax.ShapeDtypeStruct((T, D), F32), jax.ShapeDtypeStruct((T, D), BF16)],
        compiler_params=_cparams("parallel", "arbitrary"),
        name="out_proj_ln",
    )(a, w, x, g, b)


def _residual_ln_kernel(x_ref, y_ref, g_ref, b_ref, head_ref, tail_ref, *, n_head):
    i = pl.program_id(0)

    def emit(dst_ref):
        def norm_rows(r, carry):
            rows = pl.ds(pl.multiple_of(r * LN_ROWS, LN_ROWS), LN_ROWS)
            dst_ref[rows, :] = _layer_norm_rows(ALPHA * x_ref[rows, :] + y_ref[rows, :], g_ref[...], b_ref[...])
            return carry

        lax.fori_loop(0, x_ref.shape[0] // LN_ROWS, norm_rows, 0)

    pl.when(i < n_head)(lambda: emit(head_ref))
    pl.when(i >= n_head)(lambda: emit(tail_ref))


def _residual_ln(x, y, g, b, tm, n_first):
    T, D = x.shape
    assert n_first % tm == 0 and (T - n_first) % tm == 0
    n_head = n_first // tm
    return pl.pallas_call(
        functools.partial(_residual_ln_kernel, n_head=n_head),
        grid=(T // tm,),
        in_specs=[
            pl.BlockSpec((tm, D), lambda i: (i, 0)),
            pl.BlockSpec((tm, D), lambda i: (i, 0)),
            pl.BlockSpec((1, D), lambda i: (0, 0)),
            pl.BlockSpec((1, D), lambda i: (0, 0)),
        ],
        out_specs=[
            pl.BlockSpec((tm, D), lambda i: (jnp.minimum(i, n_head - 1), 0)),
            pl.BlockSpec((tm, D), lambda i: (jnp.maximum(i - n_head, 0), 0)),
        ],
        out_shape=[jax.ShapeDtypeStruct((n_first, D), F32), jax.ShapeDtypeStruct((T - n_first, D), F32)],
        compiler_params=_cparams("arbitrary"),
        name="residual_ln",
    )(x, y, g, b)


def _log_sigmoid(x):
    return jnp.minimum(x, 0.0) - jnp.log1p(jnp.exp(-jnp.abs(x)))


def _lane_column(a, col):
    lane = lax.broadcasted_iota(jnp.int32, a.shape, 1)
    return jnp.sum(jnp.where(lane == col, a, 0.0), axis=1, keepdims=True)


def _mlstm_gate_math(ig_col, fg_col, ig_row, fg_row, m0_col, seg_shift):
    L = M_CHUNK
    t = lax.broadcasted_iota(jnp.int32, (L, L), 0)
    s = lax.broadcasted_iota(jnp.int32, (L, L), 1)
    lower, upper = s <= t, t <= s
    if seg_shift is not None:
        same = (t >> seg_shift) == (s >> seg_shift)
        lower, upper = lower & same, upper & same
    lf_col, lf_row = _log_sigmoid(fg_col), _log_sigmoid(fg_row)
    f_col = jnp.sum(jnp.where(lower, lf_row, 0.0), axis=1, keepdims=True)
    f_row = jnp.sum(jnp.where(upper, lf_col, 0.0), axis=0, keepdims=True)
    d = jnp.where(lower, f_col - f_row + ig_row, NEG_INF)
    inter = f_col + m0_col
    m_t = jnp.maximum(inter, jnp.max(d, axis=1, keepdims=True))
    w_intra = jnp.exp(d - m_t)
    w_inter = jnp.exp(inter - m_t)
    return f_col, f_row, m_t, w_intra, w_inter


def _mlstm_head_out(num, den, m_t, nw_row, mo):
    hid = num * (1.0 / jnp.maximum(jnp.abs(den), jnp.exp(-m_t)))
    mu = jnp.mean(hid, axis=-1, keepdims=True)
    c = hid - mu
    var = jnp.mean(c * c, axis=-1, keepdims=True)
    return c * lax.rsqrt(var + LN_EPS) * nw_row * jax.nn.sigmoid(mo)


def _mlstm_prompt_kernel(q_ref, k_ref, v_ref, mo_ref, g_ref, gt_ref, nw_ref, init_ref,
                         mout_ref, c_out_ref, n_out_ref, m_out_ref, c_scr, n_scr, m_scr):
    del init_ref
    hp = pl.program_id(0)
    c = pl.program_id(1)
    L = M_CHUNK
    HP = M_HEADS_PER_STEP
    dk = q_ref.shape[1] // HP
    dv = v_ref.shape[1] // HP

    @pl.when(c == 0)
    def _():
        c_scr[...] = jnp.zeros_like(c_scr)
        n_scr[...] = jnp.zeros_like(n_scr)
        m_scr[...] = jnp.zeros_like(m_scr)

    g = g_ref[...]
    for hh in range(HP):
        h = hp * HP + hh
        kcols = slice(hh * dk, (hh + 1) * dk)
        vcols = slice(hh * dv, (hh + 1) * dv)
        ig_col = _lane_column(g, h)
        fg_col = _lane_column(g, M_HEADS + h)
        ig_row = gt_ref[pl.ds(h, 1), :]
        fg_row = gt_ref[pl.ds(M_HEADS + h, 1), :]
        f_col, _, m_t, w_intra, w_inter = _mlstm_gate_math(ig_col, fg_col, ig_row, fg_row, m_scr[hh], None)

        q = q_ref[:, kcols]
        kf = k_ref[:, kcols] * (1.0 / math.sqrt(dk))
        qb, kb, vb = q.astype(BF16), kf.astype(BF16), v_ref[:, vcols].astype(BF16)
        c_prev = c_scr[hh]
        n_prev = n_scr[hh]
        s = _dot_nt(qb, kb) * w_intra
        num = w_inter * _dot(qb, c_prev.astype(BF16)) + _dot(s.astype(BF16), vb)
        den = w_inter * jnp.sum(q * n_prev, axis=1, keepdims=True) + jnp.sum(s, axis=1, keepdims=True)
        mout_ref[:, vcols] = _mlstm_head_out(num, den, m_t, nw_ref[:, vcols], mo_ref[:, vcols]).astype(mout_ref.dtype)

        m_end = m_t[L - 1:L, :]
        w_state = w_inter[L - 1:L, :]
        w_tok = jnp.exp(f_col[L - 1:L, :] - f_col + ig_col - m_end)
        kw = kf * w_tok
        c_scr[hh] = w_state * c_prev + _dot_tn(kw.astype(BF16), vb)
        n_scr[hh] = w_state * n_prev + jnp.sum(kw, axis=0, keepdims=True)
        m_scr[hh] = m_end

    @pl.when(c == pl.num_programs(1) - 1)
    def _():
        for hh in range(HP):
            c_out_ref[0, hh] = c_scr[hh]
            n_out_ref[hh] = n_scr[hh]
            m_out_ref[hh] = jnp.broadcast_to(m_scr[hh], (SUBLANES, LANES))


def _mlstm_prompt(z, g, gt, nw, s_len, t_all):
    dv = nw.shape[1] // M_HEADS
    dk = dv // 2
    L = M_CHUNK
    H = M_HEADS
    HP = M_HEADS_PER_STEP
    G = H // HP
    return pl.pallas_call(
        _mlstm_prompt_kernel,
        grid=(G, s_len // L),
        in_specs=[
            pl.BlockSpec((L, HP * dk), lambda h, c: (c, h)),
            pl.BlockSpec((L, HP * dk), lambda h, c: (c, G + h)),
            pl.BlockSpec((L, HP * dv), lambda h, c: (c, G + h)),
            pl.BlockSpec((L, HP * dv), lambda h, c: (c, 2 * G + h)),
            pl.BlockSpec((L, LANES), lambda h, c: (c, 0)),
            pl.BlockSpec((2 * H, L), lambda h, c: (0, c)),
            pl.BlockSpec((1, HP * dv), lambda h, c: (0, h)),
            pl.BlockSpec(memory_space=pl.ANY),
        ],
        out_specs=[
            pl.BlockSpec((L, HP * dv), lambda h, c: (c, h)),
            pl.BlockSpec((1, HP, dk, dv), lambda h, c: (0, h, 0, 0)),
            pl.BlockSpec((HP, 1, dk), lambda h, c: (h, 0, 0)),
            pl.BlockSpec((HP, SUBLANES, LANES), lambda h, c: (h, 0, 0)),
        ],
        out_shape=[
            jax.ShapeDtypeStruct((t_all, H * dv), BF16),
            jax.ShapeDtypeStruct((1, H, dk, dv), F32),
            jax.ShapeDtypeStruct((H, 1, dk), F32),
            jax.ShapeDtypeStruct((H, SUBLANES, LANES), F32),
        ],
        scratch_shapes=[pltpu.VMEM((HP, dk, dv), F32), pltpu.VMEM((HP, 1, dk), F32), pltpu.VMEM((HP, 1, 1), F32)],
        input_output_aliases={7: 0},
        compiler_params=_cparams("parallel", "arbitrary"),
        name="mlstm_prompt",
    )(z, z, z, z, g, gt, nw, jnp.zeros((t_all, H * dv), BF16))


def _mlstm_sample_kernel(q_ref, k_ref, v_ref, mo_ref, g_ref, gt_ref, nw_ref, m0_ref, c0_ref, n0_ref, alias_ref,
                         mout_ref, c_out_ref, n_out_ref, mt_out_ref,
                         num_scr, den_scr, kw_scr, vb_scr, qb_scr, wi_scr, mt_scr, *, seg_shift):
    del alias_ref
    h = pl.program_id(0)
    b = pl.program_id(1)
    L = M_CHUNK
    seg = 1 << seg_shift

    @pl.when(b == 0)
    def _():
        g = g_ref[...]
        ig_col = _lane_column(g, h)
        fg_col = _lane_column(g, M_HEADS + h)
        ig_row = gt_ref[pl.ds(h, 1), :]
        fg_row = gt_ref[pl.ds(M_HEADS + h, 1), :]
        m0_col = _lane_column(m0_ref[...], h)
        f_col, f_row, m_t, w_intra, w_inter = _mlstm_gate_math(ig_col, fg_col, ig_row, fg_row, m0_col, seg_shift)
        q = q_ref[...]
        kf = k_ref[...] * (1.0 / math.sqrt(q.shape[1]))
        qb, kb, vb = q.astype(BF16), kf.astype(BF16), v_ref[...].astype(BF16)
        s = _dot_nt(qb, kb) * w_intra
        num_scr[...] = _dot(s.astype(BF16), vb)
        den_scr[...] = jnp.sum(s, axis=1, keepdims=True)
        t = lax.broadcasted_iota(jnp.int32, (L, L), 0)
        sidx = lax.broadcasted_iota(jnp.int32, (L, L), 1)
        is_last = sidx == (t | (seg - 1))
        m_row = jnp.sum(jnp.where(t == sidx, m_t, 0.0), axis=0, keepdims=True)
        f_last = jnp.sum(jnp.where(is_last, f_row, 0.0), axis=1, keepdims=True)
        m_end = jnp.sum(jnp.where(is_last, m_row, 0.0), axis=1, keepdims=True)
        w_tok = jnp.exp(f_last - f_col + ig_col - m_end)
        kw_scr[...] = kf * w_tok
        vb_scr[...] = vb
        qb_scr[...] = qb
        wi_scr[...] = w_inter
        mt_scr[...] = m_t

    c_b = c0_ref[0, 0]
    n_b = n0_ref[0]
    row = lax.broadcasted_iota(jnp.int32, (L, 1), 0)
    in_b = (row >> seg_shift) == b
    wi = wi_scr[...]
    qc = _dot(qb_scr[...], c_b.astype(BF16))
    num_scr[...] += jnp.where(in_b, wi * qc, 0.0)
    qn = jnp.sum(q_ref[...] * n_b, axis=1, keepdims=True)
    den_scr[...] += jnp.where(in_b, wi * qn, 0.0)
    w_state = jnp.sum(jnp.where(row == seg * b + (seg - 1), wi, 0.0), axis=0, keepdims=True)
    kw_b = jnp.where(in_b, kw_scr[...], 0.0)
    c_out_ref[0, 0] = w_state * c_b + _dot_tn(kw_b.astype(BF16), vb_scr[...])
    n_out_ref[0] = w_state * n_b + jnp.sum(kw_b, axis=0, keepdims=True)

    @pl.when(b == pl.num_programs(1) - 1)
    def _():
        m_t = mt_scr[...]
        out = _mlstm_head_out(num_scr[...], den_scr[...], m_t, nw_ref[...], mo_ref[...])
        mout_ref[...] = out.astype(mout_ref.dtype)
        mt_out_ref[0] = jnp.broadcast_to(m_t, (L, LANES))


def _mlstm_sample(z, g, gt, nw, m0_rows, c0, n0, m_out, row0):
    dv = nw.shape[1] // M_HEADS
    dk = dv // 2
    L = M_CHUNK
    H = M_HEADS
    B = c0.shape[0]
    seg = L // B
    rb = row0 // L
    kern = functools.partial(_mlstm_sample_kernel, seg_shift=seg.bit_length() - 1)
    return pl.pallas_call(
        kern,
        grid=(H, B),
        in_specs=[
            pl.BlockSpec((L, dk), lambda h, b: (rb, h)),
            pl.BlockSpec((L, dk), lambda h, b: (rb, H + h)),
            pl.BlockSpec((L, dv), lambda h, b: (rb, H + h)),
            pl.BlockSpec((L, dv), lambda h, b: (rb, 2 * H + h)),
            pl.BlockSpec((L, LANES), lambda h, b: (rb, 0)),
            pl.BlockSpec((2 * H, L), lambda h, b: (0, rb)),
            pl.BlockSpec((1, dv), lambda h, b: (0, h)),
            pl.BlockSpec((L, LANES), lambda h, b: (0, 0)),
            pl.BlockSpec((1, 1, dk, dv), lambda h, b: (b, h, 0, 0)),
            pl.BlockSpec((1, 1, dk), lambda h, b: (b * H + h, 0, 0)),
            pl.BlockSpec(memory_space=pl.ANY),
        ],
        out_specs=[
            pl.BlockSpec((L, dv), lambda h, b: (rb, h)),
            pl.BlockSpec((1, 1, dk, dv), lambda h, b: (b, h, 0, 0)),
            pl.BlockSpec((1, 1, dk), lambda h, b: (b * H + h, 0, 0)),
            pl.BlockSpec((1, L, LANES), lambda h, b: (h, 0, 0)),
        ],
        out_shape=[
            jax.ShapeDtypeStruct(m_out.shape, m_out.dtype),
            jax.ShapeDtypeStruct(c0.shape, F32),
            jax.ShapeDtypeStruct(n0.shape, F32),
            jax.ShapeDtypeStruct((H, L, LANES), F32),
        ],
        scratch_shapes=[
            pltpu.VMEM((L, dv), F32), pltpu.VMEM((L, 1), F32), pltpu.VMEM((L, dk), F32), pltpu.VMEM((L, dv), BF16),
            pltpu.VMEM((L, dk), BF16), pltpu.VMEM((L, 1), F32), pltpu.VMEM((L, 1), F32),
        ],
        input_output_aliases={10: 0},
        compiler_params=_cparams("parallel", "arbitrary"),
        name="mlstm_sample",
    )(z, z, z, z, g, gt, nw, m0_rows, c0, n0, m_out)


A_SUPER = A_STEPS * max(A_DILATIONS)
ATT_SCALE = 1.0 / math.sqrt(A_DH)


def _rows(start, size, stride):
    return pl.ds(start, size) if stride == 1 else pl.ds(start, size, stride=stride)


def _group_softmax_merge(lses, outs):
    mx = functools.reduce(jnp.maximum, lses)
    ws = [jnp.exp(l - mx) for l in lses]
    tot = functools.reduce(lambda a, b: a + b, ws)
    acc = functools.reduce(lambda a, b: a + b, [w * o for w, o in zip(ws, outs)])
    return acc / tot


def _attn_prompt_kernel(sl_ref, *refs):
    in_refs = refs[:5 * A_GROUPS]
    a_ref, o_scr, l_scr = refs[5 * A_GROUPS + 1:]
    h = pl.program_id(0)
    n = pl.program_id(1)
    B = A_STEPS
    qi = lax.broadcasted_iota(jnp.int32, (B, B), 0)
    ki = lax.broadcasted_iota(jnp.int32, (B, B), 1)
    valid_prev, valid_cur = ki >= qi, ki <= qi
    step_prev = (qi - ki + B).astype(F32)
    step_cur = (qi - ki).astype(F32)

    ones = jnp.ones((2 * B, LANES), BF16)
    for g, dil in enumerate(A_DILATIONS):
        q_ref, kc_ref, kp_ref, vc_ref, vp_ref = in_refs[5 * g:5 * g + 5]
        slope = sl_ref[g, h] * float(dil)
        bias_cur = jnp.where(valid_cur, -slope * step_cur, NEG_INF)
        bias_prev = jnp.where(valid_prev, -slope * step_prev, NEG_INF)
        bias = jnp.concatenate([bias_prev, bias_cur], axis=1)
        bias_first = jnp.concatenate([jnp.where(n > 0, bias_prev, NEG_INF), bias_cur], axis=1)
        nblk = A_SUPER // (B * dil)
        for r in range(dil):
            for blk in range(nblk):
                start = r + dil * B * blk
                rows = _rows(start, B, dil)
                qb = q_ref[rows, :].astype(BF16)
                if blk > 0:
                    prow = _rows(start - dil * B, B, dil)
                    kp, vp, bs = kc_ref[prow, :], vc_ref[prow, :], bias
                else:
                    prow = _rows(r + dil * B * (nblk - 1), B, dil)
                    kp, vp, bs = kp_ref[prow, :], vp_ref[prow, :], bias_first
                kcat = jnp.concatenate([kp, kc_ref[rows, :]], axis=0).astype(BF16)
                vcat = jnp.concatenate([vp, vc_ref[rows, :]], axis=0).astype(BF16)
                s = _dot_nt(qb, kcat) * ATT_SCALE + bs
                mx = jnp.max(s, axis=1, keepdims=True)
                e = jnp.exp(s - mx).astype(BF16)
                pv = _dot(e, jnp.concatenate([vcat, ones], axis=1))
                den = pv[:, A_DH:]
                o_scr[g, rows, :] = pv[:, :A_DH] * (1.0 / den)
                l_scr[g, rows, :] = mx + jnp.log(den)

    def merge(i, carry):
        rows = pl.ds(pl.multiple_of(i * B, B), B)
        lses = [l_scr[g, rows, :] for g in range(A_GROUPS)]
        outs = [o_scr[g, rows, :] for g in range(A_GROUPS)]
        a_ref[rows, :] = _group_softmax_merge(lses, outs)
        return carry

    lax.fori_loop(0, A_SUPER // B, merge, 0)


def _attn_prompt(z, slopes, s_len, t_all, aq_blk, ak_blk, av_blk):
    R = A_SUPER
    in_specs = [pl.BlockSpec(memory_space=pltpu.SMEM)]
    args = [slopes]
    for g in range(A_GROUPS):
        off = g * A_HEADS
        cur = lambda h, n, base: (n, base + h)
        prev = lambda h, n, base: (jnp.maximum(n - 1, 0), base + h)
        for base, fn in ((aq_blk + off, cur), (ak_blk + off, cur), (ak_blk + off, prev), (av_blk + off, cur), (av_blk + off, prev)):
            in_specs.append(pl.BlockSpec((R, A_DH), functools.partial(fn, base=base)))
            args.append(z)
    in_specs.append(pl.BlockSpec(memory_space=pl.ANY))
    args.append(jnp.zeros((t_all, A_HEADS * A_DH), F32))
    return pl.pallas_call(
        _attn_prompt_kernel,
        grid=(A_HEADS, s_len // R),
        in_specs=in_specs,
        out_specs=pl.BlockSpec((R, A_DH), lambda h, n: (n, h)),
        out_shape=jax.ShapeDtypeStruct((t_all, A_HEADS * A_DH), F32),
        scratch_shapes=[pltpu.VMEM((A_GROUPS, R, A_DH), F32), pltpu.VMEM((A_GROUPS, R, LANES), F32)],
        input_output_aliases={len(args) - 1: 0},
        compiler_params=_cparams("parallel", "arbitrary"),
        name="attn_prompt",
    )(*args)


def _attn_sample_kernel(sl_ref, *refs, seq):
    G = A_GROUPS
    z_refs = refs[:3 * G]
    buf_refs = refs[3 * G:4 * G]
    new_refs = refs[4 * G:5 * G]
    a_ref = refs[5 * G + 1]
    cache_out_refs = refs[5 * G + 2:6 * G + 2]
    sems = refs[6 * G + 2]
    b = pl.program_id(0)
    half = b % (SUBLANES // seq)
    PADK = LANES

    rows_per_pos = 2 * A_HEADS
    copies = []
    for g in range(G):
        kept = buf_refs[g].shape[1] - seq * rows_per_pos
        copies.append(pltpu.make_async_copy(buf_refs[g].at[0, pl.ds(seq * rows_per_pos, kept), :],
                                            cache_out_refs[g].at[b, pl.ds(0, kept), :], sems.at[2 * g]))
        copies.append(pltpu.make_async_copy(new_refs[g].at[0],
                                            cache_out_refs[g].at[b, pl.ds(kept, seq * rows_per_pos), :],
                                            sems.at[2 * g + 1]))
    for cp in copies:
        cp.start()

    for h in range(A_HEADS):
        cols = slice(h * A_DH, (h + 1) * A_DH)
        lses, outs = [], []
        for g, dil in enumerate(A_DILATIONS):
            q_ref, kn_ref, vn_ref = z_refs[3 * g:3 * g + 3]
            buf_ref = buf_refs[g]
            lb = A_STEPS * dil
            slope = sl_ref[g, h]
            kbuf = buf_ref[0, pl.ds(h, lb, stride=2 * A_HEADS), :]
            vbuf = buf_ref[0, pl.ds(A_HEADS + h, lb, stride=2 * A_HEADS), :]
            pad = jnp.zeros((PADK - SUBLANES, A_DH), F32)
            kx = jnp.concatenate([kbuf, kn_ref[:, cols], pad], axis=0).astype(BF16)
            vx = jnp.concatenate([vbuf, vn_ref[:, cols], pad], axis=0).astype(BF16)
            s = _dot_nt(q_ref[:, cols].astype(BF16), kx) * ATT_SCALE
            col = lax.broadcasted_iota(jnp.int32, s.shape, 1)
            t = lax.broadcasted_iota(jnp.int32, s.shape, 0) - seq * half
            t_new = col - lb - seq * half
            in_buf = col < lb
            pos = jnp.where(in_buf, col, lb + t_new)
            delta = lb + t - pos
            valid = (in_buf | ((t_new >= 0) & (t_new < seq))) & (delta >= 0) & (delta <= A_STEPS * dil) \
                & ((delta & (dil - 1)) == 0)
            s = jnp.where(valid, s - slope * delta.astype(F32), NEG_INF)
            mx = jnp.max(s, axis=1, keepdims=True)
            e = jnp.exp(s - mx)
            den = jnp.sum(e, axis=1, keepdims=True)
            outs.append(_dot(e.astype(BF16), vx) * (1.0 / den))
            lses.append(mx + jnp.log(den))
        merged = _group_softmax_merge(lses, outs)
        for hv in range(SUBLANES // seq):
            @pl.when(half == hv)
            def _(merged=merged, hv=hv):
                a_ref[hv * seq:(hv + 1) * seq, cols] = merged[hv * seq:(hv + 1) * seq, :]

    for cp in copies:
        cp.wait()


def _attn_sample(z, slopes, bufs, kv_new, a_out, row0, aq_blk, ak_blk, av_blk, seq):
    B = bufs[0].shape[0]
    W = A_HEADS * A_DH
    per_blk = SUBLANES // seq
    rb0 = row0 // SUBLANES
    in_specs = [pl.BlockSpec(memory_space=pltpu.SMEM)]
    args = [slopes]
    for g in range(A_GROUPS):
        for base in (aq_blk + g, ak_blk + g, av_blk + g):
            in_specs.append(pl.BlockSpec((SUBLANES, W), functools.partial(lambda b, base: (rb0 + b // per_blk, base), base=base)))
            args.append(z)
    for arrs in (bufs, kv_new):
        for g in range(A_GROUPS):
            in_specs.append(pl.BlockSpec((1,) + arrs[g].shape[1:], lambda b: (b, 0, 0)))
            args.append(arrs[g])
    in_specs.append(pl.BlockSpec(memory_space=pl.ANY))
    args.append(a_out)
    outs = pl.pallas_call(
        functools.partial(_attn_sample_kernel, seq=seq),
        grid=(B,),
        in_specs=in_specs,
        out_specs=[pl.BlockSpec((SUBLANES, W), lambda b: (rb0 + b // per_blk, 0))]
        + [pl.BlockSpec(memory_space=pl.ANY)] * A_GROUPS,
        out_shape=[jax.ShapeDtypeStruct(a_out.shape, a_out.dtype)]
        + [jax.ShapeDtypeStruct(c.shape, c.dtype) for c in bufs],
        scratch_shapes=[pltpu.SemaphoreType.DMA((2 * A_GROUPS,))],
        input_output_aliases={len(args) - 1: 0},
        compiler_params=_cparams("arbitrary"),
        name="attn_sample",
    )(*args)
    return outs[0], outs[1:]


ST_TAU, ST_MAX0, ST_MAX1, ST_INVZ = 0, 1, 2, 3
ST_ROWS = SUBLANES
PEER_TOKEN_CHUNK = 256
PEER_KEY_ROWS = 32
PEER_KEY_GROUPS = PEER_KEY_ROWS // SUBLANES


def _take_max(sc, out, k):
    rows = lax.broadcasted_iota(jnp.int32, sc.shape, 0)
    out_rows = lax.broadcasted_iota(jnp.int32, out.shape, 0)
    m = jnp.max(sc, axis=0, keepdims=True)
    first = jnp.min(jnp.where(sc == m, rows, sc.shape[0]), axis=0, keepdims=True)
    return jnp.where(rows == first, NEG_INF, sc), jnp.where(out_rows == k, m, out)


def _pair_candidates(top0, top1):
    assert P_TOPK == 16 and SUBLANES == 8
    lo0, hi0, lo1, hi1 = top0[0:8], top0[8:16], top1[0:8], top1[8:16]
    skip = lax.broadcasted_iota(jnp.int32, lo0.shape, 0) < 4
    blocks = [top0[0:1] + lo1, top0[0:1] + hi1]
    blocks += [top0[i:i + 1] + lo1 for i in (1, 2, 3)]
    blocks += [jnp.where(skip, NEG_INF, lo0 + top1[0:1]), hi0 + top1[0:1]]
    blocks += [jnp.where(skip, NEG_INF, lo0 + top1[j:j + 1]) for j in (1, 2, 3)]
    return jnp.concatenate(blocks, axis=0)


N_PAIR_ROWS = 10 * SUBLANES


def _peer_scores_kernel(x_ref, wq_ref, sk_ref, sc_ref, st_ref, q_scr):
    tm = x_ref.shape[0]
    dkh = sk_ref.shape[2]
    q_scr[...] = _dot(x_ref[...], wq_ref[...]).astype(BF16)
    for hp in range(2 * P_HEADS):
        sc_ref[hp] = _dot_nt(sk_ref[hp], q_scr[:, hp * dkh:(hp + 1) * dkh])

    def per_chunk(ci, carry):
        lanes = pl.ds(pl.multiple_of(ci * LANES, LANES), LANES)
        empty = jnp.zeros((P_TOPK, LANES), F32)

        def per_head(h, carry):
            cand, max0, max1 = carry
            hh = jnp.minimum(h, P_HEADS - 1)
            sc0 = sc_ref[2 * hh, :, lanes]
            sc1 = sc_ref[2 * hh + 1, :, lanes]
            top0, top1, best = empty, empty, empty
            for k in range(P_TOPK):
                sc0, top0 = _take_max(sc0, top0, k)
                sc1, top1 = _take_max(sc1, top1, k)
                cand, best = _take_max(cand, best, k)
            z = jnp.sum(jnp.exp(best - best[0:1, :]), axis=0, keepdims=True)
            st_ref[jnp.maximum(h - 1, 0), :, lanes] = jnp.concatenate(
                [best[P_TOPK - 1:P_TOPK, :], max0, max1, 1.0 / z, jnp.zeros((ST_ROWS - 4, LANES), F32)], axis=0)
            return _pair_candidates(top0, top1), top0[0:1, :], top1[0:1, :]

        row = jnp.zeros((1, LANES), F32)
        lax.fori_loop(0, P_HEADS + 1, per_head, (jnp.zeros((N_PAIR_ROWS, LANES), F32), row, row))
        return carry

    lax.fori_loop(0, tm // LANES, per_chunk, 0)


def _peer_scores(xb, wq, sk, tm):
    T, D = xb.shape
    nq = wq.shape[1]
    return pl.pallas_call(
        _peer_scores_kernel,
        grid=(T // tm,),
        in_specs=[
            pl.BlockSpec((tm, D), lambda i: (i, 0)),
            pl.BlockSpec((D, nq), lambda i: (0, 0)),
            pl.BlockSpec(sk.shape, lambda i: (0, 0, 0)),
        ],
        out_specs=[
            pl.BlockSpec((2 * P_HEADS, P_NKEYS, tm), lambda i: (0, 0, i)),
            pl.BlockSpec((P_HEADS, ST_ROWS, tm), lambda i: (0, 0, i)),
        ],
        out_shape=[
            jax.ShapeDtypeStruct((2 * P_HEADS, P_NKEYS, T), F32),
            jax.ShapeDtypeStruct((P_HEADS, ST_ROWS, T), F32),
        ],
        scratch_shapes=[pltpu.VMEM((tm, nq), BF16)],
        compiler_params=_cparams("parallel"),
        name="peer_scores",
    )(xb, wq, sk)


def _gelu(a):
    return 0.5 * a * (1.0 + lax.erf(a * (1.0 / math.sqrt(2.0))))


def _peer_dense_kernel(x_ref, sc_ref, st_ref, u_ref, v_ref, y_ref,
                       e1_scr, tau_scr, row_scr, act_a, act_b, *at_scrs, n_e):
    e = pl.program_id(1)
    tm = x_ref.shape[0]
    te = u_ref.shape[0]
    K = P_NKEYS
    n_ig = te // K

    @pl.when(e == 0)
    def _():
        y_ref[...] = jnp.zeros_like(y_ref)
        act_b[...] = jnp.zeros_like(act_b)
        for h in range(P_HEADS):
            e1_scr[h] = jnp.exp(sc_ref[2 * h + 1] - st_ref[h, ST_MAX1:ST_MAX1 + 1, :])
            tau_scr[h] = jnp.broadcast_to(st_ref[h, ST_TAU:ST_TAU + 1, :], (SUBLANES, tm))

    chunks = [(c0, min(PEER_TOKEN_CHUNK, tm - c0)) for c0 in range(0, tm, PEER_TOKEN_CHUNK)]
    assert len(chunks) == len(at_scrs)
    d_model = v_ref.shape[1]
    n_parts = max(len(chunks) - 1, 1)
    part = d_model // n_parts

    def step(act_cur, act_prev):
        def pre_activations(k):
            c0, cw = chunks[k]
            at_scrs[k][...] = _dot_nt(u_ref[...], x_ref[c0:c0 + cw, :])

        def second_matmul(p):
            cols = slice(p * part, (p + 1) * part)
            y_ref[:, cols] += _dot_tn(act_prev[...], v_ref[:, cols])

        pre_activations(0)
        for p in range(n_parts):
            second_matmul(p)
            if p + 1 < len(chunks):
                pre_activations(p + 1)

        i0 = e * n_ig
        sub = lax.broadcasted_iota(jnp.int32, (SUBLANES, tm), 0)
        for ig in range(n_ig):
            i = i0 + ig
            i_grp = i // SUBLANES
            for h in range(P_HEADS):
                s0 = jnp.sum(jnp.where(sub == i - i_grp * SUBLANES, sc_ref[2 * h, i_grp], 0.0), axis=0, keepdims=True)
                w0 = jnp.exp(s0 - st_ref[h, ST_MAX0:ST_MAX0 + 1, :]) * st_ref[h, ST_INVZ:ST_INVZ + 1, :]
                row_scr[h, ig, 0] = jnp.broadcast_to(s0, (SUBLANES, tm))
                row_scr[h, ig, 1] = jnp.broadcast_to(w0, (SUBLANES, tm))

        for k, (c0, cw) in enumerate(chunks):
            at_scr = at_scrs[k]
            for l0 in range(c0, c0 + cw, LANES):
                lanes = slice(l0, l0 + LANES)
                at_lanes = slice(l0 - c0, l0 - c0 + LANES)
                for jq in range(K // PEER_KEY_ROWS):
                    jg = slice(jq * PEER_KEY_GROUPS, (jq + 1) * PEER_KEY_GROUPS)
                    gates = [jnp.zeros((PEER_KEY_GROUPS, SUBLANES, LANES), F32) for _ in range(n_ig)]
                    for h in range(P_HEADS):
                        s1 = sc_ref[2 * h + 1, jg, :, lanes]
                        e1 = e1_scr[h, jg, :, lanes]
                        tau = tau_scr[h, :, lanes]
                        for ig in range(n_ig):
                            total = s1 + row_scr[h, ig, 0, :, lanes]
                            gates[ig] = gates[ig] + jnp.where(total >= tau, e1 * row_scr[h, ig, 1, :, lanes], 0.0)
                    for ig in range(n_ig):
                        rows = slice(ig * K + jq * PEER_KEY_ROWS, ig * K + (jq + 1) * PEER_KEY_ROWS)
                        gate = gates[ig].reshape(PEER_KEY_ROWS, LANES)
                        act_cur[rows, lanes] = (_gelu(at_scr[rows, at_lanes]) * gate).astype(BF16)

    pl.when((e < n_e) & (e % 2 == 0))(lambda: step(act_a, act_b))
    pl.when((e < n_e) & (e % 2 == 1))(lambda: step(act_b, act_a))
    last_written = act_a if (n_e - 1) % 2 == 0 else act_b

    @pl.when(e == n_e)
    def _():
        y_ref[...] += _dot_tn(last_written[...], v_ref[...])


def _peer_dense(xb, sc, st, ub, vb, tm, te):
    T, D = xb.shape
    E = ub.shape[0]
    n_e = E // te
    once = pl.Buffered(1)
    return pl.pallas_call(
        functools.partial(_peer_dense_kernel, n_e=n_e),
        grid=(T // tm, n_e + 1),
        in_specs=[
            pl.BlockSpec((tm, D), lambda i, e: (i, 0), pipeline_mode=once),
            pl.BlockSpec((2 * P_HEADS, P_NKEYS // SUBLANES, SUBLANES, tm), lambda i, e: (0, 0, 0, i), pipeline_mode=once),
            pl.BlockSpec((P_HEADS, ST_ROWS, tm), lambda i, e: (0, 0, i)),
            pl.BlockSpec((te, D), lambda i, e: (jnp.minimum(e, n_e - 1), 0)),
            pl.BlockSpec((te, D), lambda i, e: (jnp.maximum(e - 1, 0), 0)),
        ],
        out_specs=pl.BlockSpec((tm, D), lambda i, e: (i, 0)),
        out_shape=jax.ShapeDtypeStruct((T, D), F32),
        scratch_shapes=[
            pltpu.VMEM((P_HEADS, P_NKEYS // SUBLANES, SUBLANES, tm), F32),
            pltpu.VMEM((P_HEADS, SUBLANES, tm), F32),
            pltpu.VMEM((P_HEADS, te // P_NKEYS, 2, SUBLANES, tm), F32),
            pltpu.VMEM((te, tm), BF16),
            pltpu.VMEM((te, tm), BF16),
        ] + [pltpu.VMEM((te, min(PEER_TOKEN_CHUNK, tm - c0)), F32) for c0 in range(0, tm, PEER_TOKEN_CHUNK)],
        compiler_params=_cparams("parallel", "arbitrary"),
        name="peer_dense",
    )(xb, sc.reshape(2 * P_HEADS, P_NKEYS // SUBLANES, SUBLANES, T), st, ub, vb)


def _tile(n, *candidates):
    for c in candidates:
        if n % c == 0:
            return c
    raise ValueError(f"no tile of {candidates} divides {n}")


def kernel(x_prompt, x_sample, state_C, state_n, state_m, cache_kv_w128, cache_kv_w512, cache_kv_w2048,
           w_in, b_in, mlstm_norm_w, w_branch_m, w_branch_a, w_out, ln1_g, ln1_b,
           w_query, sub_keys, expert_u, expert_v, ln2_g, ln2_b):
    bp, S, D = x_prompt.shape
    DB, DS, _ = x_sample.shape
    H = M_HEADS
    dv, dk = D // 8, D // 16
    AW = A_HEADS * A_DH
    caches = (cache_kv_w128, cache_kv_w512, cache_kv_w2048)
    assert bp == 1 and DB * DS == M_CHUNK and S % A_SUPER == 0
    assert all(c.shape[1] == A_STEPS * d for c, d in zip(caches, A_DILATIONS))
    T = S + DB * DS

    g_lo = 2 * H * dk + 2 * H * dv
    g_hi = g_lo + 2 * H
    w_in_t = w_in.T
    b_main = jnp.concatenate([b_in[:g_lo], b_in[g_hi:]])[None, :]
    w_gate_t = jnp.pad(w_in_t[g_lo:g_hi], ((0, LANES - 2 * H), (0, 0)))
    b_gate = jnp.pad(b_in[g_lo:g_hi], (0, LANES - 2 * H))[None, :]
    tm = _tile(T, 640, 128)
    x, xb, gates = _gates(x_prompt.reshape(S, D), x_sample.reshape(DB * DS, D), w_gate_t, b_gate, M_CHUNK)
    z = _proj(xb, w_in_t, b_main, _tile(T, 1040, 128), 1024, g_lo, g_hi)
    gates_t = gates[:, :2 * H].T
    aq_col, ak_col, av_col = g_lo, g_lo + A_GROUPS * AW, g_lo + 2 * A_GROUPS * AW
    gm_col = g_lo + 3 * A_GROUPS * AW
    ga_col = gm_col + D

    nw = mlstm_norm_w[None, :]
    m_out, c_p, n_p, m_p = _mlstm_prompt(z, gates, gates_t, nw, S, T)
    m0_rows = jnp.pad(jnp.repeat(state_m, DS, axis=0), ((0, 0), (0, LANES - H)))
    m_out, c_s, n_s, mt_s = _mlstm_sample(z, gates, gates_t, nw, m0_rows, state_C,
                                          state_n.reshape(DB * H, 1, dk), m_out, S)

    n_slopes = A_GROUPS * A_HEADS
    slopes = (2.0 ** (-8.0 * (jnp.arange(n_slopes, dtype=F32) + 1.0) / n_slopes)).reshape(A_GROUPS, A_HEADS)
    a_out = _attn_prompt(z, slopes, S, T, aq_col // A_DH, ak_col // A_DH, av_col // A_DH)
    bufs = [c.reshape(DB, c.shape[1] * 2 * A_HEADS, A_DH) for c in caches]
    kv_new = []
    for g in range(A_GROUPS):
        kg = z[S:, ak_col + g * AW:ak_col + (g + 1) * AW]
        vg = z[S:, av_col + g * AW:av_col + (g + 1) * AW]
        kv_new.append(jnp.stack([kg, vg], axis=1).reshape(DB, DS * 2 * A_HEADS, A_DH))
    a_out, kv_s = _attn_sample(z, slopes, bufs, kv_new, a_out, S, aq_col // AW, ak_col // AW, av_col // AW, DS)
    kv_s = [k.reshape(c.shape) for k, c in zip(kv_s, caches)]

    merged = _merge(m_out, a_out, w_branch_m.astype(BF16), w_branch_a.astype(BF16), z,
                    gm_col // 1024, ga_col // 1024, tm, 1024)
    x1, x1b = _outproj_ln(merged, w_out.astype(BF16), x, ln1_g[None, :], ln1_b[None, :], tm, 512)
    sk = sub_keys.reshape(2 * P_HEADS, P_NKEYS, sub_keys.shape[-1]).astype(BF16)
    sc, st = _peer_scores(x1b, w_query.astype(BF16), sk, tm)
    y = _peer_dense(x1b, sc, st, expert_u.astype(BF16), expert_v.astype(BF16), tm, 512)
    x2_prompt, x2_sample = _residual_ln(x1, y, ln2_g[None, :], ln2_b[None, :], M_CHUNK, S)

    y_prompt = x2_prompt.reshape(1, S, D)
    y_sample = x2_sample.reshape(DB, DS, D)
    kv_p = []
    for g, cache in enumerate(caches):
        win = cache.shape[1]
        kg = z[S - win:S, ak_col + g * AW:ak_col + (g + 1) * AW]
        vg = z[S - win:S, av_col + g * AW:av_col + (g + 1) * AW]
        kv_p.append(jnp.stack([kg, vg], axis=1).reshape(1, win, 2, A_HEADS, A_DH))
    m_s = mt_s[:, DS - 1::DS, 0].T
    return (y_prompt, y_sample, c_p, n_p.reshape(1, H, dk), m_p[:, 0, 0][None, :],
            kv_p[0], kv_p[1], kv_p[2],
            c_s, n_s.reshape(DB, H, dk), m_s, kv_s[0], kv_s[1], kv_s[2])
```

```python
import functools
import math

import jax
import jax.numpy as jnp
from jax import lax
from jax.experimental import pallas as pl
from jax.experimental.pallas import tpu as pltpu

F32 = jnp.float32
BF16 = jnp.bfloat16

M_HEADS = 8
M_CHUNK = 128
M_HEADS_PER_STEP = 2
A_DILATIONS = (1, 4, 16)
A_GROUPS = len(A_DILATIONS)
A_HEADS = 8
A_DH = 128
A_STEPS = 128
P_HEADS = 8
P_NKEYS = 128
P_TOPK = 16
DEPTH = 1
ALPHA = (2 * DEPTH) ** 0.25
LN_EPS = 1e-5

LANES = 128
SUBLANES = 8
VMEM_LIMIT = 58 * 1024 * 1024
NEG_INF = float("-inf")


def _cparams(*sem):
    return pltpu.CompilerParams(dimension_semantics=sem, vmem_limit_bytes=VMEM_LIMIT)


def _dot(a, b):
    return jnp.dot(a, b, preferred_element_type=F32)


def _dot_nt(a, b):
    return lax.dot_general(a, b, (((1,), (1,)), ((), ())), preferred_element_type=F32)


def _dot_tn(a, b):
    return lax.dot_general(a, b, (((0,), (0,)), ((), ())), preferred_element_type=F32)


def _proj_kernel(x_ref, wt_hbm, b_ref, o_ref, stage, w_scr, sem, *, first_shifted, skip):
    j = pl.program_id(0)
    i = pl.program_id(1)
    tn = stage.shape[0]

    def fetch(jj):
        row0 = pl.multiple_of(jj * tn + jnp.where(jj >= first_shifted, skip, 0), SUBLANES)
        return pltpu.make_async_copy(wt_hbm.at[pl.ds(row0, tn), :], stage, sem)

    @pl.when((i == 0) & (j == 0))
    def _():
        fetch(j).start()

    @pl.when(i == 0)
    def _():
        fetch(j).wait()
        w_scr[...] = stage[...].astype(BF16)

        @pl.when(j + 1 < pl.num_programs(0))
        def _():
            fetch(j + 1).start(priority=1)

    o_ref[...] = _dot_nt(x_ref[...], w_scr[...]) + b_ref[...]


def _proj(xb, wt, b, tm, tn, skip_lo, skip_hi):
    T, K = xb.shape
    skip = skip_hi - skip_lo
    N = wt.shape[0] - skip
    assert skip_lo % tn == 0 and skip % SUBLANES == 0 and N % tn == 0 and b.shape == (1, N)
    return pl.pallas_call(
        functools.partial(_proj_kernel, first_shifted=skip_lo // tn, skip=skip),
        grid=(N // tn, T // tm),
        in_specs=[
            pl.BlockSpec((tm, K), lambda j, i: (i, 0)),
            pl.BlockSpec(memory_space=pl.ANY),
            pl.BlockSpec((1, tn), lambda j, i: (0, j)),
        ],
        out_specs=pl.BlockSpec((tm, tn), lambda j, i: (i, j)),
        out_shape=jax.ShapeDtypeStruct((T, N), F32),
        scratch_shapes=[pltpu.VMEM((tn, K), F32), pltpu.VMEM((tn, K), BF16), pltpu.SemaphoreType.DMA(())],
        compiler_params=_cparams("arbitrary", "arbitrary"),
        name="in_proj",
    )(xb, wt, b)


def _split_bf16(a):
    hi = a.astype(BF16)
    lo = (a - hi.astype(F32)).astype(BF16)
    return hi, lo


def _gates_kernel(xa_ref, xb_ref, w_ref, b_ref, x_ref, xh_ref, o_ref, *, n_head):
    x = jnp.where(pl.program_id(0) < n_head, xa_ref[...], xb_ref[...])
    x_ref[...] = x
    xh, xl = _split_bf16(x)
    xh_ref[...] = xh
    wh, wl = _split_bf16(w_ref[...])
    o_ref[...] = _dot_nt(xh, wh) + _dot_nt(xh, wl) + _dot_nt(xl, wh) + b_ref[...]


def _gates(x_first, x_second, wgt, bg, tm):
    S, K = x_first.shape
    R = x_second.shape[0]
    assert S % tm == 0 and R % tm == 0
    n_head = S // tm
    T = S + R
    return pl.pallas_call(
        functools.partial(_gates_kernel, n_head=n_head),
        grid=(T // tm,),
        in_specs=[
            pl.BlockSpec((tm, K), lambda i: (jnp.minimum(i, n_head - 1), 0)),
            pl.BlockSpec((tm, K), lambda i: (jnp.maximum(i - n_head, 0), 0)),
            pl.BlockSpec((LANES, K), lambda i: (0, 0)),
            pl.BlockSpec((1, LANES), lambda i: (0, 0)),
        ],
        out_specs=[
            pl.BlockSpec((tm, K), lambda i: (i, 0)),
            pl.BlockSpec((tm, K), lambda i: (i, 0)),
            pl.BlockSpec((tm, LANES), lambda i: (i, 0)),
        ],
        out_shape=[
            jax.ShapeDtypeStruct((T, K), F32),
            jax.ShapeDtypeStruct((T, K), BF16),
            jax.ShapeDtypeStruct((T, LANES), F32),
        ],
        compiler_params=_cparams("parallel"),
        name="gate_proj",
    )(x_first, x_second, wgt, bg)


def _merge_kernel(m_ref, a_ref, wm_ref, wa_ref, gm_ref, ga_ref, o_ref):
    bm = _dot(m_ref[...], wm_ref[...])
    ba = _dot(a_ref[...].astype(BF16), wa_ref[...])
    o_ref[...] = (jax.nn.sigmoid(gm_ref[...]) * bm + jax.nn.sigmoid(ga_ref[...]) * ba).astype(o_ref.dtype)


def _merge(m_out, a_out, wm, wa, z, gm_blk, ga_blk, tm, tn):
    T, Km = m_out.shape
    Ka = a_out.shape[1]
    N = wm.shape[1]
    return pl.pallas_call(
        _merge_kernel,
        grid=(N // tn, T // tm),
        in_specs=[
            pl.BlockSpec((tm, Km), lambda j, i: (i, 0)),
            pl.BlockSpec((tm, Ka), lambda j, i: (i, 0)),
            pl.BlockSpec((Km, tn), lambda j, i: (0, j)),
            pl.BlockSpec((Ka, tn), lambda j, i: (0, j)),
            pl.BlockSpec((tm, tn), lambda j, i: (i, gm_blk + j)),
            pl.BlockSpec((tm, tn), lambda j, i: (i, ga_blk + j)),
        ],
        out_specs=pl.BlockSpec((tm, tn), lambda j, i: (i, j)),
        out_shape=jax.ShapeDtypeStruct((T, N), BF16),
        compiler_params=_cparams("parallel", "parallel"),
        name="branch_merge",
    )(m_out, a_out, wm, wa, z, z)


LN_ROWS = 64


def _layer_norm_rows(v, g, b):
    mu = jnp.mean(v, axis=-1, keepdims=True)
    c = v - mu
    var = jnp.mean(c * c, axis=-1, keepdims=True)
    return c * lax.rsqrt(var + LN_EPS) * g + b


def _outproj_ln_kernel(a_ref, w_ref, x_ref, g_ref, b_ref, o_ref, ob_ref):
    k = pl.program_id(1)

    @pl.when(k == 0)
    def _():
        o_ref[...] = ALPHA * x_ref[...] + _dot(a_ref[...], w_ref[...])

    @pl.when(k > 0)
    def _():
        o_ref[...] += _dot(a_ref[...], w_ref[...])

    @pl.when(k == pl.num_programs(1) - 1)
    def _():
        def norm_rows(r, carry):
            rows = pl.ds(pl.multiple_of(r * LN_ROWS, LN_ROWS), LN_ROWS)
            y = _layer_norm_rows(o_ref[rows, :], g_ref[...], b_ref[...])
            o_ref[rows, :] = y
            ob_ref[rows, :] = y.astype(BF16)
            return carry

        lax.fori_loop(0, o_ref.shape[0] // LN_ROWS, norm_rows, 0)


def _outproj_ln(a, w, x, g, b, tm, tk):
    T, K = a.shape
    D = w.shape[1]
    return pl.pallas_call(
        _outproj_ln_kernel,
        grid=(T // tm, K // tk),
        in_specs=[
            pl.BlockSpec((tm, tk), lambda i, k: (i, k)),
            pl.BlockSpec((tk, D), lambda i, k: (k, 0)),
            pl.BlockSpec((tm, D), lambda i, k: (i, 0), pipeline_mode=pl.Buffered(1)),
            pl.BlockSpec((1, D), lambda i, k: (0, 0)),
            pl.BlockSpec((1, D), lambda i, k: (0, 0)),
        ],
        out_specs=[
            pl.BlockSpec((tm, D), lambda i, k: (i, 0)),
            pl.BlockSpec((tm, D), lambda i, k: (i, 0)),
        ],
        out_shape=[jax.ShapeDtypeStruct((T, D), F32), jax.ShapeDtypeStruct((T, D), BF16)],
        compiler_params=_cparams("parallel", "arbitrary"),
        name="out_proj_ln",
    )(a, w, x, g, b)


def _residual_ln_kernel(x_ref, y_ref, g_ref, b_ref, head_ref, tail_ref, *, n_head):
    i = pl.program_id(0)

    def emit(dst_ref):
        def norm_rows(r, carry):
            rows = pl.ds(pl.multiple_of(r * LN_ROWS, LN_ROWS), LN_ROWS)
            dst_ref[rows, :] = _layer_norm_rows(ALPHA * x_ref[rows, :] + y_ref[rows, :], g_ref[...], b_ref[...])
            return carry

        lax.fori_loop(0, x_ref.shape[0] // LN_ROWS, norm_rows, 0)

    pl.when(i < n_head)(lambda: emit(head_ref))
    pl.when(i >= n_head)(lambda: emit(tail_ref))


def _residual_ln(x, y, g, b, tm, n_first):
    T, D = x.shape
    assert n_first % tm == 0 and (T - n_first) % tm == 0
    n_head = n_first // tm
    return pl.pallas_call(
        functools.partial(_residual_ln_kernel, n_head=n_head),
        grid=(T // tm,),
        in_specs=[
            pl.BlockSpec((tm, D), lambda i: (i, 0)),
            pl.BlockSpec((tm, D), lambda i: (i, 0)),
            pl.BlockSpec((1, D), lambda i: (0, 0)),
            pl.BlockSpec((1, D), lambda i: (0, 0)),
        ],
        out_specs=[
            pl.BlockSpec((tm, D), lambda i: (jnp.minimum(i, n_head - 1), 0)),
            pl.BlockSpec((tm, D), lambda i: (jnp.maximum(i - n_head, 0), 0)),
        ],
        out_shape=[jax.ShapeDtypeStruct((n_first, D), F32), jax.ShapeDtypeStruct((T - n_first, D), F32)],
        compiler_params=_cparams("arbitrary"),
        name="residual_ln",
    )(x, y, g, b)


def _log_sigmoid(x):
    return jnp.minimum(x, 0.0) - jnp.log1p(jnp.exp(-jnp.abs(x)))


def _lane_column(a, col):
    lane = lax.broadcasted_iota(jnp.int32, a.shape, 1)
    return jnp.sum(jnp.where(lane == col, a, 0.0), axis=1, keepdims=True)


def _mlstm_gate_math(ig_col, fg_col, ig_row, fg_row, m0_col, seg_shift):
    L = M_CHUNK
    t = lax.broadcasted_iota(jnp.int32, (L, L), 0)
    s = lax.broadcasted_iota(jnp.int32, (L, L), 1)
    lower, upper = s <= t, t <= s
    if seg_shift is not None:
        same = (t >> seg_shift) == (s >> seg_shift)
        lower, upper = lower & same, upper & same
    lf_col, lf_row = _log_sigmoid(fg_col), _log_sigmoid(fg_row)
    f_col = jnp.sum(jnp.where(lower, lf_row, 0.0), axis=1, keepdims=True)
    f_row = jnp.sum(jnp.where(upper, lf_col, 0.0), axis=0, keepdims=True)
    d = jnp.where(lower, f_col - f_row + ig_row, NEG_INF)
    inter = f_col + m0_col
    m_t = jnp.maximum(inter, jnp.max(d, axis=1, keepdims=True))
    w_intra = jnp.exp(d - m_t)
    w_inter = jnp.exp(inter - m_t)
    return f_col, f_row, m_t, w_intra, w_inter


def _mlstm_head_out(num, den, m_t, nw_row, mo):
    hid = num * (1.0 / jnp.maximum(jnp.abs(den), jnp.exp(-m_t)))
    mu = jnp.mean(hid, axis=-1, keepdims=True)
    c = hid - mu
    var = jnp.mean(c * c, axis=-1, keepdims=True)
    return c * lax.rsqrt(var + LN_EPS) * nw_row * jax.nn.sigmoid(mo)


def _mlstm_prompt_kernel(q_ref, k_ref, v_ref, mo_ref, g_ref, gt_ref, nw_ref, init_ref,
                         mout_ref, c_out_ref, n_out_ref, m_out_ref, c_scr, n_scr, m_scr):
    del init_ref
    hp = pl.program_id(0)
    c = pl.program_id(1)
    L = M_CHUNK
    HP = M_HEADS_PER_STEP
    dk = q_ref.shape[1] // HP
    dv = v_ref.shape[1] // HP

    @pl.when(c == 0)
    def _():
        c_scr[...] = jnp.zeros_like(c_scr)
        n_scr[...] = jnp.zeros_like(n_scr)
        m_scr[...] = jnp.zeros_like(m_scr)

    g = g_ref[...]
    for hh in range(HP):
        h = hp * HP + hh
        kcols = slice(hh * dk, (hh + 1) * dk)
        vcols = slice(hh * dv, (hh + 1) * dv)
        ig_col = _lane_column(g, h)
        fg_col = _lane_column(g, M_HEADS + h)
        ig_row = gt_ref[pl.ds(h, 1), :]
        fg_row = gt_ref[pl.ds(M_HEADS + h, 1), :]
        f_col, _, m_t, w_intra, w_inter = _mlstm_gate_math(ig_col, fg_col, ig_row, fg_row, m_scr[hh], None)

        q = q_ref[:, kcols]
        kf = k_ref[:, kcols] * (1.0 / math.sqrt(dk))
        qb, kb, vb = q.astype(BF16), kf.astype(BF16), v_ref[:, vcols].astype(BF16)
        c_prev = c_scr[hh]
        n_prev = n_scr[hh]
        s = _dot_nt(qb, kb) * w_intra
        num = w_inter * _dot(qb, c_prev.astype(BF16)) + _dot(s.astype(BF16), vb)
        den = w_inter * jnp.sum(q * n_prev, axis=1, keepdims=True) + jnp.sum(s, axis=1, keepdims=True)
        mout_ref[:, vcols] = _mlstm_head_out(num, den, m_t, nw_ref[:, vcols], mo_ref[:, vcols]).astype(mout_ref.dtype)

        m_end = m_t[L - 1:L, :]
        w_state = w_inter[L - 1:L, :]
        w_tok = jnp.exp(f_col[L - 1:L, :] - f_col + ig_col - m_end)
        kw = kf * w_tok
        c_scr[hh] = w_state * c_prev + _dot_tn(kw.astype(BF16), vb)
        n_scr[hh] = w_state * n_prev + jnp.sum(kw, axis=0, keepdims=True)
        m_scr[hh] = m_end

    @pl.when(c == pl.num_programs(1) - 1)
    def _():
        for hh in range(HP):
            c_out_ref[0, hh] = c_scr[hh]
            n_out_ref[hh] = n_scr[hh]
            m_out_ref[hh] = jnp.broadcast_to(m_scr[hh], (SUBLANES, LANES))


def _mlstm_prompt(z, g, gt, nw, s_len, t_all):
    dv = nw.shape[1] // M_HEADS
    dk = dv // 2
    L = M_CHUNK
    H = M_HEADS
    HP = M_HEADS_PER_STEP
    G = H // HP
    return pl.pallas_call(
        _mlstm_prompt_kernel,
        grid=(G, s_len // L),
        in_specs=[
            pl.BlockSpec((L, HP * dk), lambda h, c: (c, h)),
            pl.BlockSpec((L, HP * dk), lambda h, c: (c, G + h)),
            pl.BlockSpec((L, HP * dv), lambda h, c: (c, G + h)),
            pl.BlockSpec((L, HP * dv), lambda h, c: (c, 2 * G + h)),
            pl.BlockSpec((L, LANES), lambda h, c: (c, 0)),
            pl.BlockSpec((2 * H, L), lambda h, c: (0, c)),
            pl.BlockSpec((1, HP * dv), lambda h, c: (0, h)),
            pl.BlockSpec(memory_space=pl.ANY),
        ],
        out_specs=[
            pl.BlockSpec((L, HP * dv), lambda h, c: (c, h)),
            pl.BlockSpec((1, HP, dk, dv), lambda h, c: (0, h, 0, 0)),
            pl.BlockSpec((HP, 1, dk), lambda h, c: (h, 0, 0)),
            pl.BlockSpec((HP, SUBLANES, LANES), lambda h, c: (h, 0, 0)),
        ],
        out_shape=[
            jax.ShapeDtypeStruct((t_all, H * dv), BF16),
            jax.ShapeDtypeStruct((1, H, dk, dv), F32),
            jax.ShapeDtypeStruct((H, 1, dk), F32),
            jax.ShapeDtypeStruct((H, SUBLANES, LANES), F32),
        ],
        scratch_shapes=[pltpu.VMEM((HP, dk, dv), F32), pltpu.VMEM((HP, 1, dk), F32), pltpu.VMEM((HP, 1, 1), F32)],
        input_output_aliases={7: 0},
        compiler_params=_cparams("parallel", "arbitrary"),
        name="mlstm_prompt",
    )(z, z, z, z, g, gt, nw, jnp.zeros((t_all, H * dv), BF16))


def _mlstm_sample_kernel(q_ref, k_ref, v_ref, mo_ref, g_ref, gt_ref, nw_ref, m0_ref, c0_ref, n0_ref, alias_ref,
                         mout_ref, c_out_ref, n_out_ref, mt_out_ref,
                         num_scr, den_scr, kw_scr, vb_scr, qb_scr, wi_scr, mt_scr, *, seg_shift):
    del alias_ref
    h = pl.program_id(0)
    b = pl.program_id(1)
    L = M_CHUNK
    seg = 1 << seg_shift

    @pl.when(b == 0)
    def _():
        g = g_ref[...]
        ig_col = _lane_column(g, h)
        fg_col = _lane_column(g, M_HEADS + h)
        ig_row = gt_ref[pl.ds(h, 1), :]
        fg_row = gt_ref[pl.ds(M_HEADS + h, 1), :]
        m0_col = _lane_column(m0_ref[...], h)
        f_col, f_row, m_t, w_intra, w_inter = _mlstm_gate_math(ig_col, fg_col, ig_row, fg_row, m0_col, seg_shift)
        q = q_ref[...]
        kf = k_ref[...] * (1.0 / math.sqrt(q.shape[1]))
        qb, kb, vb = q.astype(BF16), kf.astype(BF16), v_ref[...].astype(BF16)
        s = _dot_nt(qb, kb) * w_intra
        num_scr[...] = _dot(s.astype(BF16), vb)
        den_scr[...] = jnp.sum(s, axis=1, keepdims=True)
        t = lax.broadcasted_iota(jnp.int32, (L, L), 0)
        sidx = lax.broadcasted_iota(jnp.int32, (L, L), 1)
        is_last = sidx == (t | (seg - 1))
        m_row = jnp.sum(jnp.where(t == sidx, m_t, 0.0), axis=0, keepdims=True)
        f_last = jnp.sum(jnp.where(is_last, f_row, 0.0), axis=1, keepdims=True)
        m_end = jnp.sum(jnp.where(is_last, m_row, 0.0), axis=1, keepdims=True)
        w_tok = jnp.exp(f_last - f_col + ig_col - m_end)
        kw_scr[...] = kf * w_tok
        vb_scr[...] = vb
        qb_scr[...] = qb
        wi_scr[...] = w_inter
        mt_scr[...] = m_t

    c_b = c0_ref[0, 0]
    n_b = n0_ref[0]
    row = lax.broadcasted_iota(jnp.int32, (L, 1), 0)
    in_b = (row >> seg_shift) == b
    wi = wi_scr[...]
    qc = _dot(qb_scr[...], c_b.astype(BF16))
    num_scr[...] += jnp.where(in_b, wi * qc, 0.0)
    qn = jnp.sum(q_ref[...] * n_b, axis=1, keepdims=True)
    den_scr[...] += jnp.where(in_b, wi * qn, 0.0)
    w_state = jnp.sum(jnp.where(row == seg * b + (seg - 1), wi, 0.0), axis=0, keepdims=True)
    kw_b = jnp.where(in_b, kw_scr[...], 0.0)
    c_out_ref[0, 0] = w_state * c_b + _dot_tn(kw_b.astype(BF16), vb_scr[...])
    n_out_ref[0] = w_state * n_b + jnp.sum(kw_b, axis=0, keepdims=True)

    @pl.when(b == pl.num_programs(1) - 1)
    def _():
        m_t = mt_scr[...]
        out = _mlstm_head_out(num_scr[...], den_scr[...], m_t, nw_ref[...], mo_ref[...])
        mout_ref[...] = out.astype(mout_ref.dtype)
        mt_out_ref[0] = jnp.broadcast_to(m_t, (L, LANES))


def _mlstm_sample(z, g, gt, nw, m0_rows, c0, n0, m_out, row0):
    dv = nw.shape[1] // M_HEADS
    dk = dv // 2
    L = M_CHUNK
    H = M_HEADS
    B = c0.shape[0]
    seg = L // B
    rb = row0 // L
    kern = functools.partial(_mlstm_sample_kernel, seg_shift=seg.bit_length() - 1)
    return pl.pallas_call(
        kern,
        grid=(H, B),
        in_specs=[
            pl.BlockSpec((L, dk), lambda h, b: (rb, h)),
            pl.BlockSpec((L, dk), lambda h, b: (rb, H + h)),
            pl.BlockSpec((L, dv), lambda h, b: (rb, H + h)),
            pl.BlockSpec((L, dv), lambda h, b: (rb, 2 * H + h)),
            pl.BlockSpec((L, LANES), lambda h, b: (rb, 0)),
            pl.BlockSpec((2 * H, L), lambda h, b: (0, rb)),
            pl.BlockSpec((1, dv), lambda h, b: (0, h)),
            pl.BlockSpec((L, LANES), lambda h, b: (0, 0)),
            pl.BlockSpec((1, 1, dk, dv), lambda h, b: (b, h, 0, 0)),
            pl.BlockSpec((1, 1, dk), lambda h, b: (b * H + h, 0, 0)),
            pl.BlockSpec(memory_space=pl.ANY),
        ],
        out_specs=[
            pl.BlockSpec((L, dv), lambda h, b: (rb, h)),
            pl.BlockSpec((1, 1, dk, dv), lambda h, b: (b, h, 0, 0)),
            pl.BlockSpec((1, 1, dk), lambda h, b: (b * H + h, 0, 0)),
            pl.BlockSpec((1, L, LANES), lambda h, b: (h, 0, 0)),
        ],
        out_shape=[
            jax.ShapeDtypeStruct(m_out.shape, m_out.dtype),
            jax.ShapeDtypeStruct(c0.shape, F32),
            jax.ShapeDtypeStruct(n0.shape, F32),
            jax.ShapeDtypeStruct((H, L, LANES), F32),
        ],
        scratch_shapes=[
            pltpu.VMEM((L, dv), F32), pltpu.VMEM((L, 1), F32), pltpu.VMEM((L, dk), F32), pltpu.VMEM((L, dv), BF16),
            pltpu.VMEM((L, dk), BF16), pltpu.VMEM((L, 1), F32), pltpu.VMEM((L, 1), F32),
        ],
        input_output_aliases={10: 0},
        compiler_params=_cparams("parallel", "arbitrary"),
        name="mlstm_sample",
    )(z, z, z, z, g, gt, nw, m0_rows, c0, n0, m_out)


A_SUPER = A_STEPS * max(A_DILATIONS)
ATT_SCALE = 1.0 / math.sqrt(A_DH)


def _rows(start, size, stride):
    return pl.ds(start, size) if stride == 1 else pl.ds(start, size, stride=stride)


def _group_softmax_merge(lses, outs):
    mx = functools.reduce(jnp.maximum, lses)
    ws = [jnp.exp(l - mx) for l in lses]
    tot = functools.reduce(lambda a, b: a + b, ws)
    acc = functools.reduce(lambda a, b: a + b, [w * o for w, o in zip(ws, outs)])
    return acc / tot


def _attn_prompt_kernel(sl_ref, *refs):
    in_refs = refs[:5 * A_GROUPS]
    a_ref, o_scr, l_scr = refs[5 * A_GROUPS + 1:]
    h = pl.program_id(0)
    n = pl.program_id(1)
    B = A_STEPS
    qi = lax.broadcasted_iota(jnp.int32, (B, B), 0)
    ki = lax.broadcasted_iota(jnp.int32, (B, B), 1)
    valid_prev, valid_cur = ki >= qi, ki <= qi
    step_prev = (qi - ki + B).astype(F32)
    step_cur = (qi - ki).astype(F32)

    ones = jnp.ones((2 * B, LANES), BF16)
    for g, dil in enumerate(A_DILATIONS):
        q_ref, kc_ref, kp_ref, vc_ref, vp_ref = in_refs[5 * g:5 * g + 5]
        slope = sl_ref[g, h] * float(dil)
        bias_cur = jnp.where(valid_cur, -slope * step_cur, NEG_INF)
        bias_prev = jnp.where(valid_prev, -slope * step_prev, NEG_INF)
        bias = jnp.concatenate([bias_prev, bias_cur], axis=1)
        bias_first = jnp.concatenate([jnp.where(n > 0, bias_prev, NEG_INF), bias_cur], axis=1)
        nblk = A_SUPER // (B * dil)
        for r in range(dil):
            for blk in range(nblk):
                start = r + dil * B * blk
                rows = _rows(start, B, dil)
                qb = q_ref[rows, :].astype(BF16)
                if blk > 0:
                    prow = _rows(start - dil * B, B, dil)
                    kp, vp, bs = kc_ref[prow, :], vc_ref[prow, :], bias
                else:
                    prow = _rows(r + dil * B * (nblk - 1), B, dil)
                    kp, vp, bs = kp_ref[prow, :], vp_ref[prow, :], bias_first
                kcat = jnp.concatenate([kp, kc_ref[rows, :]], axis=0).astype(BF16)
                vcat = jnp.concatenate([vp, vc_ref[rows, :]], axis=0).astype(BF16)
                s = _dot_nt(qb, kcat) * ATT_SCALE + bs
                mx = jnp.max(s, axis=1, keepdims=True)
                e = jnp.exp(s - mx).astype(BF16)
                pv = _dot(e, jnp.concatenate([vcat, ones], axis=1))
                den = pv[:, A_DH:]
                o_scr[g, rows, :] = pv[:, :A_DH] * (1.0 / den)
                l_scr[g, rows, :] = mx + jnp.log(den)

    def merge(i, carry):
        rows = pl.ds(pl.multiple_of(i * B, B), B)
        lses = [l_scr[g, rows, :] for g in range(A_GROUPS)]
        outs = [o_scr[g, rows, :] for g in range(A_GROUPS)]
        a_ref[rows, :] = _group_softmax_merge(lses, outs)
        return carry

    lax.fori_loop(0, A_SUPER // B, merge, 0)


def _attn_prompt(z, slopes, s_len, t_all, aq_blk, ak_blk, av_blk):
    R = A_SUPER
    in_specs = [pl.BlockSpec(memory_space=pltpu.SMEM)]
    args = [slopes]
    for g in range(A_GROUPS):
        off = g * A_HEADS
        cur = lambda h, n, base: (n, base + h)
        prev = lambda h, n, base: (jnp.maximum(n - 1, 0), base + h)
        for base, fn in ((aq_blk + off, cur), (ak_blk + off, cur), (ak_blk + off, prev), (av_blk + off, cur), (av_blk + off, prev)):
            in_specs.append(pl.BlockSpec((R, A_DH), functools.partial(fn, base=base)))
            args.append(z)
    in_specs.append(pl.BlockSpec(memory_space=pl.ANY))
    args.append(jnp.zeros((t_all, A_HEADS * A_DH), F32))
    return pl.pallas_call(
        _attn_prompt_kernel,
        grid=(A_HEADS, s_len // R),
        in_specs=in_specs,
        out_specs=pl.BlockSpec((R, A_DH), lambda h, n: (n, h)),
        out_shape=jax.ShapeDtypeStruct((t_all, A_HEADS * A_DH), F32),
        scratch_shapes=[pltpu.VMEM((A_GROUPS, R, A_DH), F32), pltpu.VMEM((A_GROUPS, R, LANES), F32)],
        input_output_aliases={len(args) - 1: 0},
        compiler_params=_cparams("parallel", "arbitrary"),
        name="attn_prompt",
    )(*args)


def _attn_sample_kernel(sl_ref, *refs, seq):
    G = A_GROUPS
    z_refs = refs[:3 * G]
    buf_refs = refs[3 * G:4 * G]
    new_refs = refs[4 * G:5 * G]
    a_ref = refs[5 * G + 1]
    cache_out_refs = refs[5 * G + 2:6 * G + 2]
    sems = refs[6 * G + 2]
    b = pl.program_id(0)
    half = b % (SUBLANES // seq)
    PADK = LANES

    rows_per_pos = 2 * A_HEADS
    copies = []
    for g in range(G):
        kept = buf_refs[g].shape[1] - seq * rows_per_pos
        copies.append(pltpu.make_async_copy(buf_refs[g].at[0, pl.ds(seq * rows_per_pos, kept), :],
                                            cache_out_refs[g].at[b, pl.ds(0, kept), :], sems.at[2 * g]))
        copies.append(pltpu.make_async_copy(new_refs[g].at[0],
                                            cache_out_refs[g].at[b, pl.ds(kept, seq * rows_per_pos), :],
                                            sems.at[2 * g + 1]))
    for cp in copies:
        cp.start(priority=1)

    for h in range(A_HEADS):
        cols = slice(h * A_DH, (h + 1) * A_DH)
        lses, outs = [], []
        for g, dil in enumerate(A_DILATIONS):
            q_ref, kn_ref, vn_ref = z_refs[3 * g:3 * g + 3]
            buf_ref = buf_refs[g]
            lb = A_STEPS * dil
            slope = sl_ref[g, h]
            kbuf = buf_ref[0, pl.ds(h, lb, stride=2 * A_HEADS), :]
            vbuf = buf_ref[0, pl.ds(A_HEADS + h, lb, stride=2 * A_HEADS), :]
            pad = jnp.zeros((PADK - SUBLANES, A_DH), F32)
            kx = jnp.concatenate([kbuf, kn_ref[:, cols], pad], axis=0).astype(BF16)
            vx = jnp.concatenate([vbuf, vn_ref[:, cols], pad], axis=0).astype(BF16)
            s = _dot_nt(q_ref[:, cols].astype(BF16), kx) * ATT_SCALE
            col = lax.broadcasted_iota(jnp.int32, s.shape, 1)
            t = lax.broadcasted_iota(jnp.int32, s.shape, 0) - seq * half
            t_new = col - lb - seq * half
            in_buf = col < lb
            pos = jnp.where(in_buf, col, lb + t_new)
            delta = lb + t - pos
            valid = (in_buf | ((t_new >= 0) & (t_new < seq))) & (delta >= 0) & (delta <= A_STEPS * dil) \
                & ((delta & (dil - 1)) == 0)
            s = jnp.where(valid, s - slope * delta.astype(F32), NEG_INF)
            mx = jnp.max(s, axis=1, keepdims=True)
            e = jnp.exp(s - mx)
            den = jnp.sum(e, axis=1, keepdims=True)
            outs.append(_dot(e.astype(BF16), vx) * (1.0 / den))
            lses.append(mx + jnp.log(den))
        merged = _group_softmax_merge(lses, outs)
        for hv in range(SUBLANES // seq):
            @pl.when(half == hv)
            def _(merged=merged, hv=hv):
                a_ref[hv * seq:(hv + 1) * seq, cols] = merged[hv * seq:(hv + 1) * seq, :]

    for cp in copies:
        cp.wait()


def _attn_sample(z, slopes, bufs, kv_new, a_out, row0, aq_blk, ak_blk, av_blk, seq):
    B = bufs[0].shape[0]
    W = A_HEADS * A_DH
    per_blk = SUBLANES // seq
    rb0 = row0 // SUBLANES
    in_specs = [pl.BlockSpec(memory_space=pltpu.SMEM)]
    args = [slopes]
    for g in range(A_GROUPS):
        for base in (aq_blk + g, ak_blk + g, av_blk + g):
            in_specs.append(pl.BlockSpec((SUBLANES, W), functools.partial(lambda b, base: (rb0 + b // per_blk, base), base=base)))
            args.append(z)
    for arrs in (bufs, kv_new):
        for g in range(A_GROUPS):
            in_specs.append(pl.BlockSpec((1,) + arrs[g].shape[1:], lambda b: (b, 0, 0)))
            args.append(arrs[g])
    in_specs.append(pl.BlockSpec(memory_space=pl.ANY))
    args.append(a_out)
    outs = pl.pallas_call(
        functools.partial(_attn_sample_kernel, seq=seq),
        grid=(B,),
        in_specs=in_specs,
        out_specs=[pl.BlockSpec((SUBLANES, W), lambda b: (rb0 + b // per_blk, 0))]
        + [pl.BlockSpec(memory_space=pl.ANY)] * A_GROUPS,
        out_shape=[jax.ShapeDtypeStruct(a_out.shape, a_out.dtype)]
        + [jax.ShapeDtypeStruct(c.shape, c.dtype) for c in bufs],
        scratch_shapes=[pltpu.SemaphoreType.DMA((2 * A_GROUPS,))],
        input_output_aliases={len(args) - 1: 0},
        compiler_params=_cparams("arbitrary"),
        name="attn_sample",
    )(*args)
    return outs[0], outs[1:]


ST_TAU, ST_MAX0, ST_MAX1, ST_INVZ = 0, 1, 2, 3
ST_ROWS = SUBLANES
PEER_TOKEN_CHUNK = 256
PEER_KEY_ROWS = 32
PEER_KEY_GROUPS = PEER_KEY_ROWS // SUBLANES


def _take_max(sc, out, k):
    rows = lax.broadcasted_iota(jnp.int32, sc.shape, 0)
    out_rows = lax.broadcasted_iota(jnp.int32, out.shape, 0)
    m = jnp.max(sc, axis=0, keepdims=True)
    first = jnp.min(jnp.where(sc == m, rows, sc.shape[0]), axis=0, keepdims=True)
    return jnp.where(rows == first, NEG_INF, sc), jnp.where(out_rows == k, m, out)


def _pair_candidates(top0, top1):
    assert P_TOPK == 16 and SUBLANES == 8
    lo0, hi0, lo1, hi1 = top0[0:8], top0[8:16], top1[0:8], top1[8:16]
    skip = lax.broadcasted_iota(jnp.int32, lo0.shape, 0) < 4
    blocks = [top0[0:1] + lo1, top0[0:1] + hi1]
    blocks += [top0[i:i + 1] + lo1 for i in (1, 2, 3)]
    blocks += [jnp.where(skip, NEG_INF, lo0 + top1[0:1]), hi0 + top1[0:1]]
    blocks += [jnp.where(skip, NEG_INF, lo0 + top1[j:j + 1]) for j in (1, 2, 3)]
    return jnp.concatenate(blocks, axis=0)


N_PAIR_ROWS = 10 * SUBLANES


def _peer_scores_kernel(x_ref, wq_ref, sk_ref, sc_ref, st_ref, q_scr):
    tm = x_ref.shape[0]
    dkh = sk_ref.shape[2]
    q_scr[...] = _dot(x_ref[...], wq_ref[...]).astype(BF16)
    for hp in range(2 * P_HEADS):
        sc_ref[hp] = _dot_nt(sk_ref[hp], q_scr[:, hp * dkh:(hp + 1) * dkh])

    def per_chunk(ci, carry):
        lanes = pl.ds(pl.multiple_of(ci * LANES, LANES), LANES)
        empty = jnp.zeros((P_TOPK, LANES), F32)

        def per_head(h, carry):
            cand, max0, max1 = carry
            hh = jnp.minimum(h, P_HEADS - 1)
            sc0 = sc_ref[2 * hh, :, lanes]
            sc1 = sc_ref[2 * hh + 1, :, lanes]
            top0, top1, best = empty, empty, empty
            for k in range(P_TOPK):
                sc0, top0 = _take_max(sc0, top0, k)
                sc1, top1 = _take_max(sc1, top1, k)
                cand, best = _take_max(cand, best, k)
            z = jnp.sum(jnp.exp(best - best[0:1, :]), axis=0, keepdims=True)
            st_ref[jnp.maximum(h - 1, 0), :, lanes] = jnp.concatenate(
                [best[P_TOPK - 1:P_TOPK, :], max0, max1, 1.0 / z, jnp.zeros((ST_ROWS - 4, LANES), F32)], axis=0)
            return _pair_candidates(top0, top1), top0[0:1, :], top1[0:1, :]

        row = jnp.zeros((1, LANES), F32)
        lax.fori_loop(0, P_HEADS + 1, per_head, (jnp.zeros((N_PAIR_ROWS, LANES), F32), row, row))
        return carry

    lax.fori_loop(0, tm // LANES, per_chunk, 0)


def _peer_scores(xb, wq, sk, tm):
    T, D = xb.shape
    nq = wq.shape[1]
    return pl.pallas_call(
        _peer_scores_kernel,
        grid=(T // tm,),
        in_specs=[
            pl.BlockSpec((tm, D), lambda i: (i, 0)),
            pl.BlockSpec((D, nq), lambda i: (0, 0)),
            pl.BlockSpec(sk.shape, lambda i: (0, 0, 0)),
        ],
        out_specs=[
            pl.BlockSpec((2 * P_HEADS, P_NKEYS, tm), lambda i: (0, 0, i)),
            pl.BlockSpec((P_HEADS, ST_ROWS, tm), lambda i: (0, 0, i)),
        ],
        out_shape=[
            jax.ShapeDtypeStruct((2 * P_HEADS, P_NKEYS, T), F32),
            jax.ShapeDtypeStruct((P_HEADS, ST_ROWS, T), F32),
        ],
        scratch_shapes=[pltpu.VMEM((tm, nq), BF16)],
        compiler_params=_cparams("parallel"),
        name="peer_scores",
    )(xb, wq, sk)


def _gelu(a):
    return 0.5 * a * (1.0 + lax.erf(a * (1.0 / math.sqrt(2.0))))


def _peer_dense_kernel(x_ref, sc_ref, st_ref, u_ref, v_ref, y_ref,
                       e1_scr, tau_scr, row_scr, act_a, act_b, *at_scrs, n_e):
    e = pl.program_id(1)
    tm = x_ref.shape[0]
    te = u_ref.shape[0]
    K = P_NKEYS
    n_ig = te // K

    @pl.when(e == 0)
    def _():
        y_ref[...] = jnp.zeros_like(y_ref)
        act_b[...] = jnp.zeros_like(act_b)
        for h in range(P_HEADS):
            e1_scr[h] = jnp.exp(sc_ref[2 * h + 1] - st_ref[h, ST_MAX1:ST_MAX1 + 1, :])
            tau_scr[h] = jnp.broadcast_to(st_ref[h, ST_TAU:ST_TAU + 1, :], (SUBLANES, tm))

    chunks = [(c0, min(PEER_TOKEN_CHUNK, tm - c0)) for c0 in range(0, tm, PEER_TOKEN_CHUNK)]
    assert len(chunks) == len(at_scrs)
    d_model = v_ref.shape[1]
    n_parts = max(len(chunks) - 1, 1)
    part = d_model // n_parts

    def step(act_cur, act_prev):
        def pre_activations(k):
            c0, cw = chunks[k]
            at_scrs[k][...] = _dot_nt(u_ref[...], x_ref[c0:c0 + cw, :])

        def second_matmul(p):
            cols = slice(p * part, (p + 1) * part)
            y_ref[:, cols] += _dot_tn(act_prev[...], v_ref[:, cols])

        pre_activations(0)
        for p in range(n_parts):
            second_matmul(p)
            if p + 1 < len(chunks):
                pre_activations(p + 1)

        i0 = e * n_ig
        sub = lax.broadcasted_iota(jnp.int32, (SUBLANES, tm), 0)
        for ig in range(n_ig):
            i = i0 + ig
            i_grp = i // SUBLANES
            for h in range(P_HEADS):
                s0 = jnp.sum(jnp.where(sub == i - i_grp * SUBLANES, sc_ref[2 * h, i_grp], 0.0), axis=0, keepdims=True)
                w0 = jnp.exp(s0 - st_ref[h, ST_MAX0:ST_MAX0 + 1, :]) * st_ref[h, ST_INVZ:ST_INVZ + 1, :]
                row_scr[h, ig, 0] = jnp.broadcast_to(s0, (SUBLANES, tm))
                row_scr[h, ig, 1] = jnp.broadcast_to(w0, (SUBLANES, tm))

        for k, (c0, cw) in enumerate(chunks):
            at_scr = at_scrs[k]
            for l0 in range(c0, c0 + cw, LANES):
                lanes = slice(l0, l0 + LANES)
                at_lanes = slice(l0 - c0, l0 - c0 + LANES)
                for jq in range(K // PEER_KEY_ROWS):
                    jg = slice(jq * PEER_KEY_GROUPS, (jq + 1) * PEER_KEY_GROUPS)
                    gates = [jnp.zeros((PEER_KEY_GROUPS, SUBLANES, LANES), F32) for _ in range(n_ig)]
                    for h in range(P_HEADS):
                        s1 = sc_ref[2 * h + 1, jg, :, lanes]
                        e1 = e1_scr[h, jg, :, lanes]
                        tau = tau_scr[h, :, lanes]
                        for ig in range(n_ig):
                            total = s1 + row_scr[h, ig, 0, :, lanes]
                            gates[ig] = gates[ig] + jnp.where(total >= tau, e1 * row_scr[h, ig, 1, :, lanes], 0.0)
                    for ig in range(n_ig):
                        rows = slice(ig * K + jq * PEER_KEY_ROWS, ig * K + (jq + 1) * PEER_KEY_ROWS)
                        gate = gates[ig].reshape(PEER_KEY_ROWS, LANES)
                        act_cur[rows, lanes] = (_gelu(at_scr[rows, at_lanes]) * gate).astype(BF16)

    pl.when((e < n_e) & (e % 2 == 0))(lambda: step(act_a, act_b))
    pl.when((e < n_e) & (e % 2 == 1))(lambda: step(act_b, act_a))
    last_written = act_a if (n_e - 1) % 2 == 0 else act_b

    @pl.when(e == n_e)
    def _():
        y_ref[...] += _dot_tn(last_written[...], v_ref[...])


def _peer_dense(xb, sc, st, ub, vb, tm, te):
    T, D = xb.shape
    E = ub.shape[0]
    n_e = E // te
    once = pl.Buffered(1)
    return pl.pallas_call(
        functools.partial(_peer_dense_kernel, n_e=n_e),
        grid=(T // tm, n_e + 1),
        in_specs=[
            pl.BlockSpec((tm, D), lambda i, e: (i, 0), pipeline_mode=once),
            pl.BlockSpec((2 * P_HEADS, P_NKEYS // SUBLANES, SUBLANES, tm), lambda i, e: (0, 0, 0, i), pipeline_mode=once),
            pl.BlockSpec((P_HEADS, ST_ROWS, tm), lambda i, e: (0, 0, i)),
            pl.BlockSpec((te, D), lambda i, e: (jnp.minimum(e, n_e - 1), 0)),
            pl.BlockSpec((te, D), lambda i, e: (jnp.maximum(e - 1, 0), 0)),
        ],
        out_specs=pl.BlockSpec((tm, D), lambda i, e: (i, 0)),
        out_shape=jax.ShapeDtypeStruct((T, D), F32),
        scratch_shapes=[
            pltpu.VMEM((P_HEADS, P_NKEYS // SUBLANES, SUBLANES, tm), F32),
            pltpu.VMEM((P_HEADS, SUBLANES, tm), F32),
            pltpu.VMEM((P_HEADS, te // P_NKEYS, 2, SUBLANES, tm), F32),
            pltpu.VMEM((te, tm), BF16),
            pltpu.VMEM((te, tm), BF16),
        ] + [pltpu.VMEM((te, min(PEER_TOKEN_CHUNK, tm - c0)), F32) for c0 in range(0, tm, PEER_TOKEN_CHUNK)],
        compiler_params=_cparams("parallel", "arbitrary"),
        name="peer_dense",
    )(xb, sc.reshape(2 * P_HEADS, P_NKEYS // SUBLANES, SUBLANES, T), st, ub, vb)


def _tile(n, *candidates):
    for c in candidates:
        if n % c == 0:
            return c
    raise ValueError(f"no tile of {candidates} divides {n}")


def kernel(x_prompt, x_sample, state_C, state_n, state_m, cache_kv_w128, cache_kv_w512, cache_kv_w2048,
           w_in, b_in, mlstm_norm_w, w_branch_m, w_branch_a, w_out, ln1_g, ln1_b,
           w_query, sub_keys, expert_u, expert_v, ln2_g, ln2_b):
    bp, S, D = x_prompt.shape
    DB, DS, _ = x_sample.shape
    H = M_HEADS
    dv, dk = D // 8, D // 16
    AW = A_HEADS * A_DH
    caches = (cache_kv_w128, cache_kv_w512, cache_kv_w2048)
    assert bp == 1 and DB * DS == M_CHUNK and S % A_SUPER == 0
    assert all(c.shape[1] == A_STEPS * d for c, d in zip(caches, A_DILATIONS))
    T = S + DB * DS

    g_lo = 2 * H * dk + 2 * H * dv
    g_hi = g_lo + 2 * H
    w_in_t = w_in.T
    b_main = jnp.concatenate([b_in[:g_lo], b_in[g_hi:]])[None, :]
    w_gate_t = jnp.pad(w_in_t[g_lo:g_hi], ((0, LANES - 2 * H), (0, 0)))
    b_gate = jnp.pad(b_in[g_lo:g_hi], (0, LANES - 2 * H))[None, :]
    tm = _tile(T, 640, 128)
    x, xb, gates = _gates(x_prompt.reshape(S, D), x_sample.reshape(DB * DS, D), w_gate_t, b_gate, M_CHUNK)
    z = _proj(xb, w_in_t, b_main, _tile(T, 1040, 128), 1024, g_lo, g_hi)
    gates_t = gates[:, :2 * H].T
    aq_col, ak_col, av_col = g_lo, g_lo + A_GROUPS * AW, g_lo + 2 * A_GROUPS * AW
    gm_col = g_lo + 3 * A_GROUPS * AW
    ga_col = gm_col + D

    nw = mlstm_norm_w[None, :]
    m_out, c_p, n_p, m_p = _mlstm_prompt(z, gates, gates_t, nw, S, T)
    m0_rows = jnp.pad(jnp.repeat(state_m, DS, axis=0), ((0, 0), (0, LANES - H)))
    m_out, c_s, n_s, mt_s = _mlstm_sample(z, gates, gates_t, nw, m0_rows, state_C,
                                          state_n.reshape(DB * H, 1, dk), m_out, S)

    n_slopes = A_GROUPS * A_HEADS
    slopes = (2.0 ** (-8.0 * (jnp.arange(n_slopes, dtype=F32) + 1.0) / n_slopes)).reshape(A_GROUPS, A_HEADS)
    a_out = _attn_prompt(z, slopes, S, T, aq_col // A_DH, ak_col // A_DH, av_col // A_DH)
    bufs = [c.reshape(DB, c.shape[1] * 2 * A_HEADS, A_DH) for c in caches]
    kv_new = []
    for g in range(A_GROUPS):
        kg = z[S:, ak_col + g * AW:ak_col + (g + 1) * AW]
        vg = z[S:, av_col + g * AW:av_col + (g + 1) * AW]
        kv_new.append(jnp.stack([kg, vg], axis=1).reshape(DB, DS * 2 * A_HEADS, A_DH))
    a_out, kv_s = _attn_sample(z, slopes, bufs, kv_new, a_out, S, aq_col // AW, ak_col // AW, av_col // AW, DS)
    kv_s = [k.reshape(c.shape) for k, c in zip(kv_s, caches)]

    merged = _merge(m_out, a_out, w_branch_m.astype(BF16), w_branch_a.astype(BF16), z,
                    gm_col // 1024, ga_col // 1024, tm, 1024)
    x1, x1b = _outproj_ln(merged, w_out.astype(BF16), x, ln1_g[None, :], ln1_b[None, :], tm, 512)
    sk = sub_keys.reshape(2 * P_HEADS, P_NKEYS, sub_keys.shape[-1]).astype(BF16)
    sc, st = _peer_scores(x1b, w_query.astype(BF16), sk, tm)
    y = _peer_dense(x1b, sc, st, expert_u.astype(BF16), expert_v.astype(BF16), tm, 512)
    x2_prompt, x2_sample = _residual_ln(x1, y, ln2_g[None, :], ln2_b[None, :], M_CHUNK, S)

    y_prompt = x2_prompt.reshape(1, S, D)
    y_sample = x2_sample.reshape(DB, DS, D)
    kv_p = []
    for g, cache in enumerate(caches):
        win = cache.shape[1]
        kg = z[S - win:S, ak_col + g * AW:ak_col + (g + 1) * AW]
        vg = z[S - win:S, av_col + g * AW:av_col + (g + 1) * AW]
        kv_p.append(jnp.stack([kg, vg], axis=1).reshape(1, win, 2, A_HEADS, A_DH))
    m_s = mt_s[:, DS - 1::DS, 0].T
    return (y_prompt, y_sample, c_p, n_p.reshape(1, H, dk), m_p[:, 0, 0][None, :],
            kv_p[0], kv_p[1], kv_p[2],
            c_s, n_s.reshape(DB, H, dk), m_s, kv_s[0], kv_s[1], kv_s[2])
```
